```python
import jax, jax.numpy as jnp
from jax import lax
import numpy as np

D_MODEL = 4096
BATCH = 4
SEQ = 2048
DEPTH = 2
DEC_BATCH = 32
DEC_SEQ = 1
PAST_LEN = 16384
PAGE_SIZE = 128

N_MIXERS = 2
N_GMLP_LAYERS = (DEPTH + N_MIXERS - 1) // N_MIXERS
N_SWA_LAYERS = DEPTH // N_MIXERS
NORM_EPS = 1e-6
CHUNK = 128
D_GMLP = D_MODEL
N_GROUPS = 16
GROUP_DIM = D_GMLP // N_GROUPS
HEAD_DIM = 64
N_HEADS = D_MODEL // HEAD_DIM
N_KV_HEADS = 8
Q_PER_KV = N_HEADS // N_KV_HEADS
WINDOW = 128
BLOCK = WINDOW
ATTN_SCALE = HEAD_DIM ** -0.5
D_FF = 256 * ((8 * D_MODEL // 3 + 255) // 256)
CONV_W = 3

kernel_name = "hybrid_chunkgmlp_swa_sink_convffn_step"


def rms_norm(x, g):
    xf = x.astype(jnp.float32)
    y = xf * lax.rsqrt(jnp.mean(xf * xf, axis=-1, keepdims=True) + NORM_EPS)
    return (y * g.astype(jnp.float32)).astype(x.dtype)


def layer_norm(x, g, b):
    xf = x.astype(jnp.float32)
    xc = xf - jnp.mean(xf, axis=-1, keepdims=True)
    y = xc * lax.rsqrt(jnp.mean(xc * xc, axis=-1, keepdims=True) + NORM_EPS)
    return (y * g.astype(jnp.float32) + b.astype(jnp.float32)).astype(x.dtype)


def chunk_gmlp(x, w_in, ln_g, ln_b, w_s, b_s, w_out):
    n, L, _ = x.shape
    z = jax.nn.gelu(x @ w_in, approximate=False)
    u, v = jnp.split(z, 2, axis=-1)
    v = layer_norm(v, ln_g, ln_b)
    pad = (-L) % CHUNK
    nc = (L + pad) // CHUNK
    vc = jnp.pad(v, ((0, 0), (0, pad), (0, 0))).reshape(n, nc, CHUNK, N_GROUPS, GROUP_DIM)
    causal = jnp.tril(jnp.ones((CHUNK, CHUNK), dtype=bool))
    w_causal = jnp.where(causal, w_s, 0.0)
    s = jnp.einsum("gts,ncsgd->nctgd", w_causal, vc) + b_s.T[:, :, None]
    s = s.reshape(n, nc * CHUNK, D_GMLP)[:, :L]
    y = (u * s) @ w_out
    start = ((L - 1) // CHUNK) * CHUNK
    return y, v[:, start:]


def qkv_project(x, w_qkv, b_qkv):
    n, L, _ = x.shape
    qkv = x @ w_qkv + b_qkv
    q, k, v = jnp.split(qkv, [N_HEADS * HEAD_DIM, (N_HEADS + N_KV_HEADS) * HEAD_DIM], axis=-1)
    return (q.reshape(n, L, N_KV_HEADS, Q_PER_KV, HEAD_DIM),
            k.reshape(n, L, N_KV_HEADS, HEAD_DIM),
            v.reshape(n, L, N_KV_HEADS, HEAD_DIM))


def sink_softmax_attention(q, k, v, mask, sinks):
    s = jnp.einsum("...qhgd,...khd->...hgqk", q, k, preferred_element_type=jnp.float32) * ATTN_SCALE
    s = jnp.where(mask, s, -jnp.inf)
    sink = sinks.astype(jnp.float32).reshape(N_KV_HEADS, Q_PER_KV, 1, 1)
    m = jnp.maximum(jnp.max(s, axis=-1, keepdims=True), sink)
    p = jnp.exp(s - m)
    p = p / (jnp.sum(p, axis=-1, keepdims=True) + jnp.exp(sink - m))
    return jnp.einsum("...hgqk,...khd->...qhgd", p.astype(v.dtype), v)


def swa_prompt(x, w_qkv, b_qkv, sinks, w_o, b_o):
    n, L, _ = x.shape
    q, k, v = qkv_project(x, w_qkv, b_qkv)
    nb = L // BLOCK
    qb = q.reshape(n, nb, BLOCK, N_KV_HEADS, Q_PER_KV, HEAD_DIM)
    kb = k.reshape(n, nb, BLOCK, N_KV_HEADS, HEAD_DIM)
    vb = v.reshape(n, nb, BLOCK, N_KV_HEADS, HEAD_DIM)

    def with_prev(t):
        prev = jnp.pad(t, ((0, 0), (1, 0), (0, 0), (0, 0), (0, 0)))[:, :-1]
        return jnp.concatenate([prev, t], axis=2)

    i = jnp.arange(BLOCK)[:, None]
    j = jnp.arange(2 * BLOCK)[None, :]
    diff = BLOCK + i - j
    kpos = (jnp.arange(nb)[:, None, None] - 1) * BLOCK + j
    mask = ((diff >= 0) & (diff <= WINDOW) & (kpos >= 0))[:, None, None]
    o = sink_softmax_attention(qb, with_prev(kb), with_prev(vb), mask, sinks)
    y = o.reshape(n, L, N_HEADS * HEAD_DIM) @ w_o + b_o
    return y, k[:, L - WINDOW:], v[:, L - WINDOW:]


def swa_sample(x, win_k, win_v, w_qkv, b_qkv, sinks, w_o, b_o):
    n, T, _ = x.shape
    q, k, v = qkv_project(x, w_qkv, b_qkv)
    kk = jnp.concatenate([win_k.astype(k.dtype), k], axis=1)
    vv = jnp.concatenate([win_v.astype(v.dtype), v], axis=1)
    i = jnp.arange(T)[:, None]
    j = jnp.arange(WINDOW + T)[None, :]
    diff = i + WINDOW - j
    mask = (diff >= 0) & (diff <= WINDOW)
    o = sink_softmax_attention(q, kk, vv, mask, sinks)
    y = o.reshape(n, T, N_HEADS * HEAD_DIM) @ w_o + b_o
    return y, kk[:, T:], vv[:, T:]


def conv_ffn(x, prev, w_up, conv_w, conv_b, w_down):
    L = x.shape[1]
    h = x @ w_up
    hp = jnp.concatenate([prev.astype(h.dtype), h], axis=1)
    c = conv_b
    for t in range(CONV_W):
        c = c + conv_w[t] * hp[:, t:t + L]
    gate, val = jnp.split(c, 2, axis=-1)
    return (jax.nn.silu(gate) * val) @ w_down, hp[:, L:]


def setup_inputs(seed: int = 0) -> dict:
    key = jax.random.key(seed)
    ks = iter(jax.random.split(key, 32))

    def nrm(shape, scale):
        return jax.random.normal(next(ks), shape, jnp.float32) * scale

    def gain(shape):
        return 1.0 + nrm(shape, 0.02)

    qkv_width = (N_HEADS + 2 * N_KV_HEADS) * HEAD_DIM
    return {
        "x_prompt": nrm((BATCH, SEQ, D_MODEL), 1.0),
        "x_sample": nrm((DEC_BATCH, DEC_SEQ, D_MODEL), 1.0),
        "cache_win_k": nrm((N_SWA_LAYERS, DEC_BATCH, WINDOW, N_KV_HEADS, HEAD_DIM), 1.0),
        "cache_win_v": nrm((N_SWA_LAYERS, DEC_BATCH, WINDOW, N_KV_HEADS, HEAD_DIM), 1.0),
        "state_conv": nrm((DEPTH, DEC_BATCH, CONV_W - 1, 2 * D_FF), 1.0),
        "norm_mix_pre": gain((DEPTH, D_MODEL)),
        "norm_mix_post": gain((DEPTH, D_MODEL)),
        "norm_ffn_pre": gain((DEPTH, D_MODEL)),
        "norm_ffn_post": gain((DEPTH, D_MODEL)),
        "gmlp_w_in": nrm((N_GMLP_LAYERS, D_MODEL, 2 * D_GMLP), D_MODEL ** -0.5),
        "gmlp_ln_g": gain((N_GMLP_LAYERS, D_GMLP)),
        "gmlp_ln_b": nrm((N_GMLP_LAYERS, D_GMLP), 0.02),
        "gmlp_w_s": nrm((N_GMLP_LAYERS, N_GROUPS, CHUNK, CHUNK), CHUNK ** -0.5),
        "gmlp_b_s": 1.0 + nrm((N_GMLP_LAYERS, N_GROUPS, CHUNK), 0.1),
        "gmlp_w_out": nrm((N_GMLP_LAYERS, D_GMLP, D_MODEL), D_GMLP ** -0.5),
        "attn_w_qkv": nrm((N_SWA_LAYERS, D_MODEL, qkv_width), D_MODEL ** -0.5),
        "attn_b_qkv": nrm((N_SWA_LAYERS, qkv_width), 0.02),
        "attn_sinks": nrm((N_SWA_LAYERS, N_HEADS), 1.0),
        "attn_w_o": nrm((N_SWA_LAYERS, N_HEADS * HEAD_DIM, D_MODEL), (N_HEADS * HEAD_DIM) ** -0.5),
        "attn_b_o": nrm((N_SWA_LAYERS, D_MODEL), 0.02),
        "ffn_w_up": nrm((DEPTH, D_MODEL, 2 * D_FF), D_MODEL ** -0.5),
        "ffn_conv_w": nrm((DEPTH, CONV_W, 2 * D_FF), CONV_W ** -0.5),
        "ffn_conv_b": nrm((DEPTH, 2 * D_FF), 0.02),
        "ffn_w_down": nrm((DEPTH, D_FF, D_MODEL), D_FF ** -0.5),
    }


def reference(x_prompt, x_sample, cache_win_k, cache_win_v, state_conv,
              norm_mix_pre, norm_mix_post, norm_ffn_pre, norm_ffn_post,
              gmlp_w_in, gmlp_ln_g, gmlp_ln_b, gmlp_w_s, gmlp_b_s, gmlp_w_out,
              attn_w_qkv, attn_b_qkv, attn_sinks, attn_w_o, attn_b_o,
              ffn_w_up, ffn_conv_w, ffn_conv_b, ffn_w_down):
    xp, xs = x_prompt, x_sample
    gv_p, gv_s, wk_p, wv_p, wk_s, wv_s, cv_p, cv_s = [], [], [], [], [], [], [], []
    for layer in range(DEPTH):
        idx = layer // N_MIXERS
        hp = rms_norm(xp, norm_mix_pre[layer])
        hs = rms_norm(xs, norm_mix_pre[layer])
        if layer % N_MIXERS == 0:
            mp, g_p = chunk_gmlp(hp, gmlp_w_in[idx], gmlp_ln_g[idx], gmlp_ln_b[idx],
                                 gmlp_w_s[idx], gmlp_b_s[idx], gmlp_w_out[idx])
            ms, g_s = chunk_gmlp(hs, gmlp_w_in[idx], gmlp_ln_g[idx], gmlp_ln_b[idx],
                                 gmlp_w_s[idx], gmlp_b_s[idx], gmlp_w_out[idx])
            gv_p.append(g_p)
            gv_s.append(g_s)
        else:
            mp, k_p, v_p = swa_prompt(hp, attn_w_qkv[idx], attn_b_qkv[idx], attn_sinks[idx],
                                      attn_w_o[idx], attn_b_o[idx])
            ms, k_s, v_s = swa_sample(hs, cache_win_k[idx], cache_win_v[idx], attn_w_qkv[idx],
                                      attn_b_qkv[idx], attn_sinks[idx], attn_w_o[idx], attn_b_o[idx])
            wk_p.append(k_p)
            wv_p.append(v_p)
            wk_s.append(k_s)
            wv_s.append(v_s)
        xp = xp + rms_norm(mp, norm_mix_post[layer])
        xs = xs + rms_norm(ms, norm_mix_post[layer])

        zero_prev = jnp.zeros((xp.shape[0], CONV_W - 1, 2 * D_FF), xp.dtype)
        fp, c_p = conv_ffn(rms_norm(xp, norm_ffn_pre[layer]), zero_prev, ffn_w_up[layer],
                           ffn_conv_w[layer], ffn_conv_b[layer], ffn_w_down[layer])
        fs, c_s = conv_ffn(rms_norm(xs, norm_ffn_pre[layer]), state_conv[layer], ffn_w_up[layer],
                           ffn_conv_w[layer], ffn_conv_b[layer], ffn_w_down[layer])
        cv_p.append(c_p)
        cv_s.append(c_s)
        xp = xp + rms_norm(fp, norm_ffn_post[layer])
        xs = xs + rms_norm(fs, norm_ffn_post[layer])
    return (xp, xs, jnp.stack(gv_p), jnp.stack(gv_s), jnp.stack(wk_p), jnp.stack(wv_p),
            jnp.stack(wk_s), jnp.stack(wv_s), jnp.stack(cv_p), jnp.stack(cv_s))
```

```python
import functools
import math

import jax
import jax.numpy as jnp
from jax import lax
from jax.experimental import pallas as pl
from jax.experimental.pallas import tpu as pltpu

F32 = jnp.float32
BF16 = jnp.bfloat16

NORM_EPS = 1e-6
CHUNK = 128
N_GROUPS = 16
HEAD_DIM = 64
N_KV_HEADS = 8
Q_PER_KV = 8
WINDOW = 128
CONV_W = 3
ATTN_SCALE = HEAD_DIM ** -0.5
MASKED_SCORE = -1e30

V7X_LANES = 128
V7X_MXU_COLS = 256
V7X_SCOPED_VMEM_BYTES = 60000 * 1024


def _params(*semantics):
    return pltpu.CompilerParams(dimension_semantics=semantics,
                                vmem_limit_bytes=V7X_SCOPED_VMEM_BYTES)


def _rms(x, g):
    return x * lax.rsqrt(jnp.mean(x * x, axis=-1, keepdims=True) + NORM_EPS) * g


def _gelu(x):
    return 0.5 * x * (1.0 + lax.erf(x * math.sqrt(0.5)))


def _identity(x):
    return x


def _rmsnorm_kernel(x_ref, g_ref, h_ref):
    h_ref[...] = _rms(x_ref[...], g_ref[...]).astype(h_ref.dtype)


def rmsnorm_cast(x, g, *, tr):
    m, d = x.shape
    return pl.pallas_call(
        _rmsnorm_kernel,
        out_shape=jax.ShapeDtypeStruct((m, d), BF16),
        grid=(m // tr,),
        in_specs=[pl.BlockSpec((tr, d), lambda i: (i, 0)),
                  pl.BlockSpec((1, d), lambda i: (0, 0))],
        out_specs=pl.BlockSpec((tr, d), lambda i: (i, 0)),
        compiler_params=_params("parallel"),
        name="rmsnorm_cast",
    )(x, g.reshape(1, d))


def _resnorm_kernel(x_ref, y_ref, gpost_ref, gnext_ref, xo_ref, ho_ref):
    xn = x_ref[...] + _rms(y_ref[...], gpost_ref[...])
    xo_ref[...] = xn
    ho_ref[...] = _rms(xn, gnext_ref[...]).astype(ho_ref.dtype)


def _resnorm_last_kernel(x_ref, y_ref, gpost_ref, xo_ref):
    xo_ref[...] = x_ref[...] + _rms(y_ref[...], gpost_ref[...])


def resnorm(x, y, g_post, g_next, *, tr):
    m, d = x.shape
    row = pl.BlockSpec((tr, d), lambda i: (i, 0))
    vec = pl.BlockSpec((1, d), lambda i: (0, 0))
    if g_next is None:
        return pl.pallas_call(
            _resnorm_last_kernel,
            out_shape=jax.ShapeDtypeStruct((m, d), F32),
            grid=(m // tr,),
            in_specs=[row, row, vec],
            out_specs=row,
            compiler_params=_params("parallel"),
            name="resnorm_last",
        )(x, y, g_post.reshape(1, d)), None
    return pl.pallas_call(
        _resnorm_kernel,
        out_shape=(jax.ShapeDtypeStruct((m, d), F32), jax.ShapeDtypeStruct((m, d), BF16)),
        grid=(m // tr,),
        in_specs=[row, row, vec, vec],
        out_specs=(row, row),
        compiler_params=_params("parallel"),
        name="resnorm",
    )(x, y, g_post.reshape(1, d), g_next.reshape(1, d))


def _matmul_kernel(*refs, epilogue, has_bias):
    if has_bias:
        x_ref, w_ref, b_ref, o_ref = refs
    else:
        x_ref, w_ref, o_ref = refs
    acc = jnp.dot(x_ref[...], w_ref[...].astype(BF16), preferred_element_type=F32)
    if has_bias:
        acc = acc + b_ref[...]
    o_ref[...] = epilogue(acc).astype(o_ref.dtype)


def matmul(x, w, bias=None, *, tm, tn, n_out, col_block_offset=0, out_dtype=F32,
           epilogue=_identity, name):
    m, k = x.shape
    in_specs = [pl.BlockSpec((tm, k), lambda i, j: (i, 0)),
                pl.BlockSpec((k, tn), lambda i, j: (0, j + col_block_offset))]
    args = [x, w]
    if bias is not None:
        in_specs.append(pl.BlockSpec((1, tn), lambda i, j: (0, j + col_block_offset)))
        args.append(bias.reshape(1, -1))
    return pl.pallas_call(
        functools.partial(_matmul_kernel, epilogue=epilogue, has_bias=bias is not None),
        out_shape=jax.ShapeDtypeStruct((m, n_out), out_dtype),
        grid=(m // tm, n_out // tn),
        in_specs=in_specs,
        out_specs=pl.BlockSpec((tm, tn), lambda i, j: (i, j)),
        compiler_params=_params("parallel", "arbitrary"),
        name=name,
    )(*args)


def _layer_norm(v, g, b):
    xc = v - jnp.mean(v, axis=-1, keepdims=True)
    return xc * lax.rsqrt(jnp.mean(xc * xc, axis=-1, keepdims=True) + NORM_EPS) * g + b


def _spatial_kernel(v_ref, u_ref, lng_ref, lnb_ref, ws_ref, bias_ref, o_ref, gv_ref, *, group_dim):
    vn = _layer_norm(v_ref[...], lng_ref[...], lnb_ref[...])
    gv_ref[0] = vn
    vb = vn.astype(BF16)
    t = lax.broadcasted_iota(jnp.int32, (CHUNK, CHUNK), 0)
    s = lax.broadcasted_iota(jnp.int32, (CHUNK, CHUNK), 1)
    causal = s <= t
    for g in range(N_GROUPS):
        cols = slice(g * group_dim, (g + 1) * group_dim)
        wc = jnp.where(causal, ws_ref[g], 0.0).astype(BF16)
        mix = jnp.dot(wc, vb[:, cols], preferred_element_type=F32) + bias_ref[:, cols]
        o_ref[:, cols] = (u_ref[:, cols].astype(F32) * mix).astype(o_ref.dtype)


def spatial_gate_prompt(v, u, ln_g, ln_b, w_s, b_s, *, n_seq, seq_len):
    m, d = v.shape
    group_dim = d // N_GROUPS
    chunks_per_seq = seq_len // CHUNK
    bias_full = jnp.repeat(b_s.T, group_dim, axis=1)
    row = pl.BlockSpec((CHUNK, d), lambda c: (c, 0))
    vec = pl.BlockSpec((1, d), lambda c: (0, 0))
    return pl.pallas_call(
        functools.partial(_spatial_kernel, group_dim=group_dim),
        out_shape=(jax.ShapeDtypeStruct((m, d), BF16),
                   jax.ShapeDtypeStruct((n_seq, CHUNK, d), F32)),
        grid=(m // CHUNK,),
        in_specs=[row, row, vec, vec,
                  pl.BlockSpec((N_GROUPS, CHUNK, CHUNK), lambda c: (0, 0, 0)),
                  pl.BlockSpec((CHUNK, d), lambda c: (0, 0))],
        out_specs=(row, pl.BlockSpec((1, CHUNK, d), lambda c: (c // chunks_per_seq, 0, 0))),
        compiler_params=_params("arbitrary"),
        name="spatial_gate_prompt",
    )(v, u, ln_g.reshape(1, d), ln_b.reshape(1, d), w_s, bias_full)


def _spatial_sample_kernel(v_ref, u_ref, lng_ref, lnb_ref, w00_ref, b0_ref, o_ref, gv_ref):
    vn = _layer_norm(v_ref[...], lng_ref[...], lnb_ref[...])
    gv_ref[...] = vn
    mix = w00_ref[...].astype(BF16).astype(F32) * vn.astype(BF16).astype(F32) + b0_ref[...]
    o_ref[...] = (u_ref[...].astype(F32) * mix).astype(o_ref.dtype)


def spatial_gate_sample(v, u, ln_g, ln_b, w_s, b_s):
    m, d = v.shape
    group_dim = d // N_GROUPS
    w00 = jnp.repeat(w_s[:, 0, 0], group_dim).reshape(1, d)
    b0 = jnp.repeat(b_s[:, 0], group_dim).reshape(1, d)
    return pl.pallas_call(
        _spatial_sample_kernel,
        out_shape=(jax.ShapeDtypeStruct((m, d), BF16), jax.ShapeDtypeStruct((m, d), F32)),
        name="spatial_gate_sample",
    )(v, u, ln_g.reshape(1, d), ln_b.reshape(1, d), w00, b0)


def _pair_block_diag(pair, even):
    lane = lax.broadcasted_iota(jnp.int32, pair.shape, 1)
    if even:
        own = jnp.where(lane < HEAD_DIM, pair, 0.0)
        return jnp.concatenate([own, pltpu.roll(own, HEAD_DIM, 1)], axis=0)
    own = jnp.where(lane >= HEAD_DIM, pair, 0.0)
    return jnp.concatenate([pltpu.roll(own, HEAD_DIM, 1), own], axis=0)


def _attn_prompt_kernel(sink_ref, q_ref, kprev_ref, kown_ref, vprev_ref, vown_ref, o_ref):
    blk = pl.program_id(1)
    n_keys = 2 * WINDOW
    qi = lax.broadcasted_iota(jnp.int32, (WINDOW, n_keys), 0)
    kj = lax.broadcasted_iota(jnp.int32, (WINDOW, n_keys), 1)
    first_key = jnp.where(blk > 0, 0, WINDOW)
    valid = (kj >= qi) & (kj <= qi + WINDOW) & (kj >= first_key)
    col_blocks = Q_PER_KV * HEAD_DIM // V7X_LANES
    valid = jnp.concatenate([valid] * col_blocks, axis=0)
    block_of_row = lax.broadcasted_iota(jnp.int32, (col_blocks * WINDOW, 1), 0) // WINDOW
    group_w = Q_PER_KV * HEAD_DIM
    for pair in range(N_KV_HEADS // 2):
        lanes = slice(pair * V7X_LANES, (pair + 1) * V7X_LANES)
        k_pair = jnp.concatenate([kprev_ref[:, lanes], kown_ref[:, lanes]], axis=0)
        v_pair = jnp.concatenate([vprev_ref[:, lanes], vown_ref[:, lanes]], axis=0)
        for e in range(2):
            h = 2 * pair + e
            k2 = _pair_block_diag(k_pair, e == 0).astype(BF16)
            v2 = _pair_block_diag(v_pair, e == 0).astype(BF16)
            qs = jnp.concatenate(
                [q_ref[:, h * group_w + c * V7X_LANES: h * group_w + (c + 1) * V7X_LANES]
                 for c in range(col_blocks)], axis=0)
            qs = (qs * ATTN_SCALE).astype(BF16)
            s = lax.dot_general(qs, k2, (((1,), (1,)), ((), ())), preferred_element_type=F32)
            probs = []
            for half in range(2):
                sh = jnp.where(valid, s[:, half * n_keys:(half + 1) * n_keys], MASKED_SCORE)
                sink = jnp.zeros((col_blocks * WINDOW, 1), F32)
                for c in range(col_blocks):
                    sink = jnp.where(block_of_row == c, sink_ref[h * Q_PER_KV + 2 * c + half], sink)
                m = jnp.maximum(jnp.max(sh, axis=-1, keepdims=True), sink)
                p = jnp.exp(sh - m)
                p = p / (jnp.sum(p, axis=-1, keepdims=True) + jnp.exp(sink - m))
                probs.append(p.astype(BF16))
            o = jnp.dot(jnp.concatenate(probs, axis=1), v2, preferred_element_type=F32)
            for c in range(col_blocks):
                o_ref[:, h * group_w + c * V7X_LANES: h * group_w + (c + 1) * V7X_LANES] = (
                    o[c * WINDOW:(c + 1) * WINDOW].astype(o_ref.dtype))


def attention_prompt(qkv, sinks, *, n_seq, seq_len):
    m = qkv.shape[0]
    d_q = N_KV_HEADS * Q_PER_KV * HEAD_DIM
    d_kv = N_KV_HEADS * HEAD_DIM
    nb = seq_len // WINDOW
    k_col = d_q // d_kv
    v_col = k_col + 1

    def own(b, i):
        return b * nb + i

    def prev(b, i):
        return b * nb + jnp.maximum(i - 1, 0)

    return pl.pallas_call(
        _attn_prompt_kernel,
        out_shape=jax.ShapeDtypeStruct((m, d_q), BF16),
        grid=(n_seq, nb),
        in_specs=[pl.BlockSpec(memory_space=pltpu.SMEM),
                  pl.BlockSpec((WINDOW, d_q), lambda b, i: (own(b, i), 0)),
                  pl.BlockSpec((WINDOW, d_kv), lambda b, i: (prev(b, i), k_col)),
                  pl.BlockSpec((WINDOW, d_kv), lambda b, i: (own(b, i), k_col)),
                  pl.BlockSpec((WINDOW, d_kv), lambda b, i: (prev(b, i), v_col)),
                  pl.BlockSpec((WINDOW, d_kv), lambda b, i: (own(b, i), v_col))],
        out_specs=pl.BlockSpec((WINDOW, d_q), lambda b, i: (own(b, i), 0)),
        compiler_params=_params("parallel", "arbitrary"),
        name="attention_prompt",
    )(sinks, qkv, qkv, qkv, qkv, qkv)


def _attn_sample_kernel(q2_ref, ck_ref, cv_ref, knew_ref, vnew_ref, sink_ref, o_ref):
    for pair in range(N_KV_HEADS // 2):
        lanes = slice(pair * V7X_LANES, (pair + 1) * V7X_LANES)
        q2 = q2_ref[0, pair]
        k_pair = ck_ref[0, :, lanes].astype(BF16)
        v_pair = cv_ref[0, :, lanes].astype(BF16)
        k_new = knew_ref[0, :, lanes].astype(BF16).astype(F32)
        v_new = vnew_ref[0, :, lanes].astype(BF16).astype(F32)
        sink = sink_ref[pair][:, :1]
        s = lax.dot_general(q2, k_pair, (((1,), (1,)), ((), ())), preferred_element_type=F32)
        s_new = jnp.sum(q2.astype(F32) * k_new, axis=-1, keepdims=True)
        m = jnp.maximum(jnp.maximum(jnp.max(s, axis=-1, keepdims=True), s_new), sink)
        p = jnp.exp(s - m)
        p_new = jnp.exp(s_new - m)
        denom = jnp.sum(p, axis=-1, keepdims=True) + p_new + jnp.exp(sink - m)
        o = jnp.dot((p / denom).astype(BF16), v_pair, preferred_element_type=F32)
        o_ref[0, pair] = o + (p_new / denom).astype(BF16).astype(F32) * v_new


def attention_sample(qkv, cache_k, cache_v, sinks):
    n = qkv.shape[0]
    d_q = N_KV_HEADS * Q_PER_KV * HEAD_DIM
    d_kv = N_KV_HEADS * HEAD_DIM
    n_pairs = N_KV_HEADS // 2
    rows = 2 * Q_PER_KV
    q = (qkv[:, :d_q] * ATTN_SCALE).astype(BF16).reshape(n, n_pairs, 2, Q_PER_KV, HEAD_DIM)
    zeros = jnp.zeros((n, n_pairs, Q_PER_KV, HEAD_DIM), BF16)
    q2 = jnp.concatenate([jnp.concatenate([q[:, :, 0], zeros], axis=-1),
                          jnp.concatenate([zeros, q[:, :, 1]], axis=-1)], axis=2)
    k_new = qkv[:, d_q:d_q + d_kv].reshape(n, 1, d_kv)
    v_new = qkv[:, d_q + d_kv:].reshape(n, 1, d_kv)
    sink2 = jnp.broadcast_to(sinks.reshape(n_pairs, rows, 1), (n_pairs, rows, V7X_LANES))
    o2 = pl.pallas_call(
        _attn_sample_kernel,
        out_shape=jax.ShapeDtypeStruct((n, n_pairs, rows, V7X_LANES), F32),
        grid=(n,),
        in_specs=[pl.BlockSpec((1, n_pairs, rows, V7X_LANES), lambda b: (b, 0, 0, 0)),
                  pl.BlockSpec((1, WINDOW, d_kv), lambda b: (b, 0, 0)),
                  pl.BlockSpec((1, WINDOW, d_kv), lambda b: (b, 0, 0)),
                  pl.BlockSpec((1, 1, d_kv), lambda b: (b, 0, 0)),
                  pl.BlockSpec((1, 1, d_kv), lambda b: (b, 0, 0)),
                  pl.BlockSpec((n_pairs, rows, V7X_LANES), lambda b: (0, 0, 0))],
        out_specs=pl.BlockSpec((1, n_pairs, rows, V7X_LANES), lambda b: (b, 0, 0, 0)),
        compiler_params=_params("parallel"),
        name="attention_sample",
    )(q2, cache_k.reshape(n, WINDOW, d_kv), cache_v.reshape(n, WINDOW, d_kv), k_new, v_new, sink2)
    o = jnp.stack([o2[:, :, :Q_PER_KV, :HEAD_DIM], o2[:, :, Q_PER_KV:, HEAD_DIM:]], axis=2)
    return o.reshape(n, d_q).astype(BF16), k_new, v_new


def _silu_gate(gate, val):
    return gate * (1.0 / (1.0 + jnp.exp(-gate))) * val


def _ffn_up_prompt_kernel(x_ref, wg_ref, wv_ref, cwg_ref, cwv_ref, cbg_ref, cbv_ref,
                          a_ref, sg_ref, sv_ref):
    x = x_ref[...]
    hg = jnp.dot(x, wg_ref[...].astype(BF16), preferred_element_type=F32)
    hv = jnp.dot(x, wv_ref[...].astype(BF16), preferred_element_type=F32)
    row = lax.broadcasted_iota(jnp.int32, hg.shape, 0)

    def conv(h, cw_ref, cb_ref):
        h1 = jnp.where(row >= 1, pltpu.roll(h, 1, 0), 0.0)
        h2 = jnp.where(row >= 2, pltpu.roll(h, 2, 0), 0.0)
        return cb_ref[...] + cw_ref[0:1] * h2 + cw_ref[1:2] * h1 + cw_ref[2:3] * h

    a_ref[...] = _silu_gate(conv(hg, cwg_ref, cbg_ref), conv(hv, cwv_ref, cbv_ref)).astype(a_ref.dtype)
    tail = sg_ref.shape[1]
    sg_ref[0] = hg[hg.shape[0] - tail:]
    sv_ref[0] = hv[hv.shape[0] - tail:]


def ffn_up_prompt(h, w_up, conv_w, conv_b, *, n_seq, seq_len, tn, tail_rows):
    m, k = h.shape
    d_ff = w_up.shape[1] // 2
    nj = d_ff // tn
    x_spec = pl.BlockSpec((seq_len, k), lambda i, j: (i, 0), pipeline_mode=pl.Buffered(1))
    gate_w = pl.BlockSpec((k, tn), lambda i, j: (0, j))
    val_w = pl.BlockSpec((k, tn), lambda i, j: (0, j + nj))
    gate_cw = pl.BlockSpec((CONV_W, tn), lambda i, j: (0, j))
    val_cw = pl.BlockSpec((CONV_W, tn), lambda i, j: (0, j + nj))
    gate_cb = pl.BlockSpec((1, tn), lambda i, j: (0, j))
    val_cb = pl.BlockSpec((1, tn), lambda i, j: (0, j + nj))
    state = pl.BlockSpec((1, tail_rows, tn), lambda i, j: (i, 0, j))
    conv_b = conv_b.reshape(1, -1)
    a, sg, sv = pl.pallas_call(
        _ffn_up_prompt_kernel,
        out_shape=(jax.ShapeDtypeStruct((m, d_ff), BF16),
                   jax.ShapeDtypeStruct((n_seq, tail_rows, d_ff), F32),
                   jax.ShapeDtypeStruct((n_seq, tail_rows, d_ff), F32)),
        grid=(n_seq, nj),
        in_specs=[x_spec, gate_w, val_w, gate_cw, val_cw, gate_cb, val_cb],
        out_specs=(pl.BlockSpec((seq_len, tn), lambda i, j: (i, j)), state, state),
        compiler_params=_params("parallel", "arbitrary"),
        name="ffn_up_prompt",
    )(h, w_up, w_up, conv_w, conv_w, conv_b, conv_b)
    keep = slice(tail_rows - (CONV_W - 1), tail_rows)
    return a, jnp.concatenate([sg[:, keep], sv[:, keep]], axis=-1)


def _conv_gate_sample_kernel(hg_ref, hv_ref, p0g_ref, p0v_ref, p1g_ref, p1v_ref,
                             cwg_ref, cwv_ref, cbg_ref, cbv_ref, a_ref):
    def conv(h_ref, p0_ref, p1_ref, cw_ref, cb_ref):
        return cb_ref[...] + cw_ref[0:1] * p0_ref[...] + cw_ref[1:2] * p1_ref[...] + cw_ref[2:3] * h_ref[...]

    a_ref[...] = _silu_gate(conv(hg_ref, p0g_ref, p1g_ref, cwg_ref, cbg_ref),
                            conv(hv_ref, p0v_ref, p1v_ref, cwv_ref, cbv_ref)).astype(a_ref.dtype)


def conv_gate_sample(hu, prev, conv_w, conv_b, *, tn):
    n, two_ff = hu.shape
    d_ff = two_ff // 2
    nj = d_ff // tn
    gate = pl.BlockSpec((n, tn), lambda j: (0, j))
    val = pl.BlockSpec((n, tn), lambda j: (0, j + nj))
    return pl.pallas_call(
        _conv_gate_sample_kernel,
        out_shape=jax.ShapeDtypeStruct((n, d_ff), BF16),
        grid=(nj,),
        in_specs=[gate, val, gate, val, gate, val,
                  pl.BlockSpec((CONV_W, tn), lambda j: (0, j)),
                  pl.BlockSpec((CONV_W, tn), lambda j: (0, j + nj)),
                  pl.BlockSpec((1, tn), lambda j: (0, j)),
                  pl.BlockSpec((1, tn), lambda j: (0, j + nj))],
        out_specs=pl.BlockSpec((n, tn), lambda j: (0, j)),
        compiler_params=_params("parallel"),
        name="conv_gate_sample",
    )(hu, hu, prev[:, 0], prev[:, 0], prev[:, 1], prev[:, 1],
      conv_w, conv_w, conv_b.reshape(1, -1), conv_b.reshape(1, -1))


TM_PROMPT = 1024
TM_PROMPT_DOWN = 512
TN = 512
TN_FFN_UP = V7X_MXU_COLS
TR_ROWWISE = 256
STATE_TAIL_ROWS = 8


def kernel(x_prompt, x_sample, cache_win_k, cache_win_v, state_conv, norm_mix_pre, norm_mix_post,
           norm_ffn_pre, norm_ffn_post, gmlp_w_in, gmlp_ln_g, gmlp_ln_b, gmlp_w_s, gmlp_b_s,
           gmlp_w_out, attn_w_qkv, attn_b_qkv, attn_sinks, attn_w_o, attn_b_o, ffn_w_up,
           ffn_conv_w, ffn_conv_b, ffn_w_down):
    n_seq, seq_len, d = x_prompt.shape
    n_dec = x_sample.shape[0]
    depth = norm_mix_pre.shape[0]
    d_ff = ffn_w_down.shape[1]
    mp = n_seq * seq_len
    xp = x_prompt.reshape(mp, d)
    xs = x_sample.reshape(n_dec, d)

    def mm_both(hp, hs, w, bias=None, *, tm, n_out, col_block_offset=0, out_dtype=F32,
                epilogue=_identity, name):
        kw = dict(tn=TN, n_out=n_out, col_block_offset=col_block_offset, out_dtype=out_dtype,
                  epilogue=epilogue)
        return (matmul(hp, w, bias, tm=tm, name=name + "_prompt", **kw),
                matmul(hs, w, bias, tm=n_dec, name=name + "_sample", **kw))

    hp = rmsnorm_cast(xp, norm_mix_pre[0], tr=TR_ROWWISE)
    hs = rmsnorm_cast(xs, norm_mix_pre[0], tr=n_dec)
    gv_p, gv_s, wk_p, wv_p, wk_s, wv_s, cv_p, cv_s = [], [], [], [], [], [], [], []
    for layer in range(depth):
        idx = layer // 2
        if layer % 2 == 0:
            w_in = gmlp_w_in[idx].astype(BF16)
            w_out = gmlp_w_out[idx].astype(BF16)
            d_g = w_out.shape[0]
            up, us = mm_both(hp, hs, w_in, tm=TM_PROMPT, n_out=d_g, out_dtype=BF16, epilogue=_gelu,
                             name="gmlp_in_u")
            vp, vs = mm_both(hp, hs, w_in, tm=TM_PROMPT, n_out=d_g, col_block_offset=d_g // TN,
                             epilogue=_gelu, name="gmlp_in_v")
            gp, g_p = spatial_gate_prompt(vp, up, gmlp_ln_g[idx], gmlp_ln_b[idx], gmlp_w_s[idx],
                                          gmlp_b_s[idx], n_seq=n_seq, seq_len=seq_len)
            gs, g_s = spatial_gate_sample(vs, us, gmlp_ln_g[idx], gmlp_ln_b[idx], gmlp_w_s[idx],
                                          gmlp_b_s[idx])
            gv_p.append(g_p)
            gv_s.append(g_s.reshape(n_dec, 1, d_g))
            yp, ys = mm_both(gp, gs, w_out, tm=TM_PROMPT, n_out=d, name="gmlp_out")
        else:
            w_qkv = attn_w_qkv[idx].astype(BF16)
            w_o = attn_w_o[idx].astype(BF16)
            d_q = w_o.shape[0]
            d_kv = (w_qkv.shape[1] - d_q) // 2
            qkv_p, qkv_s = mm_both(hp, hs, w_qkv, attn_b_qkv[idx], tm=TM_PROMPT, n_out=w_qkv.shape[1],
                                   name="attn_qkv")
            op = attention_prompt(qkv_p, attn_sinks[idx], n_seq=n_seq, seq_len=seq_len)
            os_, k_new, v_new = attention_sample(qkv_s, cache_win_k[idx], cache_win_v[idx], attn_sinks[idx])
            tail = qkv_p.reshape(n_seq, seq_len, -1)[:, seq_len - WINDOW:]
            wk_p.append(tail[:, :, d_q:d_q + d_kv].reshape(n_seq, WINDOW, N_KV_HEADS, HEAD_DIM))
            wv_p.append(tail[:, :, d_q + d_kv:].reshape(n_seq, WINDOW, N_KV_HEADS, HEAD_DIM))
            wk_s.append(jnp.concatenate(
                [cache_win_k[idx][:, 1:], k_new.reshape(n_dec, 1, N_KV_HEADS, HEAD_DIM)], axis=1))
            wv_s.append(jnp.concatenate(
                [cache_win_v[idx][:, 1:], v_new.reshape(n_dec, 1, N_KV_HEADS, HEAD_DIM)], axis=1))
            yp, ys = mm_both(op, os_, w_o, attn_b_o[idx], tm=TM_PROMPT, n_out=d, name="attn_out")
        xp, hp = resnorm(xp, yp, norm_mix_post[layer], norm_ffn_pre[layer], tr=TR_ROWWISE)
        xs, hs = resnorm(xs, ys, norm_mix_post[layer], norm_ffn_pre[layer], tr=n_dec)

        w_up = ffn_w_up[layer].astype(BF16)
        w_down = ffn_w_down[layer].astype(BF16)
        ap, c_p = ffn_up_prompt(hp, w_up, ffn_conv_w[layer], ffn_conv_b[layer], n_seq=n_seq,
                                seq_len=seq_len, tn=TN_FFN_UP, tail_rows=STATE_TAIL_ROWS)
        hu_s = matmul(hs, w_up, tm=n_dec, tn=TN, n_out=2 * d_ff, name="ffn_up_sample")
        as_ = conv_gate_sample(hu_s, state_conv[layer], ffn_conv_w[layer], ffn_conv_b[layer],
                               tn=TN_FFN_UP)
        cv_p.append(c_p)
        cv_s.append(jnp.concatenate([state_conv[layer][:, 1:], hu_s[:, None]], axis=1))
        fp, fs = mm_both(ap, as_, w_down, tm=TM_PROMPT_DOWN, n_out=d, name="ffn_down")
        g_next = norm_mix_pre[layer + 1] if layer + 1 < depth else None
        xp, hp = resnorm(xp, fp, norm_ffn_post[layer], g_next, tr=TR_ROWWISE)
        xs, hs = resnorm(xs, fs, norm_ffn_post[layer], g_next, tr=n_dec)

    return (xp.reshape(n_seq, seq_len, d), xs.reshape(n_dec, 1, d), jnp.stack(gv_p), jnp.stack(gv_s),
            jnp.stack(wk_p), jnp.stack(wv_p), jnp.stack(wk_s), jnp.stack(wv_s),
            jnp.stack(cv_p), jnp.stack(cv_s))
```

```python
import functools
import math

import jax
import jax.numpy as jnp
from jax import lax
from jax.experimental import pallas as pl
from jax.experimental.pallas import tpu as pltpu

F32 = jnp.float32
BF16 = jnp.bfloat16

NORM_EPS = 1e-6
CHUNK = 128
N_GROUPS = 16
HEAD_DIM = 64
N_KV_HEADS = 8
Q_PER_KV = 8
WINDOW = 128
CONV_W = 3
ATTN_SCALE = HEAD_DIM ** -0.5
MASKED_SCORE = -1e30

V7X_LANES = 128
V7X_SUBLANES = 8
V7X_MXU_COLS = 256
V7X_SCOPED_VMEM_BYTES = 60000 * 1024

DOT_ROWS = 512


def _params(*semantics):
    return pltpu.CompilerParams(dimension_semantics=semantics,
                                vmem_limit_bytes=V7X_SCOPED_VMEM_BYTES)


def _rms(x, g):
    return x * lax.rsqrt(jnp.mean(x * x, axis=-1, keepdims=True) + NORM_EPS) * g


def _gelu(x):
    return 0.5 * x * (1.0 + lax.erf(x * math.sqrt(0.5)))


def _identity(x):
    return x


def _dot_bf16(x, w):
    return jnp.dot(x, w.astype(BF16), preferred_element_type=F32)


def _row_chunks(rows):
    n = max(rows // DOT_ROWS, 1)
    return [(c * DOT_ROWS, (c + 1) * DOT_ROWS if c + 1 < n else rows) for c in range(n)]


def _skip_aliased(body, n_aliased):
    def kernel_fn(*refs):
        body(*refs[n_aliased:])
    return kernel_fn


def _rmsnorm_kernel(x_ref, g_ref, h_ref):
    h_ref[0] = _rms(x_ref[0], g_ref[...]).astype(h_ref.dtype)


def _resnorm_kernel(x_ref, y_ref, gpost_ref, gnext_ref, xo_ref, ho_ref):
    xn = x_ref[0] + _rms(y_ref[0], gpost_ref[...])
    xo_ref[0] = xn
    ho_ref[0] = _rms(xn, gnext_ref[...]).astype(ho_ref.dtype)


def _resnorm_last_kernel(x_ref, y_ref, gpost_ref, xo_ref):
    xo_ref[0] = x_ref[0] + _rms(y_ref[0], gpost_ref[...])


def rmsnorm_first(x_prompt, x_sample, g, *, tr):
    n_seq, seq_len, d = x_prompt.shape
    n_dec = x_sample.shape[1]
    g = g.reshape(1, d)
    shape = jax.ShapeDtypeStruct((n_seq, seq_len + n_dec, d), BF16)
    row = pl.BlockSpec((1, tr, d), lambda s, r: (s, r, 0))
    h = pl.pallas_call(
        _rmsnorm_kernel, out_shape=shape, grid=(n_seq, seq_len // tr),
        in_specs=[row, pl.BlockSpec((1, d), lambda s, r: (0, 0))], out_specs=row,
        compiler_params=_params("parallel", "parallel"), name="rmsnorm_first_prompt",
    )(x_prompt, g)
    return pl.pallas_call(
        _skip_aliased(_rmsnorm_kernel, 1), out_shape=shape, grid=(n_seq,),
        in_specs=[pl.BlockSpec(memory_space=pl.ANY),
                  pl.BlockSpec((1, n_dec, d), lambda s: (0, 0, 0)),
                  pl.BlockSpec((1, d), lambda s: (0, 0))],
        out_specs=pl.BlockSpec((1, n_dec, d), lambda s: (s, seq_len // n_dec, 0)),
        input_output_aliases={0: 0},
        compiler_params=_params("arbitrary"), name="rmsnorm_first_sample",
    )(h, x_sample, g)


def resnorm_first(x_prompt, x_sample, y, g_post, g_next, *, tr):
    n_seq, seq_len, d = x_prompt.shape
    n_dec = x_sample.shape[1]
    g_post, g_next = g_post.reshape(1, d), g_next.reshape(1, d)
    shapes = (jax.ShapeDtypeStruct(y.shape, F32), jax.ShapeDtypeStruct(y.shape, BF16))
    row = pl.BlockSpec((1, tr, d), lambda s, r: (s, r, 0))
    vec = pl.BlockSpec((1, d), lambda s, r: (0, 0))
    xo, ho = pl.pallas_call(
        _resnorm_kernel, out_shape=shapes, grid=(n_seq, seq_len // tr),
        in_specs=[row, row, vec, vec], out_specs=(row, row),
        compiler_params=_params("parallel", "parallel"), name="resnorm_first_prompt",
    )(x_prompt, y, g_post, g_next)
    sample_rows = pl.BlockSpec((1, n_dec, d), lambda s: (s, seq_len // n_dec, 0))
    vec1 = pl.BlockSpec((1, d), lambda s: (0, 0))
    return pl.pallas_call(
        _skip_aliased(_resnorm_kernel, 2), out_shape=shapes, grid=(n_seq,),
        in_specs=[pl.BlockSpec(memory_space=pl.ANY), pl.BlockSpec(memory_space=pl.ANY),
                  pl.BlockSpec((1, n_dec, d), lambda s: (0, 0, 0)),
                  pl.BlockSpec((1, n_dec, d), lambda s: (0, seq_len // n_dec, 0)), vec1, vec1],
        out_specs=(sample_rows, sample_rows),
        input_output_aliases={0: 0, 1: 1},
        compiler_params=_params("arbitrary"), name="resnorm_first_sample",
    )(xo, ho, x_sample, y, g_post, g_next)


def resnorm_tiled(x, y, g_post, g_next, *, tr):
    n_seq, rows, d = x.shape
    row = pl.BlockSpec((1, tr, d), lambda s, r: (s, r, 0))
    vec = pl.BlockSpec((1, d), lambda s, r: (0, 0))
    return pl.pallas_call(
        _resnorm_kernel,
        out_shape=(jax.ShapeDtypeStruct(x.shape, F32), jax.ShapeDtypeStruct(x.shape, BF16)),
        grid=(n_seq, rows // tr),
        in_specs=[row, row, vec, vec], out_specs=(row, row),
        compiler_params=_params("parallel", "parallel"), name="resnorm_tiled",
    )(x, y, g_post.reshape(1, d), g_next.reshape(1, d))


def resnorm_last(x, y, g_post, *, seq_len, n_dec, tr):
    n_seq, _, d = x.shape
    g_post = g_post.reshape(1, d)
    row = pl.BlockSpec((1, tr, d), lambda s, r: (s, r, 0))
    y_prompt = pl.pallas_call(
        _resnorm_last_kernel, out_shape=jax.ShapeDtypeStruct((n_seq, seq_len, d), F32),
        grid=(n_seq, seq_len // tr),
        in_specs=[row, row, pl.BlockSpec((1, d), lambda s, r: (0, 0))], out_specs=row,
        compiler_params=_params("parallel", "parallel"), name="resnorm_last_prompt",
    )(x, y, g_post)
    sample_rows = pl.BlockSpec((1, n_dec, d), lambda s: (0, seq_len // n_dec, 0))
    y_sample = pl.pallas_call(
        _resnorm_last_kernel, out_shape=jax.ShapeDtypeStruct((1, n_dec, d), F32), grid=(1,),
        in_specs=[sample_rows, sample_rows, pl.BlockSpec((1, d), lambda s: (0, 0))],
        out_specs=pl.BlockSpec((1, n_dec, d), lambda s: (0, 0, 0)),
        compiler_params=_params("arbitrary"), name="resnorm_last_sample",
    )(x, y, g_post)
    return y_prompt, y_sample.reshape(n_dec, 1, d)


def _copy_rows_kernel(src_ref, o_ref):
    o_ref[0] = src_ref[...]


def insert_sample_rows(tiled, rows, *, seq_len):
    n_seq, _, c = tiled.shape
    n_dec = rows.shape[0]
    return pl.pallas_call(
        _skip_aliased(_copy_rows_kernel, 1), out_shape=jax.ShapeDtypeStruct(tiled.shape, tiled.dtype),
        grid=(n_seq,),
        in_specs=[pl.BlockSpec(memory_space=pl.ANY), pl.BlockSpec((n_dec, c), lambda s: (0, 0))],
        out_specs=pl.BlockSpec((1, n_dec, c), lambda s: (s, seq_len // n_dec, 0)),
        input_output_aliases={0: 0},
        compiler_params=_params("arbitrary"), name="insert_sample_rows",
    )(tiled, rows)


def _matmul_kernel(*refs, epilogue, has_bias):
    if has_bias:
        x_ref, w_ref, b_ref, o_ref = refs
    else:
        x_ref, w_ref, o_ref = refs
    for r0, r1 in _row_chunks(x_ref.shape[1]):
        acc = _dot_bf16(x_ref[0, r0:r1], w_ref[...])
        if has_bias:
            acc = acc + b_ref[...]
        o_ref[0, r0:r1] = epilogue(acc).astype(o_ref.dtype)


def matmul(x, w, bias=None, *, row_tiles, tn, n_out, col_block_offset=0, out_dtype=F32,
           epilogue=_identity, name):
    n_seq, rows, k = x.shape
    tm = rows // row_tiles
    in_specs = [pl.BlockSpec((1, tm, k), lambda i, j: (i // row_tiles, i % row_tiles, 0)),
                pl.BlockSpec((k, tn), lambda i, j: (0, j + col_block_offset))]
    args = [x, w]
    if bias is not None:
        in_specs.append(pl.BlockSpec((1, tn), lambda i, j: (0, j + col_block_offset)))
        args.append(bias.reshape(1, -1))
    return pl.pallas_call(
        functools.partial(_matmul_kernel, epilogue=epilogue, has_bias=bias is not None),
        out_shape=jax.ShapeDtypeStruct((n_seq, rows, n_out), out_dtype),
        grid=(n_seq * row_tiles, n_out // tn),
        in_specs=in_specs,
        out_specs=pl.BlockSpec((1, tm, tn), lambda i, j: (i // row_tiles, i % row_tiles, j)),
        compiler_params=_params("parallel", "arbitrary"),
        name=name,
    )(*args)


def _matmul_wres_kernel(x_ref, w_ref, o_ref):
    o_ref[0] = jnp.dot(x_ref[0], w_ref[...], preferred_element_type=F32)


def matmul_weight_resident(x, w, *, tm, tn, name):
    n_seq, rows, k = x.shape
    n = w.shape[1]
    per_seq = rows // tm
    return pl.pallas_call(
        _matmul_wres_kernel,
        out_shape=jax.ShapeDtypeStruct((n_seq, rows, n), F32),
        grid=(n // tn, n_seq * per_seq),
        in_specs=[pl.BlockSpec((1, tm, k), lambda j, i: (i // per_seq, i % per_seq, 0)),
                  pl.BlockSpec((k, tn), lambda j, i: (0, j))],
        out_specs=pl.BlockSpec((1, tm, tn), lambda j, i: (i // per_seq, i % per_seq, j)),
        compiler_params=_params("parallel", "arbitrary"),
        name=name,
    )(x, w)


def _layer_norm(v, g, b):
    xc = v - jnp.mean(v, axis=-1, keepdims=True)
    return xc * lax.rsqrt(jnp.mean(xc * xc, axis=-1, keepdims=True) + NORM_EPS) * g + b


def _spatial_kernel(v_ref, u_ref, lng_ref, lnb_ref, ws_ref, bias_ref, o_ref, gv_ref, *, group_dim):
    vn = _layer_norm(v_ref[0], lng_ref[...], lnb_ref[...])
    gv_ref[0] = vn
    vb = vn.astype(BF16)
    t = lax.broadcasted_iota(jnp.int32, (CHUNK, CHUNK), 0)
    s = lax.broadcasted_iota(jnp.int32, (CHUNK, CHUNK), 1)
    causal = s <= t
    for g in range(N_GROUPS):
        cols = slice(g * group_dim, (g + 1) * group_dim)
        wc = jnp.where(causal, ws_ref[g], 0.0).astype(BF16)
        mix = jnp.dot(wc, vb[:, cols], preferred_element_type=F32) + bias_ref[:, cols]
        o_ref[0, :, cols] = (u_ref[0, :, cols].astype(F32) * mix).astype(o_ref.dtype)


def _spatial_sample_kernel(v_ref, u_ref, lng_ref, lnb_ref, w00_ref, b0_ref, o_ref, gv_ref):
    vn = _layer_norm(v_ref[0], lng_ref[...], lnb_ref[...])
    gv_ref[0] = vn
    mix = w00_ref[...].astype(BF16).astype(F32) * vn.astype(BF16).astype(F32) + b0_ref[...]
    o_ref[0] = (u_ref[0].astype(F32) * mix).astype(o_ref.dtype)


def spatial_gate(v, u, ln_g, ln_b, w_s, b_s, *, seq_len):
    n_seq, rows, d = v.shape
    n_dec = rows - seq_len
    group_dim = d // N_GROUPS
    ln_g, ln_b = ln_g.reshape(1, d), ln_b.reshape(1, d)
    bias_full = jnp.repeat(b_s.T, group_dim, axis=1)
    row = pl.BlockSpec((1, CHUNK, d), lambda s, c: (s, c, 0))
    vec = pl.BlockSpec((1, d), lambda s, c: (0, 0))
    gated, gv_prompt = pl.pallas_call(
        functools.partial(_spatial_kernel, group_dim=group_dim),
        out_shape=(jax.ShapeDtypeStruct(v.shape, BF16),
                   jax.ShapeDtypeStruct((n_seq, CHUNK, d), F32)),
        grid=(n_seq, seq_len // CHUNK),
        in_specs=[row, row, vec, vec,
                  pl.BlockSpec((N_GROUPS, CHUNK, CHUNK), lambda s, c: (0, 0, 0)),
                  pl.BlockSpec((CHUNK, d), lambda s, c: (0, 0))],
        out_specs=(row, pl.BlockSpec((1, CHUNK, d), lambda s, c: (s, 0, 0))),
        compiler_params=_params("parallel", "arbitrary"),
        name="spatial_gate_prompt",
    )(v, u, ln_g, ln_b, w_s, bias_full)
    w00 = jnp.repeat(w_s[:, 0, 0], group_dim).reshape(1, d)
    b0 = jnp.repeat(b_s[:, 0], group_dim).reshape(1, d)
    tile0_rows = pl.BlockSpec((1, n_dec, d), lambda s: (0, seq_len // n_dec, 0))
    vec1 = pl.BlockSpec((1, d), lambda s: (0, 0))
    gated, gv_sample = pl.pallas_call(
        _skip_aliased(_spatial_sample_kernel, 1),
        out_shape=(jax.ShapeDtypeStruct(v.shape, BF16), jax.ShapeDtypeStruct((1, n_dec, d), F32)),
        grid=(n_seq,),
        in_specs=[pl.BlockSpec(memory_space=pl.ANY), tile0_rows, tile0_rows, vec1, vec1, vec1, vec1],
        out_specs=(pl.BlockSpec((1, n_dec, d), lambda s: (s, seq_len // n_dec, 0)),
                   pl.BlockSpec((1, n_dec, d), lambda s: (0, 0, 0))),
        input_output_aliases={0: 0},
        compiler_params=_params("arbitrary"),
        name="spatial_gate_sample",
    )(gated, v, u, ln_g, ln_b, w00, b0)
    return gated, gv_prompt, gv_sample


def _pair_block_diag(pair, even):
    lane = lax.broadcasted_iota(jnp.int32, pair.shape, 1)
    if even:
        own = jnp.where(lane < HEAD_DIM, pair, 0.0)
        return jnp.concatenate([own, pltpu.roll(own, HEAD_DIM, 1)], axis=0)
    own = jnp.where(lane >= HEAD_DIM, pair, 0.0)
    return jnp.concatenate([pltpu.roll(own, HEAD_DIM, 1), own], axis=0)


def _attn_prompt_kernel(sink_ref, q_ref, kprev_ref, kown_ref, vprev_ref, vown_ref, o_ref):
    blk = pl.program_id(1)
    n_keys = 2 * WINDOW
    qi = lax.broadcasted_iota(jnp.int32, (WINDOW, n_keys), 0)
    kj = lax.broadcasted_iota(jnp.int32, (WINDOW, n_keys), 1)
    first_key = jnp.where(blk > 0, 0, WINDOW)
    valid = (kj >= qi) & (kj <= qi + WINDOW) & (kj >= first_key)
    col_blocks = Q_PER_KV * HEAD_DIM // V7X_LANES
    valid = jnp.concatenate([valid] * col_blocks, axis=0)
    block_of_row = lax.broadcasted_iota(jnp.int32, (col_blocks * WINDOW, 1), 0) // WINDOW
    group_w = Q_PER_KV * HEAD_DIM
    for pair in range(N_KV_HEADS // 2):
        lanes = slice(pair * V7X_LANES, (pair + 1) * V7X_LANES)
        k_pair = jnp.concatenate([kprev_ref[0, :, lanes], kown_ref[0, :, lanes]], axis=0)
        v_pair = jnp.concatenate([vprev_ref[0, :, lanes], vown_ref[0, :, lanes]], axis=0)
        for e in range(2):
            h = 2 * pair + e
            k2 = _pair_block_diag(k_pair, e == 0).astype(BF16)
            v2 = _pair_block_diag(v_pair, e == 0).astype(BF16)
            qs = jnp.concatenate(
                [q_ref[0, :, h * group_w + c * V7X_LANES: h * group_w + (c + 1) * V7X_LANES]
                 for c in range(col_blocks)], axis=0)
            qs = (qs * ATTN_SCALE).astype(BF16)
            s = lax.dot_general(qs, k2, (((1,), (1,)), ((), ())), preferred_element_type=F32)
            probs = []
            for half in range(2):
                sh = jnp.where(valid, s[:, half * n_keys:(half + 1) * n_keys], MASKED_SCORE)
                sink = jnp.zeros((col_blocks * WINDOW, 1), F32)
                for c in range(col_blocks):
                    sink = jnp.where(block_of_row == c, sink_ref[h * Q_PER_KV + 2 * c + half], sink)
                m = jnp.maximum(jnp.max(sh, axis=-1, keepdims=True), sink)
                p = jnp.exp(sh - m)
                p = p / (jnp.sum(p, axis=-1, keepdims=True) + jnp.exp(sink - m))
                probs.append(p.astype(BF16))
            o = jnp.dot(jnp.concatenate(probs, axis=1), v2, preferred_element_type=F32)
            for c in range(col_blocks):
                o_ref[0, :, h * group_w + c * V7X_LANES: h * group_w + (c + 1) * V7X_LANES] = (
                    o[c * WINDOW:(c + 1) * WINDOW].astype(o_ref.dtype))


def attention_prompt(qkv, sinks, *, seq_len):
    n_seq, rows, _ = qkv.shape
    d_q = N_KV_HEADS * Q_PER_KV * HEAD_DIM
    d_kv = N_KV_HEADS * HEAD_DIM
    k_col = d_q // d_kv
    v_col = k_col + 1

    def own(col):
        return lambda s, i: (s, i, col)

    def prev(col):
        return lambda s, i: (s, jnp.maximum(i - 1, 0), col)

    return pl.pallas_call(
        _attn_prompt_kernel,
        out_shape=jax.ShapeDtypeStruct((n_seq, rows, d_q), BF16),
        grid=(n_seq, seq_len // WINDOW),
        in_specs=[pl.BlockSpec(memory_space=pltpu.SMEM),
                  pl.BlockSpec((1, WINDOW, d_q), own(0)),
                  pl.BlockSpec((1, WINDOW, d_kv), prev(k_col)),
                  pl.BlockSpec((1, WINDOW, d_kv), own(k_col)),
                  pl.BlockSpec((1, WINDOW, d_kv), prev(v_col)),
                  pl.BlockSpec((1, WINDOW, d_kv), own(v_col))],
        out_specs=pl.BlockSpec((1, WINDOW, d_q), own(0)),
        compiler_params=_params("parallel", "arbitrary"),
        name="attention_prompt",
    )(sinks, qkv, qkv, qkv, qkv, qkv)


def _attn_sample_kernel(q2_ref, ck_ref, cv_ref, knew_ref, vnew_ref, sink_ref, o_ref):
    for pair in range(N_KV_HEADS // 2):
        lanes = slice(pair * V7X_LANES, (pair + 1) * V7X_LANES)
        q2 = q2_ref[0, pair]
        k_pair = ck_ref[0, :, lanes].astype(BF16)
        v_pair = cv_ref[0, :, lanes].astype(BF16)
        k_new = knew_ref[0, :, lanes].astype(BF16).astype(F32)
        v_new = vnew_ref[0, :, lanes].astype(BF16).astype(F32)
        sink = sink_ref[pair][:, :1]
        s = lax.dot_general(q2, k_pair, (((1,), (1,)), ((), ())), preferred_element_type=F32)
        s_new = jnp.sum(q2.astype(F32) * k_new, axis=-1, keepdims=True)
        m = jnp.maximum(jnp.maximum(jnp.max(s, axis=-1, keepdims=True), s_new), sink)
        p = jnp.exp(s - m)
        p_new = jnp.exp(s_new - m)
        denom = jnp.sum(p, axis=-1, keepdims=True) + p_new + jnp.exp(sink - m)
        o = jnp.dot((p / denom).astype(BF16), v_pair, preferred_element_type=F32)
        o_ref[0, pair] = o + (p_new / denom).astype(BF16).astype(F32) * v_new


def attention_sample(qkv, cache_k, cache_v, sinks):
    n = qkv.shape[0]
    d_q = N_KV_HEADS * Q_PER_KV * HEAD_DIM
    d_kv = N_KV_HEADS * HEAD_DIM
    n_pairs = N_KV_HEADS // 2
    rows = 2 * Q_PER_KV
    q = (qkv[:, :d_q] * ATTN_SCALE).astype(BF16).reshape(n, n_pairs, 2, Q_PER_KV, HEAD_DIM)
    zeros = jnp.zeros((n, n_pairs, Q_PER_KV, HEAD_DIM), BF16)
    q2 = jnp.concatenate([jnp.concatenate([q[:, :, 0], zeros], axis=-1),
                          jnp.concatenate([zeros, q[:, :, 1]], axis=-1)], axis=2)
    k_new = qkv[:, d_q:d_q + d_kv].reshape(n, 1, d_kv)
    v_new = qkv[:, d_q + d_kv:].reshape(n, 1, d_kv)
    sink2 = jnp.broadcast_to(sinks.reshape(n_pairs, rows, 1), (n_pairs, rows, V7X_LANES))
    o2 = pl.pallas_call(
        _attn_sample_kernel,
        out_shape=jax.ShapeDtypeStruct((n, n_pairs, rows, V7X_LANES), F32),
        grid=(n,),
        in_specs=[pl.BlockSpec((1, n_pairs, rows, V7X_LANES), lambda b: (b, 0, 0, 0)),
                  pl.BlockSpec((1, WINDOW, d_kv), lambda b: (b, 0, 0)),
                  pl.BlockSpec((1, WINDOW, d_kv), lambda b: (b, 0, 0)),
                  pl.BlockSpec((1, 1, d_kv), lambda b: (b, 0, 0)),
                  pl.BlockSpec((1, 1, d_kv), lambda b: (b, 0, 0)),
                  pl.BlockSpec((n_pairs, rows, V7X_LANES), lambda b: (0, 0, 0))],
        out_specs=pl.BlockSpec((1, n_pairs, rows, V7X_LANES), lambda b: (b, 0, 0, 0)),
        compiler_params=_params("parallel"),
        name="attention_sample",
    )(q2, cache_k.reshape(n, WINDOW, d_kv), cache_v.reshape(n, WINDOW, d_kv), k_new, v_new, sink2)
    o = jnp.stack([o2[:, :, :Q_PER_KV, :HEAD_DIM], o2[:, :, Q_PER_KV:, HEAD_DIM:]], axis=2)
    return o.reshape(n, d_q).astype(BF16), k_new, v_new


def _silu_gate(gate, val):
    return gate * (1.0 / (1.0 + jnp.exp(-gate))) * val


def _ffn_up_kernel(x_ref, wg_ref, wv_ref, cwg_ref, cwv_ref, cbg_ref, cbv_ref,
                   p0g_ref, p0v_ref, p1g_ref, p1v_ref,
                   a_ref, sg_ref, sv_ref, hsg_ref, hsv_ref, h_ref, tail_ref, *, halves, n_dec):
    half = pl.program_id(0) % halves
    j = pl.program_id(1)
    tn = wg_ref.shape[1]
    tail_rows = tail_ref.shape[1]
    tm = x_ref.shape[1]
    gate_cols, val_cols = slice(0, tn), slice(tn, 2 * tn)

    @pl.when((pl.program_id(0) == 0) & (j == 0))
    def _():
        tail_ref[...] = jnp.zeros(tail_ref.shape, tail_ref.dtype)

    first_tile = jnp.full((tail_rows, 2 * tn), half, jnp.int32) == 0
    h_ref[0:tail_rows] = jnp.where(first_tile, 0.0, tail_ref[j])

    def conv(r0, r1, cols, cw_ref, cb_ref):
        ext = h_ref[r0:r1 + tail_rows, cols]
        h1 = pltpu.roll(ext, 1, 0)[tail_rows:]
        h2 = pltpu.roll(ext, 2, 0)[tail_rows:]
        return cb_ref[...] + cw_ref[0:1] * h2 + cw_ref[1:2] * h1 + cw_ref[2:3] * ext[tail_rows:]

    def epilogue(r0, r1):
        a_ref[0, r0:r1] = _silu_gate(conv(r0, r1, gate_cols, cwg_ref, cbg_ref),
                                     conv(r0, r1, val_cols, cwv_ref, cbv_ref)).astype(a_ref.dtype)

    pending = None
    for r0, r1 in _row_chunks(tm):
        x = x_ref[0, r0:r1]
        h_ref[tail_rows + r0:tail_rows + r1, gate_cols] = _dot_bf16(x, wg_ref[...])
        h_ref[tail_rows + r0:tail_rows + r1, val_cols] = _dot_bf16(x, wv_ref[...])
        if pending is not None:
            epilogue(*pending)
        pending = (r0, r1)
    epilogue(*pending)
    tail_ref[j] = h_ref[tm:tm + tail_rows]

    h_sample = h_ref[tail_rows + tm - n_dec:tail_rows + tm]
    hsg_ref[0, 0] = h_sample[:, gate_cols]
    hsv_ref[0, 0] = h_sample[:, val_cols]
    seq_tail = h_ref[tm - n_dec:tm - n_dec + tail_rows]
    sg_ref[0, 0] = seq_tail[:, gate_cols]
    sv_ref[0, 0] = seq_tail[:, val_cols]

    @pl.when(half == halves - 1)
    def _():
        def conv_sample(cols, p0_ref, p1_ref, cw_ref, cb_ref):
            return (cb_ref[...] + cw_ref[0:1] * p0_ref[...] + cw_ref[1:2] * p1_ref[...]
                    + cw_ref[2:3] * h_sample[:, cols])

        a_ref[0, tm - n_dec:tm] = _silu_gate(
            conv_sample(gate_cols, p0g_ref, p1g_ref, cwg_ref, cbg_ref),
            conv_sample(val_cols, p0v_ref, p1v_ref, cwv_ref, cbv_ref)).astype(a_ref.dtype)


def ffn_up(h, w_up, conv_w, conv_b, state, *, seq_len, halves, tn, tail_rows):
    n_seq, rows, k = h.shape
    n_dec = rows - seq_len
    d_ff = w_up.shape[1] // 2
    nj = d_ff // tn
    tm = rows // halves
    conv_b = conv_b.reshape(1, -1)
    p0, p1 = state[:, 0], state[:, 1]

    def gate(shape):
        return pl.BlockSpec(shape, lambda i, j: (0, j))

    def val(shape):
        return pl.BlockSpec(shape, lambda i, j: (0, j + nj))

    per_tile = lambda r: pl.BlockSpec((1, 1, r, tn), lambda i, j: (i // halves, i % halves, 0, j))
    a, sg, sv, hsg, hsv = pl.pallas_call(
        functools.partial(_ffn_up_kernel, halves=halves, n_dec=n_dec),
        out_shape=(jax.ShapeDtypeStruct((n_seq, rows, d_ff), BF16),
                   jax.ShapeDtypeStruct((n_seq, halves, tail_rows, d_ff), F32),
                   jax.ShapeDtypeStruct((n_seq, halves, tail_rows, d_ff), F32),
                   jax.ShapeDtypeStruct((n_seq, halves, n_dec, d_ff), F32),
                   jax.ShapeDtypeStruct((n_seq, halves, n_dec, d_ff), F32)),
        grid=(n_seq * halves, nj),
        in_specs=[pl.BlockSpec((1, tm, k), lambda i, j: (i // halves, i % halves, 0)),
                  gate((k, tn)), val((k, tn)), gate((CONV_W, tn)), val((CONV_W, tn)),
                  gate((1, tn)), val((1, tn)),
                  gate((n_dec, tn)), val((n_dec, tn)), gate((n_dec, tn)), val((n_dec, tn))],
        out_specs=(pl.BlockSpec((1, tm, tn), lambda i, j: (i // halves, i % halves, j)),
                   per_tile(tail_rows), per_tile(tail_rows), per_tile(n_dec), per_tile(n_dec)),
        scratch_shapes=[pltpu.VMEM((tail_rows + tm, 2 * tn), F32),
                        pltpu.VMEM((nj, tail_rows, 2 * tn), F32)],
        compiler_params=_params("arbitrary", "arbitrary"),
        name="ffn_up",
    )(h, w_up, w_up, conv_w, conv_w, conv_b, conv_b, p0, p0, p1, p1)
    keep = slice(tail_rows - (CONV_W - 1), tail_rows)
    state_prompt = jnp.concatenate([sg[:, halves - 1, keep], sv[:, halves - 1, keep]], axis=-1)
    h_sample = jnp.concatenate([hsg[0, halves - 1], hsv[0, halves - 1]], axis=-1)
    return a, state_prompt, h_sample


ROW_TILES = 2
TN = 512
TN_FFN_UP = V7X_MXU_COLS
TR_PROMPT = 256
TILED_ROW_BLOCKS = 10
FFN_DOWN_ROW_BLOCKS = 5
STATE_TAIL_ROWS = V7X_SUBLANES


def kernel(x_prompt, x_sample, cache_win_k, cache_win_v, state_conv, norm_mix_pre, norm_mix_post,
           norm_ffn_pre, norm_ffn_post, gmlp_w_in, gmlp_ln_g, gmlp_ln_b, gmlp_w_s, gmlp_b_s,
           gmlp_w_out, attn_w_qkv, attn_b_qkv, attn_sinks, attn_w_o, attn_b_o, ffn_w_up,
           ffn_conv_w, ffn_conv_b, ffn_w_down):
    n_seq, seq_len, d = x_prompt.shape
    n_dec = x_sample.shape[0]
    depth = norm_mix_pre.shape[0]
    rows = seq_len + n_dec
    x_sample = x_sample.reshape(1, n_dec, d)
    mm = functools.partial(matmul, row_tiles=ROW_TILES, tn=TN)

    h = rmsnorm_first(x_prompt, x_sample, norm_mix_pre[0], tr=TR_PROMPT)
    x = None
    gv_p, gv_s, wk_p, wv_p, wk_s, wv_s, cv_p, cv_s = [], [], [], [], [], [], [], []
    for layer in range(depth):
        idx = layer // 2
        if layer % 2 == 0:
            d_g = gmlp_w_out.shape[1]
            u = mm(h, gmlp_w_in[idx], n_out=d_g, out_dtype=BF16, epilogue=_gelu, name="gmlp_in_u")
            v = mm(h, gmlp_w_in[idx], n_out=d_g, col_block_offset=d_g // TN, epilogue=_gelu,
                   name="gmlp_in_v")
            gated, g_p, g_s = spatial_gate(v, u, gmlp_ln_g[idx], gmlp_ln_b[idx], gmlp_w_s[idx],
                                           gmlp_b_s[idx], seq_len=seq_len)
            gv_p.append(g_p)
            gv_s.append(g_s.reshape(n_dec, 1, d_g))
            y = mm(gated, gmlp_w_out[idx], n_out=d, name="gmlp_out")
        else:
            d_q = attn_w_o.shape[1]
            d_qkv = attn_w_qkv.shape[2]
            d_kv = (d_qkv - d_q) // 2
            qkv = mm(h, attn_w_qkv[idx], attn_b_qkv[idx], n_out=d_qkv, name="attn_qkv")
            o = attention_prompt(qkv, attn_sinks[idx], seq_len=seq_len)
            o_s, k_new, v_new = attention_sample(qkv[0, seq_len:], cache_win_k[idx], cache_win_v[idx],
                                                 attn_sinks[idx])
            o = insert_sample_rows(o, o_s, seq_len=seq_len)
            tail = qkv[:, seq_len - WINDOW:seq_len]
            wk_p.append(tail[:, :, d_q:d_q + d_kv].reshape(n_seq, WINDOW, N_KV_HEADS, HEAD_DIM))
            wv_p.append(tail[:, :, d_q + d_kv:].reshape(n_seq, WINDOW, N_KV_HEADS, HEAD_DIM))
            wk_s.append(jnp.concatenate(
                [cache_win_k[idx][:, 1:], k_new.reshape(n_dec, 1, N_KV_HEADS, HEAD_DIM)], axis=1))
            wv_s.append(jnp.concatenate(
                [cache_win_v[idx][:, 1:], v_new.reshape(n_dec, 1, N_KV_HEADS, HEAD_DIM)], axis=1))
            y = mm(o, attn_w_o[idx], attn_b_o[idx], n_out=d, name="attn_out")
        if x is None:
            x, h = resnorm_first(x_prompt, x_sample, y, norm_mix_post[layer], norm_ffn_pre[layer],
                                 tr=TR_PROMPT)
        else:
            x, h = resnorm_tiled(x, y, norm_mix_post[layer], norm_ffn_pre[layer],
                                 tr=rows // TILED_ROW_BLOCKS)

        a, c_p, hu_s = ffn_up(h, ffn_w_up[layer], ffn_conv_w[layer], ffn_conv_b[layer], state_conv[layer],
                              seq_len=seq_len, halves=ROW_TILES, tn=TN_FFN_UP, tail_rows=STATE_TAIL_ROWS)
        cv_p.append(c_p)
        cv_s.append(jnp.concatenate([state_conv[layer][:, 1:], hu_s[:, None]], axis=1))
        f = matmul_weight_resident(a, ffn_w_down[layer].astype(BF16), tm=rows // FFN_DOWN_ROW_BLOCKS,
                                   tn=TN, name="ffn_down")
        if layer + 1 < depth:
            x, h = resnorm_tiled(x, f, norm_ffn_post[layer], norm_mix_pre[layer + 1],
                                 tr=rows // TILED_ROW_BLOCKS)
        else:
            y_prompt, y_sample = resnorm_last(x, f, norm_ffn_post[layer], seq_len=seq_len, n_dec=n_dec,
                                              tr=TR_PROMPT)

    return (y_prompt, y_sample, jnp.stack(gv_p), jnp.stack(gv_s),
            jnp.stack(wk_p), jnp.stack(wv_p), jnp.stack(wk_s), jnp.stack(wv_s),
            jnp.stack(cv_p), jnp.stack(cv_s))
```

```python
import functools
import math

import jax
import jax.numpy as jnp
from jax import lax
from jax.experimental import pallas as pl
from jax.experimental.pallas import tpu as pltpu

F32 = jnp.float32
BF16 = jnp.bfloat16

NORM_EPS = 1e-6
CHUNK = 128
N_GROUPS = 16
HEAD_DIM = 64
N_KV_HEADS = 8
Q_PER_KV = 8
WINDOW = 128
CONV_W = 3
ATTN_SCALE = HEAD_DIM ** -0.5
MASKED_SCORE = -1e30

V7X_LANES = 128
V7X_SUBLANES = 8
V7X_MXU_COLS = 256
V7X_SCOPED_VMEM_BYTES = 60000 * 1024

DOT_ROWS = 512


def _params(*semantics):
    return pltpu.CompilerParams(dimension_semantics=semantics,
                                vmem_limit_bytes=V7X_SCOPED_VMEM_BYTES)


def _rms(x, g):
    return x * lax.rsqrt(jnp.mean(x * x, axis=-1, keepdims=True) + NORM_EPS) * g


def _gelu(x):
    return 0.5 * x * (1.0 + lax.erf(x * math.sqrt(0.5)))


def _identity(x):
    return x


def _dot_bf16(x, w):
    return jnp.dot(x, w.astype(BF16), preferred_element_type=F32)


def _row_chunks(rows):
    n = max(rows // DOT_ROWS, 1)
    return [(c * DOT_ROWS, (c + 1) * DOT_ROWS if c + 1 < n else rows) for c in range(n)]


def _skip_aliased(body, n_aliased):
    def kernel_fn(*refs):
        body(*refs[n_aliased:])
    return kernel_fn


def _rmsnorm_kernel(x_ref, g_ref, h_ref):
    h_ref[0] = _rms(x_ref[0], g_ref[...]).astype(h_ref.dtype)


def _resnorm_kernel(x_ref, y_ref, gpost_ref, gnext_ref, xo_ref, ho_ref):
    xn = x_ref[0] + _rms(y_ref[0], gpost_ref[...])
    xo_ref[0] = xn
    ho_ref[0] = _rms(xn, gnext_ref[...]).astype(ho_ref.dtype)


def _resnorm_last_kernel(x_ref, y_ref, gpost_ref, xo_ref):
    xo_ref[0] = x_ref[0] + _rms(y_ref[0], gpost_ref[...])


def rmsnorm_first(x_prompt, x_sample, g, *, tr):
    n_seq, seq_len, d = x_prompt.shape
    n_dec = x_sample.shape[1]
    g = g.reshape(1, d)
    shape = jax.ShapeDtypeStruct((n_seq, seq_len + n_dec, d), BF16)
    row = pl.BlockSpec((1, tr, d), lambda s, r: (s, r, 0))
    h = pl.pallas_call(
        _rmsnorm_kernel, out_shape=shape, grid=(n_seq, seq_len // tr),
        in_specs=[row, pl.BlockSpec((1, d), lambda s, r: (0, 0))], out_specs=row,
        compiler_params=_params("parallel", "parallel"), name="rmsnorm_first_prompt",
    )(x_prompt, g)
    return pl.pallas_call(
        _skip_aliased(_rmsnorm_kernel, 1), out_shape=shape, grid=(n_seq,),
        in_specs=[pl.BlockSpec(memory_space=pl.ANY),
                  pl.BlockSpec((1, n_dec, d), lambda s: (0, 0, 0)),
                  pl.BlockSpec((1, d), lambda s: (0, 0))],
        out_specs=pl.BlockSpec((1, n_dec, d), lambda s: (s, seq_len // n_dec, 0)),
        input_output_aliases={0: 0},
        compiler_params=_params("arbitrary"), name="rmsnorm_first_sample",
    )(h, x_sample, g)


def resnorm_first(x_prompt, x_sample, y, g_post, g_next, *, tr):
    n_seq, seq_len, d = x_prompt.shape
    n_dec = x_sample.shape[1]
    g_post, g_next = g_post.reshape(1, d), g_next.reshape(1, d)
    shapes = (jax.ShapeDtypeStruct(y.shape, F32), jax.ShapeDtypeStruct(y.shape, BF16))
    row = pl.BlockSpec((1, tr, d), lambda s, r: (s, r, 0))
    vec = pl.BlockSpec((1, d), lambda s, r: (0, 0))
    xo, ho = pl.pallas_call(
        _resnorm_kernel, out_shape=shapes, grid=(n_seq, seq_len // tr),
        in_specs=[row, row, vec, vec], out_specs=(row, row),
        compiler_params=_params("parallel", "parallel"), name="resnorm_first_prompt",
    )(x_prompt, y, g_post, g_next)
    sample_rows = pl.BlockSpec((1, n_dec, d), lambda s: (s, seq_len // n_dec, 0))
    vec1 = pl.BlockSpec((1, d), lambda s: (0, 0))
    return pl.pallas_call(
        _skip_aliased(_resnorm_kernel, 2), out_shape=shapes, grid=(n_seq,),
        in_specs=[pl.BlockSpec(memory_space=pl.ANY), pl.BlockSpec(memory_space=pl.ANY),
                  pl.BlockSpec((1, n_dec, d), lambda s: (0, 0, 0)),
                  pl.BlockSpec((1, n_dec, d), lambda s: (0, seq_len // n_dec, 0)), vec1, vec1],
        out_specs=(sample_rows, sample_rows),
        input_output_aliases={0: 0, 1: 1},
        compiler_params=_params("arbitrary"), name="resnorm_first_sample",
    )(xo, ho, x_sample, y, g_post, g_next)


def resnorm_tiled(x, y, g_post, g_next, *, tr):
    n_seq, rows, d = x.shape
    row = pl.BlockSpec((1, tr, d), lambda s, r: (s, r, 0))
    vec = pl.BlockSpec((1, d), lambda s, r: (0, 0))
    return pl.pallas_call(
        _resnorm_kernel,
        out_shape=(jax.ShapeDtypeStruct(x.shape, F32), jax.ShapeDtypeStruct(x.shape, BF16)),
        grid=(n_seq, rows // tr),
        in_specs=[row, row, vec, vec], out_specs=(row, row),
        compiler_params=_params("parallel", "parallel"), name="resnorm_tiled",
    )(x, y, g_post.reshape(1, d), g_next.reshape(1, d))


def resnorm_last(x, y, g_post, *, seq_len, n_dec, tr):
    n_seq, _, d = x.shape
    g_post = g_post.reshape(1, d)
    row = pl.BlockSpec((1, tr, d), lambda s, r: (s, r, 0))
    y_prompt = pl.pallas_call(
        _resnorm_last_kernel, out_shape=jax.ShapeDtypeStruct((n_seq, seq_len, d), F32),
        grid=(n_seq, seq_len // tr),
        in_specs=[row, row, pl.BlockSpec((1, d), lambda s, r: (0, 0))], out_specs=row,
        compiler_params=_params("parallel", "parallel"), name="resnorm_last_prompt",
    )(x, y, g_post)
    sample_rows = pl.BlockSpec((1, n_dec, d), lambda s: (0, seq_len // n_dec, 0))
    y_sample = pl.pallas_call(
        _resnorm_last_kernel, out_shape=jax.ShapeDtypeStruct((1, n_dec, d), F32), grid=(1,),
        in_specs=[sample_rows, sample_rows, pl.BlockSpec((1, d), lambda s: (0, 0))],
        out_specs=pl.BlockSpec((1, n_dec, d), lambda s: (0, 0, 0)),
        compiler_params=_params("arbitrary"), name="resnorm_last_sample",
    )(x, y, g_post)
    return y_prompt, y_sample.reshape(n_dec, 1, d)


def _copy_rows_kernel(src_ref, o_ref):
    o_ref[0] = src_ref[...]


def insert_sample_rows(tiled, rows, *, seq_len):
    n_seq, _, c = tiled.shape
    n_dec = rows.shape[0]
    return pl.pallas_call(
        _skip_aliased(_copy_rows_kernel, 1), out_shape=jax.ShapeDtypeStruct(tiled.shape, tiled.dtype),
        grid=(n_seq,),
        in_specs=[pl.BlockSpec(memory_space=pl.ANY), pl.BlockSpec((n_dec, c), lambda s: (0, 0))],
        out_specs=pl.BlockSpec((1, n_dec, c), lambda s: (s, seq_len // n_dec, 0)),
        input_output_aliases={0: 0},
        compiler_params=_params("arbitrary"), name="insert_sample_rows",
    )(tiled, rows)


def _matmul_kernel(*refs, epilogue, has_bias):
    if has_bias:
        x_ref, w_ref, b_ref, o_ref = refs
    else:
        x_ref, w_ref, o_ref = refs
    for r0, r1 in _row_chunks(x_ref.shape[1]):
        acc = _dot_bf16(x_ref[0, r0:r1], w_ref[...])
        if has_bias:
            acc = acc + b_ref[...]
        o_ref[0, r0:r1] = epilogue(acc).astype(o_ref.dtype)


def matmul(x, w, bias=None, *, layer, row_tiles, tn, n_out, col_block_offset=0, out_dtype=F32,
           epilogue=_identity, name):
    n_seq, rows, k = x.shape
    tm = rows // row_tiles
    in_specs = [pl.BlockSpec((1, tm, k), lambda i, j: (i // row_tiles, i % row_tiles, 0)),
                pl.BlockSpec((None, k, tn), lambda i, j: (layer, 0, j + col_block_offset))]
    args = [x, w]
    if bias is not None:
        in_specs.append(pl.BlockSpec((None, 1, tn), lambda i, j: (layer, 0, j + col_block_offset)))
        args.append(bias.reshape(bias.shape[0], 1, -1))
    return pl.pallas_call(
        functools.partial(_matmul_kernel, epilogue=epilogue, has_bias=bias is not None),
        out_shape=jax.ShapeDtypeStruct((n_seq, rows, n_out), out_dtype),
        grid=(n_seq * row_tiles, n_out // tn),
        in_specs=in_specs,
        out_specs=pl.BlockSpec((1, tm, tn), lambda i, j: (i // row_tiles, i % row_tiles, j)),
        compiler_params=_params("parallel", "arbitrary"),
        name=name,
    )(*args)


def _matmul_wres_kernel(x_ref, w_ref, o_ref, wb_ref):
    @pl.when(pl.program_id(1) == 0)
    def _():
        wb_ref[...] = w_ref[...].astype(BF16)

    o_ref[0] = jnp.dot(x_ref[0], wb_ref[...], preferred_element_type=F32)


def matmul_weight_resident(x, w, *, layer, tm, tn, name):
    n_seq, rows, k = x.shape
    n = w.shape[2]
    per_seq = rows // tm
    return pl.pallas_call(
        _matmul_wres_kernel,
        out_shape=jax.ShapeDtypeStruct((n_seq, rows, n), F32),
        grid=(n // tn, n_seq * per_seq),
        in_specs=[pl.BlockSpec((1, tm, k), lambda j, i: (i // per_seq, i % per_seq, 0)),
                  pl.BlockSpec((None, k, tn), lambda j, i: (layer, 0, j), pipeline_mode=pl.Buffered(1))],
        out_specs=pl.BlockSpec((1, tm, tn), lambda j, i: (i // per_seq, i % per_seq, j)),
        scratch_shapes=[pltpu.VMEM((k, tn), BF16)],
        compiler_params=_params("arbitrary", "arbitrary"),
        name=name,
    )(x, w)


def _layer_norm(v, g, b):
    xc = v - jnp.mean(v, axis=-1, keepdims=True)
    return xc * lax.rsqrt(jnp.mean(xc * xc, axis=-1, keepdims=True) + NORM_EPS) * g + b


def _spatial_kernel(v_ref, u_ref, lng_ref, lnb_ref, ws_ref, bias_ref, o_ref, gv_ref, *, group_dim):
    vn = _layer_norm(v_ref[0], lng_ref[...], lnb_ref[...])
    gv_ref[0] = vn
    vb = vn.astype(BF16)
    t = lax.broadcasted_iota(jnp.int32, (CHUNK, CHUNK), 0)
    s = lax.broadcasted_iota(jnp.int32, (CHUNK, CHUNK), 1)
    causal = s <= t
    for g in range(N_GROUPS):
        cols = slice(g * group_dim, (g + 1) * group_dim)
        wc = jnp.where(causal, ws_ref[g], 0.0).astype(BF16)
        mix = jnp.dot(wc, vb[:, cols], preferred_element_type=F32) + bias_ref[:, cols]
        o_ref[0, :, cols] = (u_ref[0, :, cols].astype(F32) * mix).astype(o_ref.dtype)


def _spatial_sample_kernel(v_ref, u_ref, lng_ref, lnb_ref, w00_ref, b0_ref, o_ref, gv_ref):
    vn = _layer_norm(v_ref[0], lng_ref[...], lnb_ref[...])
    gv_ref[0] = vn
    mix = w00_ref[...].astype(BF16).astype(F32) * vn.astype(BF16).astype(F32) + b0_ref[...]
    o_ref[0] = (u_ref[0].astype(F32) * mix).astype(o_ref.dtype)


def spatial_gate(v, u, ln_g, ln_b, w_s, b_s, *, seq_len):
    n_seq, rows, d = v.shape
    n_dec = rows - seq_len
    group_dim = d // N_GROUPS
    ln_g, ln_b = ln_g.reshape(1, d), ln_b.reshape(1, d)
    bias_full = jnp.repeat(b_s.T, group_dim, axis=1)
    row = pl.BlockSpec((1, CHUNK, d), lambda s, c: (s, c, 0))
    vec = pl.BlockSpec((1, d), lambda s, c: (0, 0))
    gated, gv_prompt = pl.pallas_call(
        functools.partial(_spatial_kernel, group_dim=group_dim),
        out_shape=(jax.ShapeDtypeStruct(v.shape, BF16),
                   jax.ShapeDtypeStruct((n_seq, CHUNK, d), F32)),
        grid=(n_seq, seq_len // CHUNK),
        in_specs=[row, row, vec, vec,
                  pl.BlockSpec((N_GROUPS, CHUNK, CHUNK), lambda s, c: (0, 0, 0)),
                  pl.BlockSpec((CHUNK, d), lambda s, c: (0, 0))],
        out_specs=(row, pl.BlockSpec((1, CHUNK, d), lambda s, c: (s, 0, 0))),
        compiler_params=_params("parallel", "arbitrary"),
        name="spatial_gate_prompt",
    )(v, u, ln_g, ln_b, w_s, bias_full)
    w00 = jnp.repeat(w_s[:, 0, 0], group_dim).reshape(1, d)
    b0 = jnp.repeat(b_s[:, 0], group_dim).reshape(1, d)
    tile0_rows = pl.BlockSpec((1, n_dec, d), lambda s: (0, seq_len // n_dec, 0))
    vec1 = pl.BlockSpec((1, d), lambda s: (0, 0))
    gated, gv_sample = pl.pallas_call(
        _skip_aliased(_spatial_sample_kernel, 1),
        out_shape=(jax.ShapeDtypeStruct(v.shape, BF16), jax.ShapeDtypeStruct((1, n_dec, d), F32)),
        grid=(n_seq,),
        in_specs=[pl.BlockSpec(memory_space=pl.ANY), tile0_rows, tile0_rows, vec1, vec1, vec1, vec1],
        out_specs=(pl.BlockSpec((1, n_dec, d), lambda s: (s, seq_len // n_dec, 0)),
                   pl.BlockSpec((1, n_dec, d), lambda s: (0, 0, 0))),
        input_output_aliases={0: 0},
        compiler_params=_params("arbitrary"),
        name="spatial_gate_sample",
    )(gated, v, u, ln_g, ln_b, w00, b0)
    return gated, gv_prompt, gv_sample


def _pair_block_diag(pair, even):
    lane = lax.broadcasted_iota(jnp.int32, pair.shape, 1)
    if even:
        own = jnp.where(lane < HEAD_DIM, pair, 0.0)
        return jnp.concatenate([own, pltpu.roll(own, HEAD_DIM, 1)], axis=0)
    own = jnp.where(lane >= HEAD_DIM, pair, 0.0)
    return jnp.concatenate([pltpu.roll(own, HEAD_DIM, 1), own], axis=0)


def _attn_prompt_kernel(sink_ref, q_ref, kprev_ref, kown_ref, vprev_ref, vown_ref, o_ref):
    blk = pl.program_id(1)
    n_keys = 2 * WINDOW
    qi = lax.broadcasted_iota(jnp.int32, (WINDOW, n_keys), 0)
    kj = lax.broadcasted_iota(jnp.int32, (WINDOW, n_keys), 1)
    first_key = jnp.where(blk > 0, 0, WINDOW)
    valid = (kj >= qi) & (kj <= qi + WINDOW) & (kj >= first_key)
    col_blocks = Q_PER_KV * HEAD_DIM // V7X_LANES
    valid = jnp.concatenate([valid] * col_blocks, axis=0)
    block_of_row = lax.broadcasted_iota(jnp.int32, (col_blocks * WINDOW, 1), 0) // WINDOW
    group_w = Q_PER_KV * HEAD_DIM
    for pair in range(N_KV_HEADS // 2):
        lanes = slice(pair * V7X_LANES, (pair + 1) * V7X_LANES)
        k_pair = jnp.concatenate([kprev_ref[0, :, lanes], kown_ref[0, :, lanes]], axis=0)
        v_pair = jnp.concatenate([vprev_ref[0, :, lanes], vown_ref[0, :, lanes]], axis=0)
        for e in range(2):
            h = 2 * pair + e
            k2 = _pair_block_diag(k_pair, e == 0).astype(BF16)
            v2 = _pair_block_diag(v_pair, e == 0).astype(BF16)
            qs = jnp.concatenate(
                [q_ref[0, :, h * group_w + c * V7X_LANES: h * group_w + (c + 1) * V7X_LANES]
                 for c in range(col_blocks)], axis=0)
            qs = (qs * ATTN_SCALE).astype(BF16)
            s = lax.dot_general(qs, k2, (((1,), (1,)), ((), ())), preferred_element_type=F32)
            probs = []
            for half in range(2):
                sh = jnp.where(valid, s[:, half * n_keys:(half + 1) * n_keys], MASKED_SCORE)
                sink = jnp.zeros((col_blocks * WINDOW, 1), F32)
                for c in range(col_blocks):
                    sink = jnp.where(block_of_row == c, sink_ref[h * Q_PER_KV + 2 * c + half], sink)
                m = jnp.maximum(jnp.max(sh, axis=-1, keepdims=True), sink)
                p = jnp.exp(sh - m)
                p = p / (jnp.sum(p, axis=-1, keepdims=True) + jnp.exp(sink - m))
                probs.append(p.astype(BF16))
            o = jnp.dot(jnp.concatenate(probs, axis=1), v2, preferred_element_type=F32)
            for c in range(col_blocks):
                o_ref[0, :, h * group_w + c * V7X_LANES: h * group_w + (c + 1) * V7X_LANES] = (
                    o[c * WINDOW:(c + 1) * WINDOW].astype(o_ref.dtype))


def attention_prompt(qkv, sinks, *, seq_len):
    n_seq, rows, _ = qkv.shape
    d_q = N_KV_HEADS * Q_PER_KV * HEAD_DIM
    d_kv = N_KV_HEADS * HEAD_DIM
    k_col = d_q // d_kv
    v_col = k_col + 1

    def own(col):
        return lambda s, i: (s, i, col)

    def prev(col):
        return lambda s, i: (s, jnp.maximum(i - 1, 0), col)

    return pl.pallas_call(
        _attn_prompt_kernel,
        out_shape=jax.ShapeDtypeStruct((n_seq, rows, d_q), BF16),
        grid=(n_seq, seq_len // WINDOW),
        in_specs=[pl.BlockSpec(memory_space=pltpu.SMEM),
                  pl.BlockSpec((1, WINDOW, d_q), own(0)),
                  pl.BlockSpec((1, WINDOW, d_kv), prev(k_col)),
                  pl.BlockSpec((1, WINDOW, d_kv), own(k_col)),
                  pl.BlockSpec((1, WINDOW, d_kv), prev(v_col)),
                  pl.BlockSpec((1, WINDOW, d_kv), own(v_col))],
        out_specs=pl.BlockSpec((1, WINDOW, d_q), own(0)),
        compiler_params=_params("parallel", "arbitrary"),
        name="attention_prompt",
    )(sinks, qkv, qkv, qkv, qkv, qkv)


def _attn_sample_kernel(q2_ref, ck_ref, cv_ref, knew_ref, vnew_ref, sink_ref, o_ref):
    for pair in range(N_KV_HEADS // 2):
        lanes = slice(pair * V7X_LANES, (pair + 1) * V7X_LANES)
        q2 = q2_ref[0, pair]
        k_pair = ck_ref[0, :, lanes].astype(BF16)
        v_pair = cv_ref[0, :, lanes].astype(BF16)
        k_new = knew_ref[0, :, lanes].astype(BF16).astype(F32)
        v_new = vnew_ref[0, :, lanes].astype(BF16).astype(F32)
        sink = sink_ref[pair][:, :1]
        s = lax.dot_general(q2, k_pair, (((1,), (1,)), ((), ())), preferred_element_type=F32)
        s_new = jnp.sum(q2.astype(F32) * k_new, axis=-1, keepdims=True)
        m = jnp.maximum(jnp.maximum(jnp.max(s, axis=-1, keepdims=True), s_new), sink)
        p = jnp.exp(s - m)
        p_new = jnp.exp(s_new - m)
        denom = jnp.sum(p, axis=-1, keepdims=True) + p_new + jnp.exp(sink - m)
        o = jnp.dot((p / denom).astype(BF16), v_pair, preferred_element_type=F32)
        o_ref[0, pair] = o + (p_new / denom).astype(BF16).astype(F32) * v_new


def attention_sample(qkv, cache_k, cache_v, sinks):
    n = qkv.shape[0]
    d_q = N_KV_HEADS * Q_PER_KV * HEAD_DIM
    d_kv = N_KV_HEADS * HEAD_DIM
    n_pairs = N_KV_HEADS // 2
    rows = 2 * Q_PER_KV
    q = (qkv[:, :d_q] * ATTN_SCALE).astype(BF16).reshape(n, n_pairs, 2, Q_PER_KV, HEAD_DIM)
    zeros = jnp.zeros((n, n_pairs, Q_PER_KV, HEAD_DIM), BF16)
    q2 = jnp.concatenate([jnp.concatenate([q[:, :, 0], zeros], axis=-1),
                          jnp.concatenate([zeros, q[:, :, 1]], axis=-1)], axis=2)
    k_new = qkv[:, d_q:d_q + d_kv].reshape(n, 1, d_kv)
    v_new = qkv[:, d_q + d_kv:].reshape(n, 1, d_kv)
    sink2 = jnp.broadcast_to(sinks.reshape(n_pairs, rows, 1), (n_pairs, rows, V7X_LANES))
    o2 = pl.pallas_call(
        _attn_sample_kernel,
        out_shape=jax.ShapeDtypeStruct((n, n_pairs, rows, V7X_LANES), F32),
        grid=(n,),
        in_specs=[pl.BlockSpec((1, n_pairs, rows, V7X_LANES), lambda b: (b, 0, 0, 0)),
                  pl.BlockSpec((1, WINDOW, d_kv), lambda b: (b, 0, 0)),
                  pl.BlockSpec((1, WINDOW, d_kv), lambda b: (b, 0, 0)),
                  pl.BlockSpec((1, 1, d_kv), lambda b: (b, 0, 0)),
                  pl.BlockSpec((1, 1, d_kv), lambda b: (b, 0, 0)),
                  pl.BlockSpec((n_pairs, rows, V7X_LANES), lambda b: (0, 0, 0))],
        out_specs=pl.BlockSpec((1, n_pairs, rows, V7X_LANES), lambda b: (b, 0, 0, 0)),
        compiler_params=_params("parallel"),
        name="attention_sample",
    )(q2, cache_k.reshape(n, WINDOW, d_kv), cache_v.reshape(n, WINDOW, d_kv), k_new, v_new, sink2)
    o = jnp.stack([o2[:, :, :Q_PER_KV, :HEAD_DIM], o2[:, :, Q_PER_KV:, HEAD_DIM:]], axis=2)
    return o.reshape(n, d_q).astype(BF16), k_new, v_new


def _silu_gate(gate, val):
    return gate * (1.0 / (1.0 + jnp.exp(-gate))) * val


def _ffn_up_kernel(x_ref, wg_ref, wv_ref, cwg_ref, cwv_ref, cbg_ref, cbv_ref,
                   p0g_ref, p0v_ref, p1g_ref, p1v_ref,
                   a_ref, sg_ref, sv_ref, hsg_ref, hsv_ref, h_ref, tail_ref, *, halves, n_dec):
    half = pl.program_id(0) % halves
    j = pl.program_id(1)
    tn = wg_ref.shape[1]
    tail_rows = tail_ref.shape[1]
    tm = x_ref.shape[1]
    gate_cols, val_cols = slice(0, tn), slice(tn, 2 * tn)

    @pl.when((pl.program_id(0) == 0) & (j == 0))
    def _():
        tail_ref[...] = jnp.zeros(tail_ref.shape, tail_ref.dtype)

    first_tile = jnp.full((tail_rows, 2 * tn), half, jnp.int32) == 0
    h_ref[0:tail_rows] = jnp.where(first_tile, 0.0, tail_ref[j])

    def conv(r0, r1, cols, cw_ref, cb_ref):
        ext = h_ref[r0:r1 + tail_rows, cols]
        h1 = pltpu.roll(ext, 1, 0)[tail_rows:]
        h2 = pltpu.roll(ext, 2, 0)[tail_rows:]
        return cb_ref[...] + cw_ref[0:1] * h2 + cw_ref[1:2] * h1 + cw_ref[2:3] * ext[tail_rows:]

    def epilogue(r0, r1):
        a_ref[0, r0:r1] = _silu_gate(conv(r0, r1, gate_cols, cwg_ref, cbg_ref),
                                     conv(r0, r1, val_cols, cwv_ref, cbv_ref)).astype(a_ref.dtype)

    pending = None
    for r0, r1 in _row_chunks(tm):
        x = x_ref[0, r0:r1]
        h_ref[tail_rows + r0:tail_rows + r1, gate_cols] = _dot_bf16(x, wg_ref[...])
        h_ref[tail_rows + r0:tail_rows + r1, val_cols] = _dot_bf16(x, wv_ref[...])
        if pending is not None:
            epilogue(*pending)
        pending = (r0, r1)
    epilogue(*pending)
    tail_ref[j] = h_ref[tm:tm + tail_rows]

    h_sample = h_ref[tail_rows + tm - n_dec:tail_rows + tm]
    hsg_ref[0, 0] = h_sample[:, gate_cols]
    hsv_ref[0, 0] = h_sample[:, val_cols]
    seq_tail = h_ref[tm - n_dec:tm - n_dec + tail_rows]
    sg_ref[0, 0] = seq_tail[:, gate_cols]
    sv_ref[0, 0] = seq_tail[:, val_cols]

    @pl.when(half == halves - 1)
    def _():
        def conv_sample(cols, p0_ref, p1_ref, cw_ref, cb_ref):
            return (cb_ref[...] + cw_ref[0:1] * p0_ref[...] + cw_ref[1:2] * p1_ref[...]
                    + cw_ref[2:3] * h_sample[:, cols])

        a_ref[0, tm - n_dec:tm] = _silu_gate(
            conv_sample(gate_cols, p0g_ref, p1g_ref, cwg_ref, cbg_ref),
            conv_sample(val_cols, p0v_ref, p1v_ref, cwv_ref, cbv_ref)).astype(a_ref.dtype)


def ffn_up(h, w_up, conv_w, conv_b, state, *, layer, seq_len, halves, tn, tail_rows):
    n_seq, rows, k = h.shape
    n_dec = rows - seq_len
    d_ff = w_up.shape[2] // 2
    nj = d_ff // tn
    tm = rows // halves
    conv_b = conv_b.reshape(conv_b.shape[0], 1, -1)
    p0, p1 = state[:, 0], state[:, 1]

    def gate(shape):
        return pl.BlockSpec(shape, lambda i, j: (0, j))

    def val(shape):
        return pl.BlockSpec(shape, lambda i, j: (0, j + nj))

    def gate_l(shape):
        return pl.BlockSpec((None,) + shape, lambda i, j: (layer, 0, j))

    def val_l(shape):
        return pl.BlockSpec((None,) + shape, lambda i, j: (layer, 0, j + nj))

    per_tile = lambda r: pl.BlockSpec((1, 1, r, tn), lambda i, j: (i // halves, i % halves, 0, j))
    a, sg, sv, hsg, hsv = pl.pallas_call(
        functools.partial(_ffn_up_kernel, halves=halves, n_dec=n_dec),
        out_shape=(jax.ShapeDtypeStruct((n_seq, rows, d_ff), BF16),
                   jax.ShapeDtypeStruct((n_seq, halves, tail_rows, d_ff), F32),
                   jax.ShapeDtypeStruct((n_seq, halves, tail_rows, d_ff), F32),
                   jax.ShapeDtypeStruct((n_seq, halves, n_dec, d_ff), F32),
                   jax.ShapeDtypeStruct((n_seq, halves, n_dec, d_ff), F32)),
        grid=(n_seq * halves, nj),
        in_specs=[pl.BlockSpec((1, tm, k), lambda i, j: (i // halves, i % halves, 0)),
                  gate_l((k, tn)), val_l((k, tn)), gate_l((CONV_W, tn)), val_l((CONV_W, tn)),
                  gate_l((1, tn)), val_l((1, tn)),
                  gate((n_dec, tn)), val((n_dec, tn)), gate((n_dec, tn)), val((n_dec, tn))],
        out_specs=(pl.BlockSpec((1, tm, tn), lambda i, j: (i // halves, i % halves, j)),
                   per_tile(tail_rows), per_tile(tail_rows), per_tile(n_dec), per_tile(n_dec)),
        scratch_shapes=[pltpu.VMEM((tail_rows + tm, 2 * tn), F32),
                        pltpu.VMEM((nj, tail_rows, 2 * tn), F32)],
        compiler_params=_params("arbitrary", "arbitrary"),
        name="ffn_up",
    )(h, w_up, w_up, conv_w, conv_w, conv_b, conv_b, p0, p0, p1, p1)
    keep = slice(tail_rows - (CONV_W - 1), tail_rows)
    state_prompt = jnp.concatenate([sg[:, halves - 1, keep], sv[:, halves - 1, keep]], axis=-1)
    h_sample = jnp.concatenate([hsg[0, halves - 1], hsv[0, halves - 1]], axis=-1)
    return a, state_prompt, h_sample


ROW_TILES = 2
TN = 512
TN_FFN_UP = V7X_MXU_COLS
TR_PROMPT = 256
TILED_ROW_BLOCKS = 10
FFN_DOWN_ROW_BLOCKS = 5
STATE_TAIL_ROWS = V7X_SUBLANES


def kernel(x_prompt, x_sample, cache_win_k, cache_win_v, state_conv, norm_mix_pre, norm_mix_post,
           norm_ffn_pre, norm_ffn_post, gmlp_w_in, gmlp_ln_g, gmlp_ln_b, gmlp_w_s, gmlp_b_s,
           gmlp_w_out, attn_w_qkv, attn_b_qkv, attn_sinks, attn_w_o, attn_b_o, ffn_w_up,
           ffn_conv_w, ffn_conv_b, ffn_w_down):
    n_seq, seq_len, d = x_prompt.shape
    n_dec = x_sample.shape[0]
    depth = norm_mix_pre.shape[0]
    rows = seq_len + n_dec
    x_sample = x_sample.reshape(1, n_dec, d)
    mm = functools.partial(matmul, row_tiles=ROW_TILES, tn=TN)

    h = rmsnorm_first(x_prompt, x_sample, norm_mix_pre[0], tr=TR_PROMPT)
    x = None
    gv_p, gv_s, wk_p, wv_p, wk_s, wv_s, cv_p, cv_s = [], [], [], [], [], [], [], []
    for layer in range(depth):
        idx = layer // 2
        if layer % 2 == 0:
            d_g = gmlp_w_out.shape[1]
            u = mm(h, gmlp_w_in, layer=idx, n_out=d_g, out_dtype=BF16, epilogue=_gelu, name="gmlp_in_u")
            v = mm(h, gmlp_w_in, layer=idx, n_out=d_g, col_block_offset=d_g // TN, epilogue=_gelu,
                   name="gmlp_in_v")
            gated, g_p, g_s = spatial_gate(v, u, gmlp_ln_g[idx], gmlp_ln_b[idx], gmlp_w_s[idx],
                                           gmlp_b_s[idx], seq_len=seq_len)
            gv_p.append(g_p)
            gv_s.append(g_s.reshape(n_dec, 1, d_g))
            y = mm(gated, gmlp_w_out, layer=idx, n_out=d, name="gmlp_out")
        else:
            d_q = attn_w_o.shape[1]
            d_qkv = attn_w_qkv.shape[2]
            d_kv = (d_qkv - d_q) // 2
            qkv = mm(h, attn_w_qkv, attn_b_qkv, layer=idx, n_out=d_qkv, name="attn_qkv")
            o = attention_prompt(qkv, attn_sinks[idx], seq_len=seq_len)
            o_s, k_new, v_new = attention_sample(qkv[0, seq_len:], cache_win_k[idx], cache_win_v[idx],
                                                 attn_sinks[idx])
            o = insert_sample_rows(o, o_s, seq_len=seq_len)
            tail = qkv[:, seq_len - WINDOW:seq_len]
            wk_p.append(tail[:, :, d_q:d_q + d_kv].reshape(n_seq, WINDOW, N_KV_HEADS, HEAD_DIM))
            wv_p.append(tail[:, :, d_q + d_kv:].reshape(n_seq, WINDOW, N_KV_HEADS, HEAD_DIM))
            wk_s.append(jnp.concatenate(
                [cache_win_k[idx][:, 1:], k_new.reshape(n_dec, 1, N_KV_HEADS, HEAD_DIM)], axis=1))
            wv_s.append(jnp.concatenate(
                [cache_win_v[idx][:, 1:], v_new.reshape(n_dec, 1, N_KV_HEADS, HEAD_DIM)], axis=1))
            y = mm(o, attn_w_o, attn_b_o, layer=idx, n_out=d, name="attn_out")
        if x is None:
            x, h = resnorm_first(x_prompt, x_sample, y, norm_mix_post[layer], norm_ffn_pre[layer],
                                 tr=TR_PROMPT)
        else:
            x, h = resnorm_tiled(x, y, norm_mix_post[layer], norm_ffn_pre[layer],
                                 tr=rows // TILED_ROW_BLOCKS)

        a, c_p, hu_s = ffn_up(h, ffn_w_up, ffn_conv_w, ffn_conv_b, state_conv[layer], layer=layer,
                              seq_len=seq_len, halves=ROW_TILES, tn=TN_FFN_UP, tail_rows=STATE_TAIL_ROWS)
        cv_p.append(c_p)
        cv_s.append(jnp.concatenate([state_conv[layer][:, 1:], hu_s[:, None]], axis=1))
        f = matmul_weight_resident(a, ffn_w_down, layer=layer, tm=rows // FFN_DOWN_ROW_BLOCKS, tn=TN,
                                   name="ffn_down")
        if layer + 1 < depth:
            x, h = resnorm_tiled(x, f, norm_ffn_post[layer], norm_mix_pre[layer + 1],
                                 tr=rows // TILED_ROW_BLOCKS)
        else:
            y_prompt, y_sample = resnorm_last(x, f, norm_ffn_post[layer], seq_len=seq_len, n_dec=n_dec,
                                              tr=TR_PROMPT)

    return (y_prompt, y_sample, jnp.stack(gv_p), jnp.stack(gv_s),
            jnp.stack(wk_p), jnp.stack(wv_p), jnp.stack(wk_s), jnp.stack(wv_s),
            jnp.stack(cv_p), jnp.stack(cv_s))
```

```python
import functools
import math

import jax
import jax.numpy as jnp
from jax import lax
from jax.experimental import pallas as pl
from jax.experimental.pallas import tpu as pltpu

F32 = jnp.float32
BF16 = jnp.bfloat16

NORM_EPS = 1e-6
CHUNK = 128
N_GROUPS = 16
HEAD_DIM = 64
N_KV_HEADS = 8
Q_PER_KV = 8
WINDOW = 128
CONV_W = 3
ATTN_SCALE = HEAD_DIM ** -0.5
MASKED_SCORE = -1e30
LOG2_E = math.log2(math.e)

V7X_LANES = 128
V7X_SUBLANES = 8
V7X_MXU_COLS = 256
V7X_SCOPED_VMEM_BYTES = 60000 * 1024

DOT_ROWS = 512
FFN_UP_DOT_ROWS = 256


def _params(*semantics):
    return pltpu.CompilerParams(dimension_semantics=semantics,
                                vmem_limit_bytes=V7X_SCOPED_VMEM_BYTES)


def _rms(x, g):
    return x * lax.rsqrt(jnp.mean(x * x, axis=-1, keepdims=True) + NORM_EPS) * g


def _gelu(x):
    return 0.5 * x * (1.0 + lax.erf(x * math.sqrt(0.5)))


def _identity(x):
    return x


def _dot_bf16(x, w):
    return jnp.dot(x, w.astype(BF16), preferred_element_type=F32)


def _row_chunks(rows, size=DOT_ROWS):
    n = max(rows // size, 1)
    return [(c * size, (c + 1) * size if c + 1 < n else rows) for c in range(n)]


def _skip_aliased(body, n_aliased):
    def kernel_fn(*refs):
        body(*refs[n_aliased:])
    return kernel_fn


def _rmsnorm_kernel(x_ref, g_ref, h_ref):
    h_ref[0] = _rms(x_ref[0], g_ref[...]).astype(h_ref.dtype)


def _resnorm_kernel(x_ref, y_ref, gpost_ref, gnext_ref, xo_ref, ho_ref):
    xn = x_ref[0] + _rms(y_ref[0].astype(F32), gpost_ref[...])
    xo_ref[0] = xn
    ho_ref[0] = _rms(xn, gnext_ref[...]).astype(ho_ref.dtype)


def _resnorm_last_kernel(x_ref, y_ref, gpost_ref, xo_ref):
    xo_ref[0] = x_ref[0] + _rms(y_ref[0].astype(F32), gpost_ref[...])


def rmsnorm_first(x_prompt, x_sample, g, *, tr):
    n_seq, seq_len, d = x_prompt.shape
    n_dec = x_sample.shape[1]
    g = g.reshape(1, d)
    shape = jax.ShapeDtypeStruct((n_seq, seq_len + n_dec, d), BF16)
    row = pl.BlockSpec((1, tr, d), lambda s, r: (s, r, 0))
    h = pl.pallas_call(
        _rmsnorm_kernel, out_shape=shape, grid=(n_seq, seq_len // tr),
        in_specs=[row, pl.BlockSpec((1, d), lambda s, r: (0, 0))], out_specs=row,
        compiler_params=_params("parallel", "parallel"), name="rmsnorm_first_prompt",
    )(x_prompt, g)
    return pl.pallas_call(
        _skip_aliased(_rmsnorm_kernel, 1), out_shape=shape, grid=(n_seq,),
        in_specs=[pl.BlockSpec(memory_space=pl.ANY),
                  pl.BlockSpec((1, n_dec, d), lambda s: (0, 0, 0)),
                  pl.BlockSpec((1, d), lambda s: (0, 0))],
        out_specs=pl.BlockSpec((1, n_dec, d), lambda s: (s, seq_len // n_dec, 0)),
        input_output_aliases={0: 0},
        compiler_params=_params("arbitrary"), name="rmsnorm_first_sample",
    )(h, x_sample, g)


def resnorm_first(x_prompt, x_sample, y, g_post, g_next, *, tr):
    n_seq, seq_len, d = x_prompt.shape
    n_dec = x_sample.shape[1]
    g_post, g_next = g_post.reshape(1, d), g_next.reshape(1, d)
    shapes = (jax.ShapeDtypeStruct(y.shape, F32), jax.ShapeDtypeStruct(y.shape, BF16))
    row = pl.BlockSpec((1, tr, d), lambda s, r: (s, r, 0))
    vec = pl.BlockSpec((1, d), lambda s, r: (0, 0))
    xo, ho = pl.pallas_call(
        _resnorm_kernel, out_shape=shapes, grid=(n_seq, seq_len // tr),
        in_specs=[row, row, vec, vec], out_specs=(row, row),
        compiler_params=_params("parallel", "parallel"), name="resnorm_first_prompt",
    )(x_prompt, y, g_post, g_next)
    sample_rows = pl.BlockSpec((1, n_dec, d), lambda s: (s, seq_len // n_dec, 0))
    vec1 = pl.BlockSpec((1, d), lambda s: (0, 0))
    return pl.pallas_call(
        _skip_aliased(_resnorm_kernel, 2), out_shape=shapes, grid=(n_seq,),
        in_specs=[pl.BlockSpec(memory_space=pl.ANY), pl.BlockSpec(memory_space=pl.ANY),
                  pl.BlockSpec((1, n_dec, d), lambda s: (0, 0, 0)),
                  pl.BlockSpec((1, n_dec, d), lambda s: (0, seq_len // n_dec, 0)), vec1, vec1],
        out_specs=(sample_rows, sample_rows),
        input_output_aliases={0: 0, 1: 1},
        compiler_params=_params("arbitrary"), name="resnorm_first_sample",
    )(xo, ho, x_sample, y, g_post, g_next)


def resnorm_tiled(x, y, g_post, g_next, *, tr):
    n_seq, rows, d = x.shape
    row = pl.BlockSpec((1, tr, d), lambda s, r: (s, r, 0))
    vec = pl.BlockSpec((1, d), lambda s, r: (0, 0))
    return pl.pallas_call(
        _resnorm_kernel,
        out_shape=(jax.ShapeDtypeStruct(x.shape, F32), jax.ShapeDtypeStruct(x.shape, BF16)),
        grid=(n_seq, rows // tr),
        in_specs=[row, row, vec, vec], out_specs=(row, row),
        compiler_params=_params("parallel", "parallel"), name="resnorm_tiled",
    )(x, y, g_post.reshape(1, d), g_next.reshape(1, d))


def resnorm_last(x, y, g_post, *, seq_len, n_dec, tr):
    n_seq, _, d = x.shape
    g_post = g_post.reshape(1, d)
    row = pl.BlockSpec((1, tr, d), lambda s, r: (s, r, 0))
    y_prompt = pl.pallas_call(
        _resnorm_last_kernel, out_shape=jax.ShapeDtypeStruct((n_seq, seq_len, d), F32),
        grid=(n_seq, seq_len // tr),
        in_specs=[row, row, pl.BlockSpec((1, d), lambda s, r: (0, 0))], out_specs=row,
        compiler_params=_params("parallel", "parallel"), name="resnorm_last_prompt",
    )(x, y, g_post)
    sample_rows = pl.BlockSpec((1, n_dec, d), lambda s: (0, seq_len // n_dec, 0))
    y_sample = pl.pallas_call(
        _resnorm_last_kernel, out_shape=jax.ShapeDtypeStruct((1, n_dec, d), F32), grid=(1,),
        in_specs=[sample_rows, sample_rows, pl.BlockSpec((1, d), lambda s: (0, 0))],
        out_specs=pl.BlockSpec((1, n_dec, d), lambda s: (0, 0, 0)),
        compiler_params=_params("arbitrary"), name="resnorm_last_sample",
    )(x, y, g_post)
    return y_prompt, y_sample.reshape(n_dec, 1, d)


def _copy_rows_kernel(src_ref, o_ref):
    o_ref[0] = src_ref[...]


def insert_sample_rows(tiled, rows, *, seq_len):
    n_seq, _, c = tiled.shape
    n_dec = rows.shape[0]
    return pl.pallas_call(
        _skip_aliased(_copy_rows_kernel, 1), out_shape=jax.ShapeDtypeStruct(tiled.shape, tiled.dtype),
        grid=(n_seq,),
        in_specs=[pl.BlockSpec(memory_space=pl.ANY), pl.BlockSpec((n_dec, c), lambda s: (0, 0))],
        out_specs=pl.BlockSpec((1, n_dec, c), lambda s: (s, seq_len // n_dec, 0)),
        input_output_aliases={0: 0},
        compiler_params=_params("arbitrary"), name="insert_sample_rows",
    )(tiled, rows)


def _matmul_kernel(*refs, epilogue, has_bias):
    if has_bias:
        x_ref, w_ref, b_ref, o_ref = refs
    else:
        x_ref, w_ref, o_ref = refs
    for r0, r1 in _row_chunks(x_ref.shape[1]):
        acc = _dot_bf16(x_ref[0, r0:r1], w_ref[...])
        if has_bias:
            acc = acc + b_ref[...]
        o_ref[0, r0:r1] = epilogue(acc).astype(o_ref.dtype)


def matmul(x, w, bias=None, *, layer, row_tiles, tn, n_out, col_block_offset=0, out_dtype=F32,
           epilogue=_identity, name):
    n_seq, rows, k = x.shape
    tm = rows // row_tiles
    in_specs = [pl.BlockSpec((1, tm, k), lambda i, j: (i // row_tiles, i % row_tiles, 0)),
                pl.BlockSpec((None, k, tn), lambda i, j: (layer, 0, j + col_block_offset))]
    args = [x, w]
    if bias is not None:
        in_specs.append(pl.BlockSpec((None, 1, tn), lambda i, j: (layer, 0, j + col_block_offset)))
        args.append(bias.reshape(bias.shape[0], 1, -1))
    return pl.pallas_call(
        functools.partial(_matmul_kernel, epilogue=epilogue, has_bias=bias is not None),
        out_shape=jax.ShapeDtypeStruct((n_seq, rows, n_out), out_dtype),
        grid=(n_seq * row_tiles, n_out // tn),
        in_specs=in_specs,
        out_specs=pl.BlockSpec((1, tm, tn), lambda i, j: (i // row_tiles, i % row_tiles, j)),
        compiler_params=_params("parallel", "arbitrary"),
        name=name,
    )(*args)


def _matmul_wres_kernel(x_ref, w_ref, o_ref, wb_ref):
    @pl.when(pl.program_id(1) == 0)
    def _():
        wb_ref[...] = w_ref[...].astype(BF16)

    o_ref[0] = jnp.dot(x_ref[0], wb_ref[...], preferred_element_type=F32).astype(o_ref.dtype)


def matmul_weight_resident(x, w, *, layer, tm, tn, name):
    n_seq, rows, k = x.shape
    n = w.shape[2]
    per_seq = rows // tm
    return pl.pallas_call(
        _matmul_wres_kernel,
        out_shape=jax.ShapeDtypeStruct((n_seq, rows, n), BF16),
        grid=(n // tn, n_seq * per_seq),
        in_specs=[pl.BlockSpec((1, tm, k), lambda j, i: (i // per_seq, i % per_seq, 0)),
                  pl.BlockSpec((None, k, tn), lambda j, i: (layer, 0, j), pipeline_mode=pl.Buffered(1))],
        out_specs=pl.BlockSpec((1, tm, tn), lambda j, i: (i // per_seq, i % per_seq, j)),
        scratch_shapes=[pltpu.VMEM((k, tn), BF16)],
        compiler_params=_params("arbitrary", "arbitrary"),
        name=name,
    )(x, w)


def _layer_norm(v, g, b):
    xc = v - jnp.mean(v, axis=-1, keepdims=True)
    return xc * lax.rsqrt(jnp.mean(xc * xc, axis=-1, keepdims=True) + NORM_EPS) * g + b


def _spatial_kernel(v_ref, u_ref, lng_ref, lnb_ref, ws_ref, bias_ref, o_ref, gv_ref, *, group_dim):
    vn = _layer_norm(v_ref[0], lng_ref[...], lnb_ref[...])
    gv_ref[0] = vn
    vb = vn.astype(BF16)
    t = lax.broadcasted_iota(jnp.int32, (CHUNK, CHUNK), 0)
    s = lax.broadcasted_iota(jnp.int32, (CHUNK, CHUNK), 1)
    causal = s <= t
    for g in range(N_GROUPS):
        cols = slice(g * group_dim, (g + 1) * group_dim)
        wc = jnp.where(causal, ws_ref[g], 0.0).astype(BF16)
        mix = jnp.dot(wc, vb[:, cols], preferred_element_type=F32) + bias_ref[:, cols]
        o_ref[0, :, cols] = (u_ref[0, :, cols].astype(F32) * mix).astype(o_ref.dtype)


def _spatial_sample_kernel(v_ref, u_ref, lng_ref, lnb_ref, w00_ref, b0_ref, o_ref, gv_ref):
    vn = _layer_norm(v_ref[0], lng_ref[...], lnb_ref[...])
    gv_ref[0] = vn
    mix = w00_ref[...].astype(BF16).astype(F32) * vn.astype(BF16).astype(F32) + b0_ref[...]
    o_ref[0] = (u_ref[0].astype(F32) * mix).astype(o_ref.dtype)


def spatial_gate(v, u, ln_g, ln_b, w_s, b_s, *, seq_len):
    n_seq, rows, d = v.shape
    n_dec = rows - seq_len
    group_dim = d // N_GROUPS
    ln_g, ln_b = ln_g.reshape(1, d), ln_b.reshape(1, d)
    bias_full = jnp.repeat(b_s.T, group_dim, axis=1)
    row = pl.BlockSpec((1, CHUNK, d), lambda s, c: (s, c, 0))
    vec = pl.BlockSpec((1, d), lambda s, c: (0, 0))
    gated, gv_prompt = pl.pallas_call(
        functools.partial(_spatial_kernel, group_dim=group_dim),
        out_shape=(jax.ShapeDtypeStruct(v.shape, BF16),
                   jax.ShapeDtypeStruct((n_seq, CHUNK, d), F32)),
        grid=(n_seq, seq_len // CHUNK),
        in_specs=[row, row, vec, vec,
                  pl.BlockSpec((N_GROUPS, CHUNK, CHUNK), lambda s, c: (0, 0, 0)),
                  pl.BlockSpec((CHUNK, d), lambda s, c: (0, 0))],
        out_specs=(row, pl.BlockSpec((1, CHUNK, d), lambda s, c: (s, 0, 0))),
        compiler_params=_params("parallel", "arbitrary"),
        name="spatial_gate_prompt",
    )(v, u, ln_g, ln_b, w_s, bias_full)
    w00 = jnp.repeat(w_s[:, 0, 0], group_dim).reshape(1, d)
    b0 = jnp.repeat(b_s[:, 0], group_dim).reshape(1, d)
    tile0_rows = pl.BlockSpec((1, n_dec, d), lambda s: (0, seq_len // n_dec, 0))
    vec1 = pl.BlockSpec((1, d), lambda s: (0, 0))
    gated, gv_sample = pl.pallas_call(
        _skip_aliased(_spatial_sample_kernel, 1),
        out_shape=(jax.ShapeDtypeStruct(v.shape, BF16), jax.ShapeDtypeStruct((1, n_dec, d), F32)),
        grid=(n_seq,),
        in_specs=[pl.BlockSpec(memory_space=pl.ANY), tile0_rows, tile0_rows, vec1, vec1, vec1, vec1],
        out_specs=(pl.BlockSpec((1, n_dec, d), lambda s: (s, seq_len // n_dec, 0)),
                   pl.BlockSpec((1, n_dec, d), lambda s: (0, 0, 0))),
        input_output_aliases={0: 0},
        compiler_params=_params("arbitrary"),
        name="spatial_gate_sample",
    )(gated, v, u, ln_g, ln_b, w00, b0)
    return gated, gv_prompt, gv_sample


def _pair_block_diag(pair, even):
    lane = lax.broadcasted_iota(jnp.int32, pair.shape, 1)
    if even:
        own = jnp.where(lane < HEAD_DIM, pair, 0.0)
        return jnp.concatenate([own, pltpu.roll(own, HEAD_DIM, 1)], axis=0)
    own = jnp.where(lane >= HEAD_DIM, pair, 0.0)
    return jnp.concatenate([pltpu.roll(own, HEAD_DIM, 1), own], axis=0)


def _attn_prompt_kernel(sink_ref, q_ref, kprev_ref, kown_ref, vprev_ref, vown_ref, o_ref):
    blk = pl.program_id(1)
    n_keys = 2 * WINDOW
    qi = lax.broadcasted_iota(jnp.int32, (WINDOW, n_keys), 0)
    kj = lax.broadcasted_iota(jnp.int32, (WINDOW, n_keys), 1)
    first_key = jnp.where(blk > 0, 0, WINDOW)
    valid = (kj >= qi) & (kj <= qi + WINDOW) & (kj >= first_key)
    col_blocks = Q_PER_KV * HEAD_DIM // V7X_LANES
    valid = jnp.concatenate([valid] * col_blocks, axis=0)
    block_of_row = lax.broadcasted_iota(jnp.int32, (col_blocks * WINDOW, 1), 0) // WINDOW
    head_lane = lax.broadcasted_iota(jnp.int32, (col_blocks * WINDOW, V7X_LANES), 1)
    group_w = Q_PER_KV * HEAD_DIM
    for pair in range(N_KV_HEADS // 2):
        lanes = slice(pair * V7X_LANES, (pair + 1) * V7X_LANES)
        k_pair = jnp.concatenate([kprev_ref[0, :, lanes], kown_ref[0, :, lanes]], axis=0)
        v_pair = jnp.concatenate([vprev_ref[0, :, lanes], vown_ref[0, :, lanes]], axis=0)
        for e in range(2):
            h = 2 * pair + e
            k2 = _pair_block_diag(k_pair, e == 0).astype(BF16)
            v2 = _pair_block_diag(v_pair, e == 0).astype(BF16)
            qs = jnp.concatenate(
                [q_ref[0, :, h * group_w + c * V7X_LANES: h * group_w + (c + 1) * V7X_LANES]
                 for c in range(col_blocks)], axis=0)
            qs = (qs * (ATTN_SCALE * LOG2_E)).astype(BF16)
            s = lax.dot_general(qs, k2, (((1,), (1,)), ((), ())), preferred_element_type=F32)
            probs, inv_denoms = [], []
            for half in range(2):
                sh = jnp.where(valid, s[:, half * n_keys:(half + 1) * n_keys], MASKED_SCORE)
                sink = jnp.zeros((col_blocks * WINDOW, 1), F32)
                for c in range(col_blocks):
                    sink = jnp.where(block_of_row == c, sink_ref[h * Q_PER_KV + 2 * c + half] * LOG2_E,
                                     sink)
                m = jnp.maximum(jnp.max(sh, axis=-1, keepdims=True), sink)
                p = jnp.exp2(sh - m)
                inv_denoms.append(1.0 / (jnp.sum(p, axis=-1, keepdims=True) + jnp.exp2(sink - m)))
                probs.append(p.astype(BF16))
            o = jnp.dot(jnp.concatenate(probs, axis=1), v2, preferred_element_type=F32)
            o = o * jnp.where(head_lane < HEAD_DIM, inv_denoms[0], inv_denoms[1])
            for c in range(col_blocks):
                o_ref[0, :, h * group_w + c * V7X_LANES: h * group_w + (c + 1) * V7X_LANES] = (
                    o[c * WINDOW:(c + 1) * WINDOW].astype(o_ref.dtype))


def attention_prompt(qkv, sinks, *, seq_len):
    n_seq, rows, _ = qkv.shape
    d_q = N_KV_HEADS * Q_PER_KV * HEAD_DIM
    d_kv = N_KV_HEADS * HEAD_DIM
    k_col = d_q // d_kv
    v_col = k_col + 1

    def own(col):
        return lambda s, i: (s, i, col)

    def prev(col):
        return lambda s, i: (s, jnp.maximum(i - 1, 0), col)

    return pl.pallas_call(
        _attn_prompt_kernel,
        out_shape=jax.ShapeDtypeStruct((n_seq, rows, d_q), BF16),
        grid=(n_seq, seq_len // WINDOW),
        in_specs=[pl.BlockSpec(memory_space=pltpu.SMEM),
                  pl.BlockSpec((1, WINDOW, d_q), own(0)),
                  pl.BlockSpec((1, WINDOW, d_kv), prev(k_col)),
                  pl.BlockSpec((1, WINDOW, d_kv), own(k_col)),
                  pl.BlockSpec((1, WINDOW, d_kv), prev(v_col)),
                  pl.BlockSpec((1, WINDOW, d_kv), own(v_col))],
        out_specs=pl.BlockSpec((1, WINDOW, d_q), own(0)),
        compiler_params=_params("parallel", "arbitrary"),
        name="attention_prompt",
    )(sinks, qkv, qkv, qkv, qkv, qkv)


def _attn_sample_kernel(q2_ref, ck_ref, cv_ref, knew_ref, vnew_ref, sink_ref, o_ref):
    for pair in range(N_KV_HEADS // 2):
        lanes = slice(pair * V7X_LANES, (pair + 1) * V7X_LANES)
        q2 = q2_ref[0, pair]
        k_pair = ck_ref[0, :, lanes].astype(BF16)
        v_pair = cv_ref[0, :, lanes].astype(BF16)
        k_new = knew_ref[0, :, lanes].astype(BF16).astype(F32)
        v_new = vnew_ref[0, :, lanes].astype(BF16).astype(F32)
        sink = sink_ref[pair][:, :1]
        s = lax.dot_general(q2, k_pair, (((1,), (1,)), ((), ())), preferred_element_type=F32)
        s_new = jnp.sum(q2.astype(F32) * k_new, axis=-1, keepdims=True)
        m = jnp.maximum(jnp.maximum(jnp.max(s, axis=-1, keepdims=True), s_new), sink)
        p = jnp.exp(s - m)
        p_new = jnp.exp(s_new - m)
        denom = jnp.sum(p, axis=-1, keepdims=True) + p_new + jnp.exp(sink - m)
        o = jnp.dot((p / denom).astype(BF16), v_pair, preferred_element_type=F32)
        o_ref[0, pair] = o + (p_new / denom).astype(BF16).astype(F32) * v_new


def attention_sample(qkv, cache_k, cache_v, sinks):
    n = qkv.shape[0]
    d_q = N_KV_HEADS * Q_PER_KV * HEAD_DIM
    d_kv = N_KV_HEADS * HEAD_DIM
    n_pairs = N_KV_HEADS // 2
    rows = 2 * Q_PER_KV
    q = (qkv[:, :d_q] * ATTN_SCALE).astype(BF16).reshape(n, n_pairs, 2, Q_PER_KV, HEAD_DIM)
    zeros = jnp.zeros((n, n_pairs, Q_PER_KV, HEAD_DIM), BF16)
    q2 = jnp.concatenate([jnp.concatenate([q[:, :, 0], zeros], axis=-1),
                          jnp.concatenate([zeros, q[:, :, 1]], axis=-1)], axis=2)
    k_new = qkv[:, d_q:d_q + d_kv].reshape(n, 1, d_kv)
    v_new = qkv[:, d_q + d_kv:].reshape(n, 1, d_kv)
    sink2 = jnp.broadcast_to(sinks.reshape(n_pairs, rows, 1), (n_pairs, rows, V7X_LANES))
    o2 = pl.pallas_call(
        _attn_sample_kernel,
        out_shape=jax.ShapeDtypeStruct((n, n_pairs, rows, V7X_LANES), F32),
        grid=(n,),
        in_specs=[pl.BlockSpec((1, n_pairs, rows, V7X_LANES), lambda b: (b, 0, 0, 0)),
                  pl.BlockSpec((1, WINDOW, d_kv), lambda b: (b, 0, 0)),
                  pl.BlockSpec((1, WINDOW, d_kv), lambda b: (b, 0, 0)),
                  pl.BlockSpec((1, 1, d_kv), lambda b: (b, 0, 0)),
                  pl.BlockSpec((1, 1, d_kv), lambda b: (b, 0, 0)),
                  pl.BlockSpec((n_pairs, rows, V7X_LANES), lambda b: (0, 0, 0))],
        out_specs=pl.BlockSpec((1, n_pairs, rows, V7X_LANES), lambda b: (b, 0, 0, 0)),
        compiler_params=_params("parallel"),
        name="attention_sample",
    )(q2, cache_k.reshape(n, WINDOW, d_kv), cache_v.reshape(n, WINDOW, d_kv), k_new, v_new, sink2)
    o = jnp.stack([o2[:, :, :Q_PER_KV, :HEAD_DIM], o2[:, :, Q_PER_KV:, HEAD_DIM:]], axis=2)
    return o.reshape(n, d_q).astype(BF16), k_new, v_new


def _silu_gate(gate, val):
    return gate * (1.0 / (1.0 + jnp.exp(-gate))) * val


def _ffn_up_kernel(x_ref, wg_ref, wv_ref, cwg_ref, cwv_ref, cbg_ref, cbv_ref,
                   p0g_ref, p0v_ref, p1g_ref, p1v_ref,
                   a_ref, sg_ref, sv_ref, hsg_ref, hsv_ref, xf_ref, wb_ref, h_ref, tail_ref,
                   *, halves, n_dec, dot_rows):
    half = pl.program_id(0) % halves
    j = pl.program_id(1)
    tn = wg_ref.shape[1]
    tail_rows = tail_ref.shape[1]
    tm = x_ref.shape[1]
    gate_cols, val_cols = slice(0, tn), slice(tn, 2 * tn)

    @pl.when((pl.program_id(0) == 0) & (j == 0))
    def _():
        tail_ref[...] = jnp.zeros(tail_ref.shape, tail_ref.dtype)

    @pl.when(j == 0)
    def _():
        xf_ref[...] = x_ref[0].astype(F32)

    wb_ref[:, gate_cols] = wg_ref[...].astype(BF16)
    wb_ref[:, val_cols] = wv_ref[...].astype(BF16)

    first_tile = jnp.full((tail_rows, 2 * tn), half, jnp.int32) == 0
    h_ref[0:tail_rows] = jnp.where(first_tile, 0.0, tail_ref[j])

    def conv(r0, r1, cols, cw_ref, cb_ref):
        ext = h_ref[r0:r1 + tail_rows, cols]
        h1 = pltpu.roll(ext, 1, 0)[tail_rows:]
        h2 = pltpu.roll(ext, 2, 0)[tail_rows:]
        return cb_ref[...] + cw_ref[0:1] * h2 + cw_ref[1:2] * h1 + cw_ref[2:3] * ext[tail_rows:]

    def epilogue(r0, r1):
        a_ref[0, r0:r1] = _silu_gate(conv(r0, r1, gate_cols, cwg_ref, cbg_ref),
                                     conv(r0, r1, val_cols, cwv_ref, cbv_ref)).astype(a_ref.dtype)

    pending = None
    for r0, r1 in _row_chunks(tm, dot_rows):
        h_ref[tail_rows + r0:tail_rows + r1] = lax.dot_general(
            xf_ref[r0:r1], wb_ref[...], (((1,), (0,)), ((), ())), preferred_element_type=F32)
        if pending is not None:
            epilogue(*pending)
        pending = (r0, r1)
    epilogue(*pending)
    tail_ref[j] = h_ref[tm:tm + tail_rows]

    h_sample = h_ref[tail_rows + tm - n_dec:tail_rows + tm]
    hsg_ref[0, 0] = h_sample[:, gate_cols]
    hsv_ref[0, 0] = h_sample[:, val_cols]
    seq_tail = h_ref[tm - n_dec:tm - n_dec + tail_rows]
    sg_ref[0, 0] = seq_tail[:, gate_cols]
    sv_ref[0, 0] = seq_tail[:, val_cols]

    @pl.when(half == halves - 1)
    def _():
        def conv_sample(cols, p0_ref, p1_ref, cw_ref, cb_ref):
            return (cb_ref[...] + cw_ref[0:1] * p0_ref[...] + cw_ref[1:2] * p1_ref[...]
                    + cw_ref[2:3] * h_sample[:, cols])

        a_ref[0, tm - n_dec:tm] = _silu_gate(
            conv_sample(gate_cols, p0g_ref, p1g_ref, cwg_ref, cbg_ref),
            conv_sample(val_cols, p0v_ref, p1v_ref, cwv_ref, cbv_ref)).astype(a_ref.dtype)


def ffn_up(h, w_up, conv_w, conv_b, state, *, layer, seq_len, halves, tn, tail_rows):
    n_seq, rows, k = h.shape
    n_dec = rows - seq_len
    d_ff = w_up.shape[2] // 2
    nj = d_ff // tn
    tm = rows // halves
    conv_b = conv_b.reshape(conv_b.shape[0], 1, -1)
    p0, p1 = state[:, 0], state[:, 1]

    def gate(shape):
        return pl.BlockSpec(shape, lambda i, j: (0, j))

    def val(shape):
        return pl.BlockSpec(shape, lambda i, j: (0, j + nj))

    def gate_l(shape):
        return pl.BlockSpec((None,) + shape, lambda i, j: (layer, 0, j))

    def val_l(shape):
        return pl.BlockSpec((None,) + shape, lambda i, j: (layer, 0, j + nj))

    per_tile = lambda r: pl.BlockSpec((1, 1, r, tn), lambda i, j: (i // halves, i % halves, 0, j))
    a, sg, sv, hsg, hsv = pl.pallas_call(
        functools.partial(_ffn_up_kernel, halves=halves, n_dec=n_dec, dot_rows=FFN_UP_DOT_ROWS),
        out_shape=(jax.ShapeDtypeStruct((n_seq, rows, d_ff), BF16),
                   jax.ShapeDtypeStruct((n_seq, halves, tail_rows, d_ff), F32),
                   jax.ShapeDtypeStruct((n_seq, halves, tail_rows, d_ff), F32),
                   jax.ShapeDtypeStruct((n_seq, halves, n_dec, d_ff), F32),
                   jax.ShapeDtypeStruct((n_seq, halves, n_dec, d_ff), F32)),
        grid=(n_seq * halves, nj),
        in_specs=[pl.BlockSpec((1, tm, k), lambda i, j: (i // halves, i % halves, 0),
                               pipeline_mode=pl.Buffered(1)),
                  gate_l((k, tn)), val_l((k, tn)), gate_l((CONV_W, tn)), val_l((CONV_W, tn)),
                  gate_l((1, tn)), val_l((1, tn)),
                  gate((n_dec, tn)), val((n_dec, tn)), gate((n_dec, tn)), val((n_dec, tn))],
        out_specs=(pl.BlockSpec((1, tm, tn), lambda i, j: (i // halves, i % halves, j)),
                   per_tile(tail_rows), per_tile(tail_rows), per_tile(n_dec), per_tile(n_dec)),
        scratch_shapes=[pltpu.VMEM((tm, k), F32), pltpu.VMEM((k, 2 * tn), BF16),
                        pltpu.VMEM((tail_rows + tm, 2 * tn), F32),
                        pltpu.VMEM((nj, tail_rows, 2 * tn), F32)],
        compiler_params=_params("arbitrary", "arbitrary"),
        name="ffn_up",
    )(h, w_up, w_up, conv_w, conv_w, conv_b, conv_b, p0, p0, p1, p1)
    keep = slice(tail_rows - (CONV_W - 1), tail_rows)
    state_prompt = jnp.concatenate([sg[:, halves - 1, keep], sv[:, halves - 1, keep]], axis=-1)
    h_sample = jnp.concatenate([hsg[0, halves - 1], hsv[0, halves - 1]], axis=-1)
    return a, state_prompt, h_sample


ROW_TILES = 2
TN = 512
TN_FFN_UP = V7X_MXU_COLS
TR_PROMPT = 256
TILED_ROW_BLOCKS = 10
FFN_DOWN_ROW_BLOCKS = 5
STATE_TAIL_ROWS = V7X_SUBLANES


def kernel(x_prompt, x_sample, cache_win_k, cache_win_v, state_conv, norm_mix_pre, norm_mix_post,
           norm_ffn_pre, norm_ffn_post, gmlp_w_in, gmlp_ln_g, gmlp_ln_b, gmlp_w_s, gmlp_b_s,
           gmlp_w_out, attn_w_qkv, attn_b_qkv, attn_sinks, attn_w_o, attn_b_o, ffn_w_up,
           ffn_conv_w, ffn_conv_b, ffn_w_down):
    n_seq, seq_len, d = x_prompt.shape
    n_dec = x_sample.shape[0]
    depth = norm_mix_pre.shape[0]
    rows = seq_len + n_dec
    x_sample = x_sample.reshape(1, n_dec, d)
    mm = functools.partial(matmul, row_tiles=ROW_TILES, tn=TN)

    h = rmsnorm_first(x_prompt, x_sample, norm_mix_pre[0], tr=TR_PROMPT)
    x = None
    gv_p, gv_s, wk_p, wv_p, wk_s, wv_s, cv_p, cv_s = [], [], [], [], [], [], [], []
    for layer in range(depth):
        idx = layer // 2
        if layer % 2 == 0:
            d_g = gmlp_w_out.shape[1]
            u = mm(h, gmlp_w_in, layer=idx, n_out=d_g, out_dtype=BF16, epilogue=_gelu, name="gmlp_in_u")
            v = mm(h, gmlp_w_in, layer=idx, n_out=d_g, col_block_offset=d_g // TN, epilogue=_gelu,
                   name="gmlp_in_v")
            gated, g_p, g_s = spatial_gate(v, u, gmlp_ln_g[idx], gmlp_ln_b[idx], gmlp_w_s[idx],
                                           gmlp_b_s[idx], seq_len=seq_len)
            gv_p.append(g_p)
            gv_s.append(g_s.reshape(n_dec, 1, d_g))
            y = mm(gated, gmlp_w_out, layer=idx, n_out=d, out_dtype=BF16, name="gmlp_out")
        else:
            d_q = attn_w_o.shape[1]
            d_qkv = attn_w_qkv.shape[2]
            d_kv = (d_qkv - d_q) // 2
            qkv = mm(h, attn_w_qkv, attn_b_qkv, layer=idx, n_out=d_qkv, name="attn_qkv")
            o = attention_prompt(qkv, attn_sinks[idx], seq_len=seq_len)
            o_s, k_new, v_new = attention_sample(qkv[0, seq_len:], cache_win_k[idx], cache_win_v[idx],
                                                 attn_sinks[idx])
            o = insert_sample_rows(o, o_s, seq_len=seq_len)
            tail = qkv[:, seq_len - WINDOW:seq_len]
            wk_p.append(tail[:, :, d_q:d_q + d_kv].reshape(n_seq, WINDOW, N_KV_HEADS, HEAD_DIM))
            wv_p.append(tail[:, :, d_q + d_kv:].reshape(n_seq, WINDOW, N_KV_HEADS, HEAD_DIM))
            wk_s.append(jnp.concatenate(
                [cache_win_k[idx][:, 1:], k_new.reshape(n_dec, 1, N_KV_HEADS, HEAD_DIM)], axis=1))
            wv_s.append(jnp.concatenate(
                [cache_win_v[idx][:, 1:], v_new.reshape(n_dec, 1, N_KV_HEADS, HEAD_DIM)], axis=1))
            y = mm(o, attn_w_o, attn_b_o, layer=idx, n_out=d, out_dtype=BF16, name="attn_out")
        if x is None:
            x, h = resnorm_first(x_prompt, x_sample, y, norm_mix_post[layer], norm_ffn_pre[layer],
                                 tr=TR_PROMPT)
        else:
            x, h = resnorm_tiled(x, y, norm_mix_post[layer], norm_ffn_pre[layer],
                                 tr=rows // TILED_ROW_BLOCKS)

        a, c_p, hu_s = ffn_up(h, ffn_w_up, ffn_conv_w, ffn_conv_b, state_conv[layer], layer=layer,
                              seq_len=seq_len, halves=ROW_TILES, tn=TN_FFN_UP, tail_rows=STATE_TAIL_ROWS)
        cv_p.append(c_p)
        cv_s.append(jnp.concatenate([state_conv[layer][:, 1:], hu_s[:, None]], axis=1))
        f = matmul_weight_resident(a, ffn_w_down, layer=layer, tm=rows // FFN_DOWN_ROW_BLOCKS, tn=TN,
                                   name="ffn_down")
        if layer + 1 < depth:
            x, h = resnorm_tiled(x, f, norm_ffn_post[layer], norm_mix_pre[layer + 1],
                                 tr=rows // TILED_ROW_BLOCKS)
        else:
            y_prompt, y_sample = resnorm_last(x, f, norm_ffn_post[layer], seq_len=seq_len, n_dec=n_dec,
                                              tr=TR_PROMPT)

    return (y_prompt, y_sample, jnp.stack(gv_p), jnp.stack(gv_s),
            jnp.stack(wk_p), jnp.stack(wv_p), jnp.stack(wk_s), jnp.stack(wv_s),
            jnp.stack(cv_p), jnp.stack(cv_s))
```

```python
import functools
import math

import jax
import jax.numpy as jnp
from jax import lax
from jax.experimental import pallas as pl
from jax.experimental.pallas import tpu as pltpu

F32 = jnp.float32
BF16 = jnp.bfloat16

NORM_EPS = 1e-6
CHUNK = 128
N_GROUPS = 16
HEAD_DIM = 64
N_KV_HEADS = 8
Q_PER_KV = 8
WINDOW = 128
CONV_W = 3
ATTN_SCALE = HEAD_DIM ** -0.5
MASKED_SCORE = -1e30
LOG2_E = math.log2(math.e)

V7X_LANES = 128
V7X_SUBLANES = 8
V7X_MXU_COLS = 256
V7X_SCOPED_VMEM_BYTES = 60000 * 1024

DOT_ROWS = 512


def _params(*semantics):
    return pltpu.CompilerParams(dimension_semantics=semantics,
                                vmem_limit_bytes=V7X_SCOPED_VMEM_BYTES)


def _rms(x, g):
    return x * lax.rsqrt(jnp.mean(x * x, axis=-1, keepdims=True) + NORM_EPS) * g


def _gelu(x):
    return 0.5 * x * (1.0 + lax.erf(x * math.sqrt(0.5)))


def _identity(x):
    return x


def _dot_bf16(x, w):
    return jnp.dot(x, w.astype(BF16), preferred_element_type=F32)


def _row_chunks(rows, size=DOT_ROWS):
    n = max(rows // size, 1)
    return [(c * size, (c + 1) * size if c + 1 < n else rows) for c in range(n)]


def _skip_aliased(body, n_aliased):
    def kernel_fn(*refs):
        body(*refs[n_aliased:])
    return kernel_fn


def _rmsnorm_kernel(x_ref, g_ref, h_ref):
    h_ref[0] = _rms(x_ref[0], g_ref[...]).astype(h_ref.dtype)


def _resnorm_kernel(x_ref, y_ref, gpost_ref, gnext_ref, xo_ref, ho_ref):
    xn = x_ref[0] + _rms(y_ref[0].astype(F32), gpost_ref[...])
    xo_ref[0] = xn
    ho_ref[0] = _rms(xn, gnext_ref[...]).astype(ho_ref.dtype)


def _resnorm_last_kernel(x_ref, y_ref, gpost_ref, xo_ref):
    xo_ref[0] = x_ref[0] + _rms(y_ref[0].astype(F32), gpost_ref[...])


def rmsnorm_first(x_prompt, x_sample, g, *, tr):
    n_seq, seq_len, d = x_prompt.shape
    n_dec = x_sample.shape[1]
    g = g.reshape(1, d)
    shape = jax.ShapeDtypeStruct((n_seq, seq_len + n_dec, d), BF16)
    row = pl.BlockSpec((1, tr, d), lambda s, r: (s, r, 0))
    h = pl.pallas_call(
        _rmsnorm_kernel, out_shape=shape, grid=(n_seq, seq_len // tr),
        in_specs=[row, pl.BlockSpec((1, d), lambda s, r: (0, 0))], out_specs=row,
        compiler_params=_params("parallel", "parallel"), name="rmsnorm_first_prompt",
    )(x_prompt, g)
    return pl.pallas_call(
        _skip_aliased(_rmsnorm_kernel, 1), out_shape=shape, grid=(n_seq,),
        in_specs=[pl.BlockSpec(memory_space=pl.ANY),
                  pl.BlockSpec((1, n_dec, d), lambda s: (0, 0, 0)),
                  pl.BlockSpec((1, d), lambda s: (0, 0))],
        out_specs=pl.BlockSpec((1, n_dec, d), lambda s: (s, seq_len // n_dec, 0)),
        input_output_aliases={0: 0},
        compiler_params=_params("arbitrary"), name="rmsnorm_first_sample",
    )(h, x_sample, g)


def resnorm_first(x_prompt, x_sample, y, g_post, g_next, *, tr):
    n_seq, seq_len, d = x_prompt.shape
    n_dec = x_sample.shape[1]
    g_post, g_next = g_post.reshape(1, d), g_next.reshape(1, d)
    shapes = (jax.ShapeDtypeStruct(y.shape, F32), jax.ShapeDtypeStruct(y.shape, BF16))
    row = pl.BlockSpec((1, tr, d), lambda s, r: (s, r, 0))
    vec = pl.BlockSpec((1, d), lambda s, r: (0, 0))
    xo, ho = pl.pallas_call(
        _resnorm_kernel, out_shape=shapes, grid=(n_seq, seq_len // tr),
        in_specs=[row, row, vec, vec], out_specs=(row, row),
        compiler_params=_params("parallel", "parallel"), name="resnorm_first_prompt",
    )(x_prompt, y, g_post, g_next)
    sample_rows = pl.BlockSpec((1, n_dec, d), lambda s: (s, seq_len // n_dec, 0))
    vec1 = pl.BlockSpec((1, d), lambda s: (0, 0))
    return pl.pallas_call(
        _skip_aliased(_resnorm_kernel, 2), out_shape=shapes, grid=(n_seq,),
        in_specs=[pl.BlockSpec(memory_space=pl.ANY), pl.BlockSpec(memory_space=pl.ANY),
                  pl.BlockSpec((1, n_dec, d), lambda s: (0, 0, 0)),
                  pl.BlockSpec((1, n_dec, d), lambda s: (0, seq_len // n_dec, 0)), vec1, vec1],
        out_specs=(sample_rows, sample_rows),
        input_output_aliases={0: 0, 1: 1},
        compiler_params=_params("arbitrary"), name="resnorm_first_sample",
    )(xo, ho, x_sample, y, g_post, g_next)


def resnorm_tiled(x, y, g_post, g_next, *, tr):
    n_seq, rows, d = x.shape
    row = pl.BlockSpec((1, tr, d), lambda s, r: (s, r, 0))
    vec = pl.BlockSpec((1, d), lambda s, r: (0, 0))
    return pl.pallas_call(
        _resnorm_kernel,
        out_shape=(jax.ShapeDtypeStruct(x.shape, F32), jax.ShapeDtypeStruct(x.shape, BF16)),
        grid=(n_seq, rows // tr),
        in_specs=[row, row, vec, vec], out_specs=(row, row),
        compiler_params=_params("parallel", "parallel"), name="resnorm_tiled",
    )(x, y, g_post.reshape(1, d), g_next.reshape(1, d))


def resnorm_last(x, y, g_post, *, seq_len, n_dec, tr):
    n_seq, _, d = x.shape
    g_post = g_post.reshape(1, d)
    row = pl.BlockSpec((1, tr, d), lambda s, r: (s, r, 0))
    y_prompt = pl.pallas_call(
        _resnorm_last_kernel, out_shape=jax.ShapeDtypeStruct((n_seq, seq_len, d), F32),
        grid=(n_seq, seq_len // tr),
        in_specs=[row, row, pl.BlockSpec((1, d), lambda s, r: (0, 0))], out_specs=row,
        compiler_params=_params("parallel", "parallel"), name="resnorm_last_prompt",
    )(x, y, g_post)
    sample_rows = pl.BlockSpec((1, n_dec, d), lambda s: (0, seq_len // n_dec, 0))
    y_sample = pl.pallas_call(
        _resnorm_last_kernel, out_shape=jax.ShapeDtypeStruct((1, n_dec, d), F32), grid=(1,),
        in_specs=[sample_rows, sample_rows, pl.BlockSpec((1, d), lambda s: (0, 0))],
        out_specs=pl.BlockSpec((1, n_dec, d), lambda s: (0, 0, 0)),
        compiler_params=_params("arbitrary"), name="resnorm_last_sample",
    )(x, y, g_post)
    return y_prompt, y_sample.reshape(n_dec, 1, d)


def _copy_rows_kernel(src_ref, o_ref):
    o_ref[0] = src_ref[...]


def insert_sample_rows(tiled, rows, *, seq_len):
    n_seq, _, c = tiled.shape
    n_dec = rows.shape[0]
    return pl.pallas_call(
        _skip_aliased(_copy_rows_kernel, 1), out_shape=jax.ShapeDtypeStruct(tiled.shape, tiled.dtype),
        grid=(n_seq,),
        in_specs=[pl.BlockSpec(memory_space=pl.ANY), pl.BlockSpec((n_dec, c), lambda s: (0, 0))],
        out_specs=pl.BlockSpec((1, n_dec, c), lambda s: (s, seq_len // n_dec, 0)),
        input_output_aliases={0: 0},
        compiler_params=_params("arbitrary"), name="insert_sample_rows",
    )(tiled, rows)


def _matmul_kernel(*refs, epilogue, has_bias):
    if has_bias:
        x_ref, w_ref, b_ref, o_ref = refs
    else:
        x_ref, w_ref, o_ref = refs
    for r0, r1 in _row_chunks(x_ref.shape[1]):
        acc = _dot_bf16(x_ref[0, r0:r1], w_ref[...])
        if has_bias:
            acc = acc + b_ref[...]
        o_ref[0, r0:r1] = epilogue(acc).astype(o_ref.dtype)


def matmul(x, w, bias=None, *, layer, row_tiles, tn, n_out, col_block_offset=0, out_dtype=F32,
           epilogue=_identity, name):
    n_seq, rows, k = x.shape
    tm = rows // row_tiles
    in_specs = [pl.BlockSpec((1, tm, k), lambda i, j: (i // row_tiles, i % row_tiles, 0)),
                pl.BlockSpec((None, k, tn), lambda i, j: (layer, 0, j + col_block_offset))]
    args = [x, w]
    if bias is not None:
        in_specs.append(pl.BlockSpec((None, 1, tn), lambda i, j: (layer, 0, j + col_block_offset)))
        args.append(bias.reshape(bias.shape[0], 1, -1))
    return pl.pallas_call(
        functools.partial(_matmul_kernel, epilogue=epilogue, has_bias=bias is not None),
        out_shape=jax.ShapeDtypeStruct((n_seq, rows, n_out), out_dtype),
        grid=(n_seq * row_tiles, n_out // tn),
        in_specs=in_specs,
        out_specs=pl.BlockSpec((1, tm, tn), lambda i, j: (i // row_tiles, i % row_tiles, j)),
        compiler_params=_params("parallel", "arbitrary"),
        name=name,
    )(*args)


def _matmul_wres_kernel(x_ref, w_ref, o_ref, wb_ref):
    @pl.when(pl.program_id(1) == 0)
    def _():
        wb_ref[...] = w_ref[...].astype(BF16)

    o_ref[0] = jnp.dot(x_ref[0], wb_ref[...], preferred_element_type=F32).astype(o_ref.dtype)


def matmul_weight_resident(x, w, *, layer, tm, tn, name):
    n_seq, rows, k = x.shape
    n = w.shape[2]
    per_seq = rows // tm
    return pl.pallas_call(
        _matmul_wres_kernel,
        out_shape=jax.ShapeDtypeStruct((n_seq, rows, n), BF16),
        grid=(n // tn, n_seq * per_seq),
        in_specs=[pl.BlockSpec((1, tm, k), lambda j, i: (i // per_seq, i % per_seq, 0)),
                  pl.BlockSpec((None, k, tn), lambda j, i: (layer, 0, j), pipeline_mode=pl.Buffered(1))],
        out_specs=pl.BlockSpec((1, tm, tn), lambda j, i: (i // per_seq, i % per_seq, j)),
        scratch_shapes=[pltpu.VMEM((k, tn), BF16)],
        compiler_params=_params("arbitrary", "arbitrary"),
        name=name,
    )(x, w)


def _layer_norm(v, g, b):
    xc = v - jnp.mean(v, axis=-1, keepdims=True)
    return xc * lax.rsqrt(jnp.mean(xc * xc, axis=-1, keepdims=True) + NORM_EPS) * g + b


def _spatial_kernel(v_ref, u_ref, lng_ref, lnb_ref, ws_ref, bias_ref, o_ref, gv_ref, *, group_dim):
    vn = _layer_norm(v_ref[0], lng_ref[...], lnb_ref[...])
    gv_ref[0] = vn
    vb = vn.astype(BF16)
    t = lax.broadcasted_iota(jnp.int32, (CHUNK, CHUNK), 0)
    s = lax.broadcasted_iota(jnp.int32, (CHUNK, CHUNK), 1)
    causal = s <= t
    for g in range(N_GROUPS):
        cols = slice(g * group_dim, (g + 1) * group_dim)
        wc = jnp.where(causal, ws_ref[g], 0.0).astype(BF16)
        mix = jnp.dot(wc, vb[:, cols], preferred_element_type=F32) + bias_ref[:, cols]
        o_ref[0, :, cols] = (u_ref[0, :, cols].astype(F32) * mix).astype(o_ref.dtype)


def _spatial_sample_kernel(v_ref, u_ref, lng_ref, lnb_ref, w00_ref, b0_ref, o_ref, gv_ref):
    vn = _layer_norm(v_ref[0], lng_ref[...], lnb_ref[...])
    gv_ref[0] = vn
    mix = w00_ref[...].astype(BF16).astype(F32) * vn.astype(BF16).astype(F32) + b0_ref[...]
    o_ref[0] = (u_ref[0].astype(F32) * mix).astype(o_ref.dtype)


def spatial_gate(v, u, ln_g, ln_b, w_s, b_s, *, seq_len):
    n_seq, rows, d = v.shape
    n_dec = rows - seq_len
    group_dim = d // N_GROUPS
    ln_g, ln_b = ln_g.reshape(1, d), ln_b.reshape(1, d)
    bias_full = jnp.repeat(b_s.T, group_dim, axis=1)
    row = pl.BlockSpec((1, CHUNK, d), lambda s, c: (s, c, 0))
    vec = pl.BlockSpec((1, d), lambda s, c: (0, 0))
    gated, gv_prompt = pl.pallas_call(
        functools.partial(_spatial_kernel, group_dim=group_dim),
        out_shape=(jax.ShapeDtypeStruct(v.shape, BF16),
                   jax.ShapeDtypeStruct((n_seq, CHUNK, d), F32)),
        grid=(n_seq, seq_len // CHUNK),
        in_specs=[row, row, vec, vec,
                  pl.BlockSpec((N_GROUPS, CHUNK, CHUNK), lambda s, c: (0, 0, 0)),
                  pl.BlockSpec((CHUNK, d), lambda s, c: (0, 0))],
        out_specs=(row, pl.BlockSpec((1, CHUNK, d), lambda s, c: (s, 0, 0))),
        compiler_params=_params("parallel", "arbitrary"),
        name="spatial_gate_prompt",
    )(v, u, ln_g, ln_b, w_s, bias_full)
    w00 = jnp.repeat(w_s[:, 0, 0], group_dim).reshape(1, d)
    b0 = jnp.repeat(b_s[:, 0], group_dim).reshape(1, d)
    tile0_rows = pl.BlockSpec((1, n_dec, d), lambda s: (0, seq_len // n_dec, 0))
    vec1 = pl.BlockSpec((1, d), lambda s: (0, 0))
    gated, gv_sample = pl.pallas_call(
        _skip_aliased(_spatial_sample_kernel, 1),
        out_shape=(jax.ShapeDtypeStruct(v.shape, BF16), jax.ShapeDtypeStruct((1, n_dec, d), F32)),
        grid=(n_seq,),
        in_specs=[pl.BlockSpec(memory_space=pl.ANY), tile0_rows, tile0_rows, vec1, vec1, vec1, vec1],
        out_specs=(pl.BlockSpec((1, n_dec, d), lambda s: (s, seq_len // n_dec, 0)),
                   pl.BlockSpec((1, n_dec, d), lambda s: (0, 0, 0))),
        input_output_aliases={0: 0},
        compiler_params=_params("arbitrary"),
        name="spatial_gate_sample",
    )(gated, v, u, ln_g, ln_b, w00, b0)
    return gated, gv_prompt, gv_sample


def _pair_block_diag(pair, even):
    lane = lax.broadcasted_iota(jnp.int32, pair.shape, 1)
    if even:
        own = jnp.where(lane < HEAD_DIM, pair, 0.0)
        return jnp.concatenate([own, pltpu.roll(own, HEAD_DIM, 1)], axis=0)
    own = jnp.where(lane >= HEAD_DIM, pair, 0.0)
    return jnp.concatenate([pltpu.roll(own, HEAD_DIM, 1), own], axis=0)


def _attn_prompt_kernel(sink_ref, q_ref, kprev_ref, kown_ref, vprev_ref, vown_ref, o_ref):
    blk = pl.program_id(1)
    n_keys = 2 * WINDOW
    qi = lax.broadcasted_iota(jnp.int32, (WINDOW, n_keys), 0)
    kj = lax.broadcasted_iota(jnp.int32, (WINDOW, n_keys), 1)
    first_key = jnp.where(blk > 0, 0, WINDOW)
    valid = (kj >= qi) & (kj <= qi + WINDOW) & (kj >= first_key)
    col_blocks = Q_PER_KV * HEAD_DIM // V7X_LANES
    valid = jnp.concatenate([valid] * col_blocks, axis=0)
    block_of_row = lax.broadcasted_iota(jnp.int32, (col_blocks * WINDOW, 1), 0) // WINDOW
    head_lane = lax.broadcasted_iota(jnp.int32, (col_blocks * WINDOW, V7X_LANES), 1)
    group_w = Q_PER_KV * HEAD_DIM
    for pair in range(N_KV_HEADS // 2):
        lanes = slice(pair * V7X_LANES, (pair + 1) * V7X_LANES)
        k_pair = jnp.concatenate([kprev_ref[0, :, lanes], kown_ref[0, :, lanes]], axis=0)
        v_pair = jnp.concatenate([vprev_ref[0, :, lanes], vown_ref[0, :, lanes]], axis=0)
        for e in range(2):
            h = 2 * pair + e
            k2 = _pair_block_diag(k_pair, e == 0).astype(BF16)
            v2 = _pair_block_diag(v_pair, e == 0).astype(BF16)
            qs = jnp.concatenate(
                [q_ref[0, :, h * group_w + c * V7X_LANES: h * group_w + (c + 1) * V7X_LANES]
                 for c in range(col_blocks)], axis=0)
            qs = (qs.astype(F32) * (ATTN_SCALE * LOG2_E)).astype(BF16)
            s = lax.dot_general(qs, k2, (((1,), (1,)), ((), ())), preferred_element_type=F32)
            probs, inv_denoms = [], []
            for half in range(2):
                sh = jnp.where(valid, s[:, half * n_keys:(half + 1) * n_keys], MASKED_SCORE)
                sink = jnp.zeros((col_blocks * WINDOW, 1), F32)
                for c in range(col_blocks):
                    sink = jnp.where(block_of_row == c, sink_ref[h * Q_PER_KV + 2 * c + half] * LOG2_E,
                                     sink)
                m = jnp.maximum(jnp.max(sh, axis=-1, keepdims=True), sink)
                p = jnp.exp2(sh - m)
                inv_denoms.append(1.0 / (jnp.sum(p, axis=-1, keepdims=True) + jnp.exp2(sink - m)))
                probs.append(p.astype(BF16))
            o = jnp.dot(jnp.concatenate(probs, axis=1), v2, preferred_element_type=F32)
            o = o * jnp.where(head_lane < HEAD_DIM, inv_denoms[0], inv_denoms[1])
            for c in range(col_blocks):
                o_ref[0, :, h * group_w + c * V7X_LANES: h * group_w + (c + 1) * V7X_LANES] = (
                    o[c * WINDOW:(c + 1) * WINDOW].astype(o_ref.dtype))


def attention_prompt(q, kv, sinks, *, seq_len):
    n_seq, rows, d_q = q.shape
    d_kv = N_KV_HEADS * HEAD_DIM
    k_col, v_col = 0, 1

    def own(col):
        return lambda s, i: (s, i, col)

    def prev(col):
        return lambda s, i: (s, jnp.maximum(i - 1, 0), col)

    return pl.pallas_call(
        _attn_prompt_kernel,
        out_shape=jax.ShapeDtypeStruct((n_seq, rows, d_q), BF16),
        grid=(n_seq, seq_len // WINDOW),
        in_specs=[pl.BlockSpec(memory_space=pltpu.SMEM),
                  pl.BlockSpec((1, WINDOW, d_q), own(0)),
                  pl.BlockSpec((1, WINDOW, d_kv), prev(k_col)),
                  pl.BlockSpec((1, WINDOW, d_kv), own(k_col)),
                  pl.BlockSpec((1, WINDOW, d_kv), prev(v_col)),
                  pl.BlockSpec((1, WINDOW, d_kv), own(v_col))],
        out_specs=pl.BlockSpec((1, WINDOW, d_q), own(0)),
        compiler_params=_params("parallel", "arbitrary"),
        name="attention_prompt",
    )(sinks, q, kv, kv, kv, kv)


def _attn_sample_kernel(q2_ref, ck_ref, cv_ref, knew_ref, vnew_ref, sink_ref, o_ref):
    for pair in range(N_KV_HEADS // 2):
        lanes = slice(pair * V7X_LANES, (pair + 1) * V7X_LANES)
        q2 = q2_ref[0, pair]
        k_pair = ck_ref[0, :, lanes].astype(BF16)
        v_pair = cv_ref[0, :, lanes].astype(BF16)
        k_new = knew_ref[0, :, lanes].astype(BF16).astype(F32)
        v_new = vnew_ref[0, :, lanes].astype(BF16).astype(F32)
        sink = sink_ref[pair][:, :1]
        s = lax.dot_general(q2, k_pair, (((1,), (1,)), ((), ())), preferred_element_type=F32)
        s_new = jnp.sum(q2.astype(F32) * k_new, axis=-1, keepdims=True)
        m = jnp.maximum(jnp.maximum(jnp.max(s, axis=-1, keepdims=True), s_new), sink)
        p = jnp.exp(s - m)
        p_new = jnp.exp(s_new - m)
        denom = jnp.sum(p, axis=-1, keepdims=True) + p_new + jnp.exp(sink - m)
        o = jnp.dot((p / denom).astype(BF16), v_pair, preferred_element_type=F32)
        o_ref[0, pair] = o + (p_new / denom).astype(BF16).astype(F32) * v_new


def attention_sample(q, kv, cache_k, cache_v, sinks):
    n, d_q = q.shape
    d_kv = N_KV_HEADS * HEAD_DIM
    n_pairs = N_KV_HEADS // 2
    rows = 2 * Q_PER_KV
    q = (q.astype(F32) * ATTN_SCALE).astype(BF16).reshape(n, n_pairs, 2, Q_PER_KV, HEAD_DIM)
    zeros = jnp.zeros((n, n_pairs, Q_PER_KV, HEAD_DIM), BF16)
    q2 = jnp.concatenate([jnp.concatenate([q[:, :, 0], zeros], axis=-1),
                          jnp.concatenate([zeros, q[:, :, 1]], axis=-1)], axis=2)
    k_new = kv[:, :d_kv].reshape(n, 1, d_kv)
    v_new = kv[:, d_kv:].reshape(n, 1, d_kv)
    sink2 = jnp.broadcast_to(sinks.reshape(n_pairs, rows, 1), (n_pairs, rows, V7X_LANES))
    o2 = pl.pallas_call(
        _attn_sample_kernel,
        out_shape=jax.ShapeDtypeStruct((n, n_pairs, rows, V7X_LANES), F32),
        grid=(n,),
        in_specs=[pl.BlockSpec((1, n_pairs, rows, V7X_LANES), lambda b: (b, 0, 0, 0)),
                  pl.BlockSpec((1, WINDOW, d_kv), lambda b: (b, 0, 0)),
                  pl.BlockSpec((1, WINDOW, d_kv), lambda b: (b, 0, 0)),
                  pl.BlockSpec((1, 1, d_kv), lambda b: (b, 0, 0)),
                  pl.BlockSpec((1, 1, d_kv), lambda b: (b, 0, 0)),
                  pl.BlockSpec((n_pairs, rows, V7X_LANES), lambda b: (0, 0, 0))],
        out_specs=pl.BlockSpec((1, n_pairs, rows, V7X_LANES), lambda b: (b, 0, 0, 0)),
        compiler_params=_params("parallel"),
        name="attention_sample",
    )(q2, cache_k.reshape(n, WINDOW, d_kv), cache_v.reshape(n, WINDOW, d_kv), k_new, v_new, sink2)
    o = jnp.stack([o2[:, :, :Q_PER_KV, :HEAD_DIM], o2[:, :, Q_PER_KV:, HEAD_DIM:]], axis=2)
    return o.reshape(n, d_q).astype(BF16), k_new, v_new


def _silu_gate(gate, val):
    return gate * (1.0 / (1.0 + jnp.exp(-gate))) * val


def _ffn_up_kernel(x_ref, wg_ref, wv_ref, cwg_ref, cwv_ref, cbg_ref, cbv_ref,
                   p0g_ref, p0v_ref, p1g_ref, p1v_ref,
                   a_ref, sg_ref, sv_ref, hsg_ref, hsv_ref, h_ref, tail_ref, *, halves, n_dec):
    half = pl.program_id(0) % halves
    j = pl.program_id(1)
    tn = wg_ref.shape[1]
    tail_rows = tail_ref.shape[1]
    tm = x_ref.shape[1]
    gate_cols, val_cols = slice(0, tn), slice(tn, 2 * tn)

    @pl.when((pl.program_id(0) == 0) & (j == 0))
    def _():
        tail_ref[...] = jnp.zeros(tail_ref.shape, tail_ref.dtype)

    first_tile = jnp.full((tail_rows, 2 * tn), half, jnp.int32) == 0
    h_ref[0:tail_rows] = jnp.where(first_tile, 0.0, tail_ref[j])

    def conv(r0, r1, cols, cw_ref, cb_ref):
        ext = h_ref[r0:r1 + tail_rows, cols]
        h1 = pltpu.roll(ext, 1, 0)[tail_rows:]
        h2 = pltpu.roll(ext, 2, 0)[tail_rows:]
        return cb_ref[...] + cw_ref[0:1] * h2 + cw_ref[1:2] * h1 + cw_ref[2:3] * ext[tail_rows:]

    def epilogue(r0, r1):
        a_ref[0, r0:r1] = _silu_gate(conv(r0, r1, gate_cols, cwg_ref, cbg_ref),
                                     conv(r0, r1, val_cols, cwv_ref, cbv_ref)).astype(a_ref.dtype)

    pending = None
    for r0, r1 in _row_chunks(tm):
        x = x_ref[0, r0:r1]
        h_ref[tail_rows + r0:tail_rows + r1, gate_cols] = _dot_bf16(x, wg_ref[...])
        h_ref[tail_rows + r0:tail_rows + r1, val_cols] = _dot_bf16(x, wv_ref[...])
        if pending is not None:
            epilogue(*pending)
        pending = (r0, r1)
    epilogue(*pending)
    tail_ref[j] = h_ref[tm:tm + tail_rows]

    h_sample = h_ref[tail_rows + tm - n_dec:tail_rows + tm]
    hsg_ref[0, 0] = h_sample[:, gate_cols]
    hsv_ref[0, 0] = h_sample[:, val_cols]
    seq_tail = h_ref[tm - n_dec:tm - n_dec + tail_rows]
    sg_ref[0, 0] = seq_tail[:, gate_cols]
    sv_ref[0, 0] = seq_tail[:, val_cols]

    @pl.when(half == halves - 1)
    def _():
        def conv_sample(cols, p0_ref, p1_ref, cw_ref, cb_ref):
            return (cb_ref[...] + cw_ref[0:1] * p0_ref[...] + cw_ref[1:2] * p1_ref[...]
                    + cw_ref[2:3] * h_sample[:, cols])

        a_ref[0, tm - n_dec:tm] = _silu_gate(
            conv_sample(gate_cols, p0g_ref, p1g_ref, cwg_ref, cbg_ref),
            conv_sample(val_cols, p0v_ref, p1v_ref, cwv_ref, cbv_ref)).astype(a_ref.dtype)


def ffn_up(h, w_up, conv_w, conv_b, state, *, layer, seq_len, halves, tn, tail_rows):
    n_seq, rows, k = h.shape
    n_dec = rows - seq_len
    d_ff = w_up.shape[2] // 2
    nj = d_ff // tn
    tm = rows // halves
    conv_b = conv_b.reshape(conv_b.shape[0], 1, -1)
    p0, p1 = state[:, 0], state[:, 1]

    def gate(shape):
        return pl.BlockSpec(shape, lambda i, j: (0, j))

    def val(shape):
        return pl.BlockSpec(shape, lambda i, j: (0, j + nj))

    def gate_l(shape):
        return pl.BlockSpec((None,) + shape, lambda i, j: (layer, 0, j))

    def val_l(shape):
        return pl.BlockSpec((None,) + shape, lambda i, j: (layer, 0, j + nj))

    per_tile = lambda r: pl.BlockSpec((1, 1, r, tn), lambda i, j: (i // halves, i % halves, 0, j))
    a, sg, sv, hsg, hsv = pl.pallas_call(
        functools.partial(_ffn_up_kernel, halves=halves, n_dec=n_dec),
        out_shape=(jax.ShapeDtypeStruct((n_seq, rows, d_ff), BF16),
                   jax.ShapeDtypeStruct((n_seq, halves, tail_rows, d_ff), F32),
                   jax.ShapeDtypeStruct((n_seq, halves, tail_rows, d_ff), F32),
                   jax.ShapeDtypeStruct((n_seq, halves, n_dec, d_ff), F32),
                   jax.ShapeDtypeStruct((n_seq, halves, n_dec, d_ff), F32)),
        grid=(n_seq * halves, nj),
        in_specs=[pl.BlockSpec((1, tm, k), lambda i, j: (i // halves, i % halves, 0)),
                  gate_l((k, tn)), val_l((k, tn)), gate_l((CONV_W, tn)), val_l((CONV_W, tn)),
                  gate_l((1, tn)), val_l((1, tn)),
                  gate((n_dec, tn)), val((n_dec, tn)), gate((n_dec, tn)), val((n_dec, tn))],
        out_specs=(pl.BlockSpec((1, tm, tn), lambda i, j: (i // halves, i % halves, j)),
                   per_tile(tail_rows), per_tile(tail_rows), per_tile(n_dec), per_tile(n_dec)),
        scratch_shapes=[pltpu.VMEM((tail_rows + tm, 2 * tn), F32),
                        pltpu.VMEM((nj, tail_rows, 2 * tn), F32)],
        compiler_params=_params("arbitrary", "arbitrary"),
        name="ffn_up",
    )(h, w_up, w_up, conv_w, conv_w, conv_b, conv_b, p0, p0, p1, p1)
    keep = slice(tail_rows - (CONV_W - 1), tail_rows)
    state_prompt = jnp.concatenate([sg[:, halves - 1, keep], sv[:, halves - 1, keep]], axis=-1)
    h_sample = jnp.concatenate([hsg[0, halves - 1], hsv[0, halves - 1]], axis=-1)
    return a, state_prompt, h_sample


ROW_TILES = 2
TN = 512
TN_FFN_UP = V7X_MXU_COLS
TR_PROMPT = 256
TILED_ROW_BLOCKS = 10
FFN_DOWN_ROW_BLOCKS = 5
STATE_TAIL_ROWS = V7X_SUBLANES


def kernel(x_prompt, x_sample, cache_win_k, cache_win_v, state_conv, norm_mix_pre, norm_mix_post,
           norm_ffn_pre, norm_ffn_post, gmlp_w_in, gmlp_ln_g, gmlp_ln_b, gmlp_w_s, gmlp_b_s,
           gmlp_w_out, attn_w_qkv, attn_b_qkv, attn_sinks, attn_w_o, attn_b_o, ffn_w_up,
           ffn_conv_w, ffn_conv_b, ffn_w_down):
    n_seq, seq_len, d = x_prompt.shape
    n_dec = x_sample.shape[0]
    depth = norm_mix_pre.shape[0]
    rows = seq_len + n_dec
    x_sample = x_sample.reshape(1, n_dec, d)
    mm = functools.partial(matmul, row_tiles=ROW_TILES, tn=TN)

    h = rmsnorm_first(x_prompt, x_sample, norm_mix_pre[0], tr=TR_PROMPT)
    x = None
    gv_p, gv_s, wk_p, wv_p, wk_s, wv_s, cv_p, cv_s = [], [], [], [], [], [], [], []
    for layer in range(depth):
        idx = layer // 2
        if layer % 2 == 0:
            d_g = gmlp_w_out.shape[1]
            u = mm(h, gmlp_w_in, layer=idx, n_out=d_g, out_dtype=BF16, epilogue=_gelu, name="gmlp_in_u")
            v = mm(h, gmlp_w_in, layer=idx, n_out=d_g, col_block_offset=d_g // TN, epilogue=_gelu,
                   name="gmlp_in_v")
            gated, g_p, g_s = spatial_gate(v, u, gmlp_ln_g[idx], gmlp_ln_b[idx], gmlp_w_s[idx],
                                           gmlp_b_s[idx], seq_len=seq_len)
            gv_p.append(g_p)
            gv_s.append(g_s.reshape(n_dec, 1, d_g))
            y = mm(gated, gmlp_w_out, layer=idx, n_out=d, out_dtype=BF16, name="gmlp_out")
        else:
            d_q = attn_w_o.shape[1]
            d_qkv = attn_w_qkv.shape[2]
            d_kv = (d_qkv - d_q) // 2
            q = mm(h, attn_w_qkv, attn_b_qkv, layer=idx, n_out=d_q, out_dtype=BF16, name="attn_q")
            kv = mm(h, attn_w_qkv, attn_b_qkv, layer=idx, n_out=2 * d_kv, col_block_offset=d_q // TN,
                    name="attn_kv")
            o = attention_prompt(q, kv, attn_sinks[idx], seq_len=seq_len)
            o_s, k_new, v_new = attention_sample(q[0, seq_len:], kv[0, seq_len:], cache_win_k[idx],
                                                 cache_win_v[idx], attn_sinks[idx])
            o = insert_sample_rows(o, o_s, seq_len=seq_len)
            tail = kv[:, seq_len - WINDOW:seq_len]
            wk_p.append(tail[:, :, :d_kv].reshape(n_seq, WINDOW, N_KV_HEADS, HEAD_DIM))
            wv_p.append(tail[:, :, d_kv:].reshape(n_seq, WINDOW, N_KV_HEADS, HEAD_DIM))
            wk_s.append(jnp.concatenate(
                [cache_win_k[idx][:, 1:], k_new.reshape(n_dec, 1, N_KV_HEADS, HEAD_DIM)], axis=1))
            wv_s.append(jnp.concatenate(
                [cache_win_v[idx][:, 1:], v_new.reshape(n_dec, 1, N_KV_HEADS, HEAD_DIM)], axis=1))
            y = mm(o, attn_w_o, attn_b_o, layer=idx, n_out=d, out_dtype=BF16, name="attn_out")
        if x is None:
            x, h = resnorm_first(x_prompt, x_sample, y, norm_mix_post[layer], norm_ffn_pre[layer],
                                 tr=TR_PROMPT)
        else:
            x, h = resnorm_tiled(x, y, norm_mix_post[layer], norm_ffn_pre[layer],
                                 tr=rows // TILED_ROW_BLOCKS)

        a, c_p, hu_s = ffn_up(h, ffn_w_up, ffn_conv_w, ffn_conv_b, state_conv[layer], layer=layer,
                              seq_len=seq_len, halves=ROW_TILES, tn=TN_FFN_UP, tail_rows=STATE_TAIL_ROWS)
        cv_p.append(c_p)
        cv_s.append(jnp.concatenate([state_conv[layer][:, 1:], hu_s[:, None]], axis=1))
        f = matmul_weight_resident(a, ffn_w_down, layer=layer, tm=rows // FFN_DOWN_ROW_BLOCKS, tn=TN,
                                   name="ffn_down")
        if layer + 1 < depth:
            x, h = resnorm_tiled(x, f, norm_ffn_post[layer], norm_mix_pre[layer + 1],
                                 tr=rows // TILED_ROW_BLOCKS)
        else:
            y_prompt, y_sample = resnorm_last(x, f, norm_ffn_post[layer], seq_len=seq_len, n_dec=n_dec,
                                              tr=TR_PROMPT)

    return (y_prompt, y_sample, jnp.stack(gv_p), jnp.stack(gv_s),
            jnp.stack(wk_p), jnp.stack(wv_p), jnp.stack(wk_s), jnp.stack(wv_s),
            jnp.stack(cv_p), jnp.stack(cv_s))
```

```python
import functools
import math

import jax
import jax.numpy as jnp
from jax import lax
from jax.experimental import pallas as pl
from jax.experimental.pallas import tpu as pltpu

F32 = jnp.float32
BF16 = jnp.bfloat16

NORM_EPS = 1e-6
CHUNK = 128
N_GROUPS = 16
HEAD_DIM = 64
N_KV_HEADS = 8
Q_PER_KV = 8
WINDOW = 128
CONV_W = 3
ATTN_SCALE = HEAD_DIM ** -0.5
MASKED_SCORE = -1e30
LOG2_E = math.log2(math.e)

V7X_LANES = 128
V7X_SUBLANES = 8
V7X_MXU_COLS = 256
V7X_SCOPED_VMEM_BYTES = 60000 * 1024

DOT_ROWS = 512


def _params(*semantics):
    return pltpu.CompilerParams(dimension_semantics=semantics,
                                vmem_limit_bytes=V7X_SCOPED_VMEM_BYTES)


def _rms(x, g):
    return x * lax.rsqrt(jnp.mean(x * x, axis=-1, keepdims=True) + NORM_EPS) * g


def _gelu(x):
    return 0.5 * x * (1.0 + lax.erf(x * math.sqrt(0.5)))


def _identity(x):
    return x


def _dot_bf16(x, w):
    return jnp.dot(x, w.astype(BF16), preferred_element_type=F32)


def _row_chunks(rows, size=DOT_ROWS):
    n = max(rows // size, 1)
    return [(c * size, (c + 1) * size if c + 1 < n else rows) for c in range(n)]


def _skip_aliased(body, n_aliased):
    def kernel_fn(*refs):
        body(*refs[n_aliased:])
    return kernel_fn


def _rmsnorm_kernel(x_ref, g_ref, h_ref):
    h_ref[0] = _rms(x_ref[0], g_ref[...]).astype(h_ref.dtype)


def _resnorm_kernel(x_ref, y_ref, gpost_ref, gnext_ref, xo_ref, ho_ref):
    xn = x_ref[0] + _rms(y_ref[0].astype(F32), gpost_ref[...])
    xo_ref[0] = xn
    ho_ref[0] = _rms(xn, gnext_ref[...]).astype(ho_ref.dtype)


def _resnorm_last_kernel(x_ref, y_ref, gpost_ref, xo_ref):
    xo_ref[0] = x_ref[0] + _rms(y_ref[0].astype(F32), gpost_ref[...])


def rmsnorm_first(x_prompt, x_sample, g, *, tr):
    n_seq, seq_len, d = x_prompt.shape
    n_dec = x_sample.shape[1]
    g = g.reshape(1, d)
    shape = jax.ShapeDtypeStruct((n_seq, seq_len + n_dec, d), BF16)
    row = pl.BlockSpec((1, tr, d), lambda s, r: (s, r, 0))
    h = pl.pallas_call(
        _rmsnorm_kernel, out_shape=shape, grid=(n_seq, seq_len // tr),
        in_specs=[row, pl.BlockSpec((1, d), lambda s, r: (0, 0))], out_specs=row,
        compiler_params=_params("parallel", "parallel"), name="rmsnorm_first_prompt",
    )(x_prompt, g)
    return pl.pallas_call(
        _skip_aliased(_rmsnorm_kernel, 1), out_shape=shape, grid=(n_seq,),
        in_specs=[pl.BlockSpec(memory_space=pl.ANY),
                  pl.BlockSpec((1, n_dec, d), lambda s: (0, 0, 0)),
                  pl.BlockSpec((1, d), lambda s: (0, 0))],
        out_specs=pl.BlockSpec((1, n_dec, d), lambda s: (s, seq_len // n_dec, 0)),
        input_output_aliases={0: 0},
        compiler_params=_params("arbitrary"), name="rmsnorm_first_sample",
    )(h, x_sample, g)


def resnorm_first(x_prompt, x_sample, y, g_post, g_next, *, tr):
    n_seq, seq_len, d = x_prompt.shape
    n_dec = x_sample.shape[1]
    g_post, g_next = g_post.reshape(1, d), g_next.reshape(1, d)
    shapes = (jax.ShapeDtypeStruct(y.shape, F32), jax.ShapeDtypeStruct(y.shape, BF16))
    row = pl.BlockSpec((1, tr, d), lambda s, r: (s, r, 0))
    vec = pl.BlockSpec((1, d), lambda s, r: (0, 0))
    xo, ho = pl.pallas_call(
        _resnorm_kernel, out_shape=shapes, grid=(n_seq, seq_len // tr),
        in_specs=[row, row, vec, vec], out_specs=(row, row),
        compiler_params=_params("parallel", "parallel"), name="resnorm_first_prompt",
    )(x_prompt, y, g_post, g_next)
    sample_rows = pl.BlockSpec((1, n_dec, d), lambda s: (s, seq_len // n_dec, 0))
    vec1 = pl.BlockSpec((1, d), lambda s: (0, 0))
    return pl.pallas_call(
        _skip_aliased(_resnorm_kernel, 2), out_shape=shapes, grid=(n_seq,),
        in_specs=[pl.BlockSpec(memory_space=pl.ANY), pl.BlockSpec(memory_space=pl.ANY),
                  pl.BlockSpec((1, n_dec, d), lambda s: (0, 0, 0)),
                  pl.BlockSpec((1, n_dec, d), lambda s: (0, seq_len // n_dec, 0)), vec1, vec1],
        out_specs=(sample_rows, sample_rows),
        input_output_aliases={0: 0, 1: 1},
        compiler_params=_params("arbitrary"), name="resnorm_first_sample",
    )(xo, ho, x_sample, y, g_post, g_next)


def resnorm_tiled(x, y, g_post, g_next, *, tr):
    n_seq, rows, d = x.shape
    row = pl.BlockSpec((1, tr, d), lambda s, r: (s, r, 0))
    vec = pl.BlockSpec((1, d), lambda s, r: (0, 0))
    return pl.pallas_call(
        _resnorm_kernel,
        out_shape=(jax.ShapeDtypeStruct(x.shape, F32), jax.ShapeDtypeStruct(x.shape, BF16)),
        grid=(n_seq, rows // tr),
        in_specs=[row, row, vec, vec], out_specs=(row, row),
        compiler_params=_params("parallel", "parallel"), name="resnorm_tiled",
    )(x, y, g_post.reshape(1, d), g_next.reshape(1, d))


def resnorm_last(x, y, g_post, *, seq_len, n_dec, tr):
    n_seq, _, d = x.shape
    g_post = g_post.reshape(1, d)
    row = pl.BlockSpec((1, tr, d), lambda s, r: (s, r, 0))
    y_prompt = pl.pallas_call(
        _resnorm_last_kernel, out_shape=jax.ShapeDtypeStruct((n_seq, seq_len, d), F32),
        grid=(n_seq, seq_len // tr),
        in_specs=[row, row, pl.BlockSpec((1, d), lambda s, r: (0, 0))], out_specs=row,
        compiler_params=_params("parallel", "parallel"), name="resnorm_last_prompt",
    )(x, y, g_post)
    sample_rows = pl.BlockSpec((1, n_dec, d), lambda s: (0, seq_len // n_dec, 0))
    y_sample = pl.pallas_call(
        _resnorm_last_kernel, out_shape=jax.ShapeDtypeStruct((1, n_dec, d), F32), grid=(1,),
        in_specs=[sample_rows, sample_rows, pl.BlockSpec((1, d), lambda s: (0, 0))],
        out_specs=pl.BlockSpec((1, n_dec, d), lambda s: (0, 0, 0)),
        compiler_params=_params("arbitrary"), name="resnorm_last_sample",
    )(x, y, g_post)
    return y_prompt, y_sample.reshape(n_dec, 1, d)


def _copy_rows_kernel(src_ref, o_ref):
    o_ref[0] = src_ref[...]


def insert_sample_rows(tiled, rows, *, seq_len):
    n_seq, _, c = tiled.shape
    n_dec = rows.shape[0]
    return pl.pallas_call(
        _skip_aliased(_copy_rows_kernel, 1), out_shape=jax.ShapeDtypeStruct(tiled.shape, tiled.dtype),
        grid=(n_seq,),
        in_specs=[pl.BlockSpec(memory_space=pl.ANY), pl.BlockSpec((n_dec, c), lambda s: (0, 0))],
        out_specs=pl.BlockSpec((1, n_dec, c), lambda s: (s, seq_len // n_dec, 0)),
        input_output_aliases={0: 0},
        compiler_params=_params("arbitrary"), name="insert_sample_rows",
    )(tiled, rows)


def _matmul_kernel(*refs, epilogue, has_bias):
    if has_bias:
        x_ref, w_ref, b_ref, o_ref = refs
    else:
        x_ref, w_ref, o_ref = refs
    for r0, r1 in _row_chunks(x_ref.shape[1]):
        acc = _dot_bf16(x_ref[0, r0:r1], w_ref[...])
        if has_bias:
            acc = acc + b_ref[...]
        o_ref[0, r0:r1] = epilogue(acc).astype(o_ref.dtype)


def matmul(x, w, bias=None, *, layer, row_tiles, tn, n_out, col_block_offset=0, out_dtype=F32,
           epilogue=_identity, name):
    n_seq, rows, k = x.shape
    tm = rows // row_tiles
    in_specs = [pl.BlockSpec((1, tm, k), lambda i, j: (i // row_tiles, i % row_tiles, 0)),
                pl.BlockSpec((None, k, tn), lambda i, j: (layer, 0, j + col_block_offset))]
    args = [x, w]
    if bias is not None:
        in_specs.append(pl.BlockSpec((None, 1, tn), lambda i, j: (layer, 0, j + col_block_offset)))
        args.append(bias.reshape(bias.shape[0], 1, -1))
    return pl.pallas_call(
        functools.partial(_matmul_kernel, epilogue=epilogue, has_bias=bias is not None),
        out_shape=jax.ShapeDtypeStruct((n_seq, rows, n_out), out_dtype),
        grid=(n_seq * row_tiles, n_out // tn),
        in_specs=in_specs,
        out_specs=pl.BlockSpec((1, tm, tn), lambda i, j: (i // row_tiles, i % row_tiles, j)),
        compiler_params=_params("parallel", "arbitrary"),
        name=name,
    )(*args)


def _matmul_wres_kernel(x_ref, w_hbm, o_ref, wb_ref, stage_ref, sem, *, layer, parts, steps_per_part):
    j = pl.program_id(0)
    i = pl.program_id(1)
    n_col_tiles = pl.num_programs(0)
    _, k, tn = wb_ref.shape
    kp = k // parts

    def part_copy(col_tile, part):
        rows = pl.ds(pl.multiple_of(part * kp, V7X_SUBLANES), kp)
        cols = pl.ds(pl.multiple_of(col_tile * tn, V7X_LANES), tn)
        return pltpu.make_async_copy(w_hbm.at[layer, rows, cols], stage_ref, sem.at[0])

    def round_part(col_tile, part):
        rows = pl.ds(pl.multiple_of(part * kp, 2 * V7X_SUBLANES), kp)
        wb_ref[col_tile % 2, rows, :] = stage_ref[...].astype(BF16)

    @pl.when((j == 0) & (i == 0))
    def _():
        for part in range(parts):
            part_copy(0, part).start()
            part_copy(0, part).wait()
            round_part(0, part)

    part = i // steps_per_part
    phase = i % steps_per_part
    prefetching = (j + 1 < n_col_tiles) & (part < parts)

    @pl.when(prefetching & (phase == 0))
    def _():
        part_copy(j + 1, part).start()

    @pl.when(prefetching & (phase == steps_per_part - 2))
    def _():
        part_copy(j + 1, part).wait()
        round_part(j + 1, part)

    o_ref[0] = jnp.dot(x_ref[0], wb_ref[j % 2], preferred_element_type=F32).astype(o_ref.dtype)


def matmul_weight_resident(x, w, *, layer, tm, tn, parts, name):
    n_seq, rows, k = x.shape
    n = w.shape[2]
    per_seq = rows // tm
    n_row_tiles = n_seq * per_seq
    steps_per_part = n_row_tiles // parts
    assert steps_per_part >= 2 and k % (parts * 2 * V7X_SUBLANES) == 0
    return pl.pallas_call(
        functools.partial(_matmul_wres_kernel, layer=layer, parts=parts, steps_per_part=steps_per_part),
        out_shape=jax.ShapeDtypeStruct((n_seq, rows, n), BF16),
        grid=(n // tn, n_row_tiles),
        in_specs=[pl.BlockSpec((1, tm, k), lambda j, i: (i // per_seq, i % per_seq, 0)),
                  pl.BlockSpec(memory_space=pl.ANY)],
        out_specs=pl.BlockSpec((1, tm, tn), lambda j, i: (i // per_seq, i % per_seq, j)),
        scratch_shapes=[pltpu.VMEM((2, k, tn), BF16), pltpu.VMEM((k // parts, tn), F32),
                        pltpu.SemaphoreType.DMA((1,))],
        compiler_params=_params("arbitrary", "arbitrary"),
        name=name,
    )(x, w)


def _layer_norm(v, g, b):
    xc = v - jnp.mean(v, axis=-1, keepdims=True)
    return xc * lax.rsqrt(jnp.mean(xc * xc, axis=-1, keepdims=True) + NORM_EPS) * g + b


def _spatial_kernel(v_ref, u_ref, lng_ref, lnb_ref, ws_ref, bias_ref, o_ref, gv_ref, *, group_dim):
    vn = _layer_norm(v_ref[0], lng_ref[...], lnb_ref[...])
    gv_ref[0] = vn
    vb = vn.astype(BF16)
    t = lax.broadcasted_iota(jnp.int32, (CHUNK, CHUNK), 0)
    s = lax.broadcasted_iota(jnp.int32, (CHUNK, CHUNK), 1)
    causal = s <= t
    for g in range(N_GROUPS):
        cols = slice(g * group_dim, (g + 1) * group_dim)
        wc = jnp.where(causal, ws_ref[g], 0.0).astype(BF16)
        mix = jnp.dot(wc, vb[:, cols], preferred_element_type=F32) + bias_ref[:, cols]
        o_ref[0, :, cols] = (u_ref[0, :, cols].astype(F32) * mix).astype(o_ref.dtype)


def _spatial_sample_kernel(v_ref, u_ref, lng_ref, lnb_ref, w00_ref, b0_ref, o_ref, gv_ref):
    vn = _layer_norm(v_ref[0], lng_ref[...], lnb_ref[...])
    gv_ref[0] = vn
    mix = w00_ref[...].astype(BF16).astype(F32) * vn.astype(BF16).astype(F32) + b0_ref[...]
    o_ref[0] = (u_ref[0].astype(F32) * mix).astype(o_ref.dtype)


def spatial_gate(v, u, ln_g, ln_b, w_s, b_s, *, seq_len):
    n_seq, rows, d = v.shape
    n_dec = rows - seq_len
    group_dim = d // N_GROUPS
    ln_g, ln_b = ln_g.reshape(1, d), ln_b.reshape(1, d)
    bias_full = jnp.repeat(b_s.T, group_dim, axis=1)
    row = pl.BlockSpec((1, CHUNK, d), lambda s, c: (s, c, 0))
    vec = pl.BlockSpec((1, d), lambda s, c: (0, 0))
    gated, gv_prompt = pl.pallas_call(
        functools.partial(_spatial_kernel, group_dim=group_dim),
        out_shape=(jax.ShapeDtypeStruct(v.shape, BF16),
                   jax.ShapeDtypeStruct((n_seq, CHUNK, d), F32)),
        grid=(n_seq, seq_len // CHUNK),
        in_specs=[row, row, vec, vec,
                  pl.BlockSpec((N_GROUPS, CHUNK, CHUNK), lambda s, c: (0, 0, 0)),
                  pl.BlockSpec((CHUNK, d), lambda s, c: (0, 0))],
        out_specs=(row, pl.BlockSpec((1, CHUNK, d), lambda s, c: (s, 0, 0))),
        compiler_params=_params("parallel", "arbitrary"),
        name="spatial_gate_prompt",
    )(v, u, ln_g, ln_b, w_s, bias_full)
    w00 = jnp.repeat(w_s[:, 0, 0], group_dim).reshape(1, d)
    b0 = jnp.repeat(b_s[:, 0], group_dim).reshape(1, d)
    tile0_rows = pl.BlockSpec((1, n_dec, d), lambda s: (0, seq_len // n_dec, 0))
    vec1 = pl.BlockSpec((1, d), lambda s: (0, 0))
    gated, gv_sample = pl.pallas_call(
        _skip_aliased(_spatial_sample_kernel, 1),
        out_shape=(jax.ShapeDtypeStruct(v.shape, BF16), jax.ShapeDtypeStruct((1, n_dec, d), F32)),
        grid=(n_seq,),
        in_specs=[pl.BlockSpec(memory_space=pl.ANY), tile0_rows, tile0_rows, vec1, vec1, vec1, vec1],
        out_specs=(pl.BlockSpec((1, n_dec, d), lambda s: (s, seq_len // n_dec, 0)),
                   pl.BlockSpec((1, n_dec, d), lambda s: (0, 0, 0))),
        input_output_aliases={0: 0},
        compiler_params=_params("arbitrary"),
        name="spatial_gate_sample",
    )(gated, v, u, ln_g, ln_b, w00, b0)
    return gated, gv_prompt, gv_sample


def _pair_block_diag(pair, even):
    lane = lax.broadcasted_iota(jnp.int32, pair.shape, 1)
    if even:
        own = jnp.where(lane < HEAD_DIM, pair, 0.0)
        return jnp.concatenate([own, pltpu.roll(own, HEAD_DIM, 1)], axis=0)
    own = jnp.where(lane >= HEAD_DIM, pair, 0.0)
    return jnp.concatenate([pltpu.roll(own, HEAD_DIM, 1), own], axis=0)


def _attn_prompt_kernel(sink_ref, q_ref, kprev_ref, kown_ref, vprev_ref, vown_ref, o_ref):
    blk = pl.program_id(1)
    n_keys = 2 * WINDOW
    qi = lax.broadcasted_iota(jnp.int32, (WINDOW, n_keys), 0)
    kj = lax.broadcasted_iota(jnp.int32, (WINDOW, n_keys), 1)
    first_key = jnp.where(blk > 0, 0, WINDOW)
    valid = (kj >= qi) & (kj <= qi + WINDOW) & (kj >= first_key)
    col_blocks = Q_PER_KV * HEAD_DIM // V7X_LANES
    valid = jnp.concatenate([valid] * col_blocks, axis=0)
    block_of_row = lax.broadcasted_iota(jnp.int32, (col_blocks * WINDOW, 1), 0) // WINDOW
    head_lane = lax.broadcasted_iota(jnp.int32, (col_blocks * WINDOW, V7X_LANES), 1)
    group_w = Q_PER_KV * HEAD_DIM
    for pair in range(N_KV_HEADS // 2):
        lanes = slice(pair * V7X_LANES, (pair + 1) * V7X_LANES)
        k_pair = jnp.concatenate([kprev_ref[0, :, lanes], kown_ref[0, :, lanes]], axis=0)
        v_pair = jnp.concatenate([vprev_ref[0, :, lanes], vown_ref[0, :, lanes]], axis=0)
        for e in range(2):
            h = 2 * pair + e
            k2 = _pair_block_diag(k_pair, e == 0).astype(BF16)
            v2 = _pair_block_diag(v_pair, e == 0).astype(BF16)
            qs = jnp.concatenate(
                [q_ref[0, :, h * group_w + c * V7X_LANES: h * group_w + (c + 1) * V7X_LANES]
                 for c in range(col_blocks)], axis=0)
            qs = (qs.astype(F32) * (ATTN_SCALE * LOG2_E)).astype(BF16)
            s = lax.dot_general(qs, k2, (((1,), (1,)), ((), ())), preferred_element_type=F32)
            probs, inv_denoms = [], []
            for half in range(2):
                sh = jnp.where(valid, s[:, half * n_keys:(half + 1) * n_keys], MASKED_SCORE)
                sink = jnp.zeros((col_blocks * WINDOW, 1), F32)
                for c in range(col_blocks):
                    sink = jnp.where(block_of_row == c, sink_ref[h * Q_PER_KV + 2 * c + half] * LOG2_E,
                                     sink)
                m = jnp.maximum(jnp.max(sh, axis=-1, keepdims=True), sink)
                p = jnp.exp2(sh - m)
                inv_denoms.append(1.0 / (jnp.sum(p, axis=-1, keepdims=True) + jnp.exp2(sink - m)))
                probs.append(p.astype(BF16))
            o = jnp.dot(jnp.concatenate(probs, axis=1), v2, preferred_element_type=F32)
            o = o * jnp.where(head_lane < HEAD_DIM, inv_denoms[0], inv_denoms[1])
            for c in range(col_blocks):
                o_ref[0, :, h * group_w + c * V7X_LANES: h * group_w + (c + 1) * V7X_LANES] = (
                    o[c * WINDOW:(c + 1) * WINDOW].astype(o_ref.dtype))


def attention_prompt(q, kv, sinks, *, seq_len):
    n_seq, rows, d_q = q.shape
    d_kv = N_KV_HEADS * HEAD_DIM
    k_col, v_col = 0, 1

    def own(col):
        return lambda s, i: (s, i, col)

    def prev(col):
        return lambda s, i: (s, jnp.maximum(i - 1, 0), col)

    return pl.pallas_call(
        _attn_prompt_kernel,
        out_shape=jax.ShapeDtypeStruct((n_seq, rows, d_q), BF16),
        grid=(n_seq, seq_len // WINDOW),
        in_specs=[pl.BlockSpec(memory_space=pltpu.SMEM),
                  pl.BlockSpec((1, WINDOW, d_q), own(0)),
                  pl.BlockSpec((1, WINDOW, d_kv), prev(k_col)),
                  pl.BlockSpec((1, WINDOW, d_kv), own(k_col)),
                  pl.BlockSpec((1, WINDOW, d_kv), prev(v_col)),
                  pl.BlockSpec((1, WINDOW, d_kv), own(v_col))],
        out_specs=pl.BlockSpec((1, WINDOW, d_q), own(0)),
        compiler_params=_params("parallel", "arbitrary"),
        name="attention_prompt",
    )(sinks, q, kv, kv, kv, kv)


def _attn_sample_kernel(q2_ref, ck_ref, cv_ref, knew_ref, vnew_ref, sink_ref, o_ref):
    for pair in range(N_KV_HEADS // 2):
        lanes = slice(pair * V7X_LANES, (pair + 1) * V7X_LANES)
        q2 = q2_ref[0, pair]
        k_pair = ck_ref[0, :, lanes].astype(BF16)
        v_pair = cv_ref[0, :, lanes].astype(BF16)
        k_new = knew_ref[0, :, lanes].astype(BF16).astype(F32)
        v_new = vnew_ref[0, :, lanes].astype(BF16).astype(F32)
        sink = sink_ref[pair][:, :1]
        s = lax.dot_general(q2, k_pair, (((1,), (1,)), ((), ())), preferred_element_type=F32)
        s_new = jnp.sum(q2.astype(F32) * k_new, axis=-1, keepdims=True)
        m = jnp.maximum(jnp.maximum(jnp.max(s, axis=-1, keepdims=True), s_new), sink)
        p = jnp.exp(s - m)
        p_new = jnp.exp(s_new - m)
        denom = jnp.sum(p, axis=-1, keepdims=True) + p_new + jnp.exp(sink - m)
        o = jnp.dot((p / denom).astype(BF16), v_pair, preferred_element_type=F32)
        o_ref[0, pair] = o + (p_new / denom).astype(BF16).astype(F32) * v_new


def attention_sample(q, kv, cache_k, cache_v, sinks):
    n, d_q = q.shape
    d_kv = N_KV_HEADS * HEAD_DIM
    n_pairs = N_KV_HEADS // 2
    rows = 2 * Q_PER_KV
    q = (q.astype(F32) * ATTN_SCALE).astype(BF16).reshape(n, n_pairs, 2, Q_PER_KV, HEAD_DIM)
    zeros = jnp.zeros((n, n_pairs, Q_PER_KV, HEAD_DIM), BF16)
    q2 = jnp.concatenate([jnp.concatenate([q[:, :, 0], zeros], axis=-1),
                          jnp.concatenate([zeros, q[:, :, 1]], axis=-1)], axis=2)
    k_new = kv[:, :d_kv].reshape(n, 1, d_kv)
    v_new = kv[:, d_kv:].reshape(n, 1, d_kv)
    sink2 = jnp.broadcast_to(sinks.reshape(n_pairs, rows, 1), (n_pairs, rows, V7X_LANES))
    o2 = pl.pallas_call(
        _attn_sample_kernel,
        out_shape=jax.ShapeDtypeStruct((n, n_pairs, rows, V7X_LANES), F32),
        grid=(n,),
        in_specs=[pl.BlockSpec((1, n_pairs, rows, V7X_LANES), lambda b: (b, 0, 0, 0)),
                  pl.BlockSpec((1, WINDOW, d_kv), lambda b: (b, 0, 0)),
                  pl.BlockSpec((1, WINDOW, d_kv), lambda b: (b, 0, 0)),
                  pl.BlockSpec((1, 1, d_kv), lambda b: (b, 0, 0)),
                  pl.BlockSpec((1, 1, d_kv), lambda b: (b, 0, 0)),
                  pl.BlockSpec((n_pairs, rows, V7X_LANES), lambda b: (0, 0, 0))],
        out_specs=pl.BlockSpec((1, n_pairs, rows, V7X_LANES), lambda b: (b, 0, 0, 0)),
        compiler_params=_params("parallel"),
        name="attention_sample",
    )(q2, cache_k.reshape(n, WINDOW, d_kv), cache_v.reshape(n, WINDOW, d_kv), k_new, v_new, sink2)
    o = jnp.stack([o2[:, :, :Q_PER_KV, :HEAD_DIM], o2[:, :, Q_PER_KV:, HEAD_DIM:]], axis=2)
    return o.reshape(n, d_q).astype(BF16), k_new, v_new


def _silu_gate(gate, val):
    return gate * (1.0 / (1.0 + jnp.exp(-gate))) * val


def _ffn_up_kernel(x_ref, wg_ref, wv_ref, cwg_ref, cwv_ref, cbg_ref, cbv_ref,
                   p0g_ref, p0v_ref, p1g_ref, p1v_ref,
                   a_ref, sg_ref, sv_ref, hsg_ref, hsv_ref, h_ref, tail_ref, *, halves, n_dec):
    half = pl.program_id(0) % halves
    j = pl.program_id(1)
    tn = wg_ref.shape[1]
    tail_rows = tail_ref.shape[1]
    tm = x_ref.shape[1]
    gate_cols, val_cols = slice(0, tn), slice(tn, 2 * tn)

    @pl.when((pl.program_id(0) == 0) & (j == 0))
    def _():
        tail_ref[...] = jnp.zeros(tail_ref.shape, tail_ref.dtype)

    first_tile = jnp.full((tail_rows, 2 * tn), half, jnp.int32) == 0
    h_ref[0:tail_rows] = jnp.where(first_tile, 0.0, tail_ref[j])

    def conv(r0, r1, cols, cw_ref, cb_ref):
        ext = h_ref[r0:r1 + tail_rows, cols]
        h1 = pltpu.roll(ext, 1, 0)[tail_rows:]
        h2 = pltpu.roll(ext, 2, 0)[tail_rows:]
        return cb_ref[...] + cw_ref[0:1] * h2 + cw_ref[1:2] * h1 + cw_ref[2:3] * ext[tail_rows:]

    def epilogue(r0, r1):
        a_ref[0, r0:r1] = _silu_gate(conv(r0, r1, gate_cols, cwg_ref, cbg_ref),
                                     conv(r0, r1, val_cols, cwv_ref, cbv_ref)).astype(a_ref.dtype)

    pending = None
    for r0, r1 in _row_chunks(tm):
        x = x_ref[0, r0:r1]
        h_ref[tail_rows + r0:tail_rows + r1, gate_cols] = _dot_bf16(x, wg_ref[...])
        h_ref[tail_rows + r0:tail_rows + r1, val_cols] = _dot_bf16(x, wv_ref[...])
        if pending is not None:
            epilogue(*pending)
        pending = (r0, r1)
    epilogue(*pending)
    tail_ref[j] = h_ref[tm:tm + tail_rows]

    h_sample = h_ref[tail_rows + tm - n_dec:tail_rows + tm]
    hsg_ref[0, 0] = h_sample[:, gate_cols]
    hsv_ref[0, 0] = h_sample[:, val_cols]
    seq_tail = h_ref[tm - n_dec:tm - n_dec + tail_rows]
    sg_ref[0, 0] = seq_tail[:, gate_cols]
    sv_ref[0, 0] = seq_tail[:, val_cols]

    @pl.when(half == halves - 1)
    def _():
        def conv_sample(cols, p0_ref, p1_ref, cw_ref, cb_ref):
            return (cb_ref[...] + cw_ref[0:1] * p0_ref[...] + cw_ref[1:2] * p1_ref[...]
                    + cw_ref[2:3] * h_sample[:, cols])

        a_ref[0, tm - n_dec:tm] = _silu_gate(
            conv_sample(gate_cols, p0g_ref, p1g_ref, cwg_ref, cbg_ref),
            conv_sample(val_cols, p0v_ref, p1v_ref, cwv_ref, cbv_ref)).astype(a_ref.dtype)


def ffn_up(h, w_up, conv_w, conv_b, state, *, layer, seq_len, halves, tn, tail_rows):
    n_seq, rows, k = h.shape
    n_dec = rows - seq_len
    d_ff = w_up.shape[2] // 2
    nj = d_ff // tn
    tm = rows // halves
    conv_b = conv_b.reshape(conv_b.shape[0], 1, -1)
    p0, p1 = state[:, 0], state[:, 1]

    def gate(shape):
        return pl.BlockSpec(shape, lambda i, j: (0, j))

    def val(shape):
        return pl.BlockSpec(shape, lambda i, j: (0, j + nj))

    def gate_l(shape):
        return pl.BlockSpec((None,) + shape, lambda i, j: (layer, 0, j))

    def val_l(shape):
        return pl.BlockSpec((None,) + shape, lambda i, j: (layer, 0, j + nj))

    per_tile = lambda r: pl.BlockSpec((1, 1, r, tn), lambda i, j: (i // halves, i % halves, 0, j))
    a, sg, sv, hsg, hsv = pl.pallas_call(
        functools.partial(_ffn_up_kernel, halves=halves, n_dec=n_dec),
        out_shape=(jax.ShapeDtypeStruct((n_seq, rows, d_ff), BF16),
                   jax.ShapeDtypeStruct((n_seq, halves, tail_rows, d_ff), F32),
                   jax.ShapeDtypeStruct((n_seq, halves, tail_rows, d_ff), F32),
                   jax.ShapeDtypeStruct((n_seq, halves, n_dec, d_ff), F32),
                   jax.ShapeDtypeStruct((n_seq, halves, n_dec, d_ff), F32)),
        grid=(n_seq * halves, nj),
        in_specs=[pl.BlockSpec((1, tm, k), lambda i, j: (i // halves, i % halves, 0)),
                  gate_l((k, tn)), val_l((k, tn)), gate_l((CONV_W, tn)), val_l((CONV_W, tn)),
                  gate_l((1, tn)), val_l((1, tn)),
                  gate((n_dec, tn)), val((n_dec, tn)), gate((n_dec, tn)), val((n_dec, tn))],
        out_specs=(pl.BlockSpec((1, tm, tn), lambda i, j: (i // halves, i % halves, j)),
                   per_tile(tail_rows), per_tile(tail_rows), per_tile(n_dec), per_tile(n_dec)),
        scratch_shapes=[pltpu.VMEM((tail_rows + tm, 2 * tn), F32),
                        pltpu.VMEM((nj, tail_rows, 2 * tn), F32)],
        compiler_params=_params("arbitrary", "arbitrary"),
        name="ffn_up",
    )(h, w_up, w_up, conv_w, conv_w, conv_b, conv_b, p0, p0, p1, p1)
    keep = slice(tail_rows - (CONV_W - 1), tail_rows)
    state_prompt = jnp.concatenate([sg[:, halves - 1, keep], sv[:, halves - 1, keep]], axis=-1)
    h_sample = jnp.concatenate([hsg[0, halves - 1], hsv[0, halves - 1]], axis=-1)
    return a, state_prompt, h_sample


ROW_TILES = 2
TN = 512
TN_FFN_UP = V7X_MXU_COLS
TR_PROMPT = 256
TILED_ROW_BLOCKS = 10
FFN_DOWN_ROW_BLOCKS = 5
FFN_DOWN_WEIGHT_PARTS = 4
STATE_TAIL_ROWS = V7X_SUBLANES


def kernel(x_prompt, x_sample, cache_win_k, cache_win_v, state_conv, norm_mix_pre, norm_mix_post,
           norm_ffn_pre, norm_ffn_post, gmlp_w_in, gmlp_ln_g, gmlp_ln_b, gmlp_w_s, gmlp_b_s,
           gmlp_w_out, attn_w_qkv, attn_b_qkv, attn_sinks, attn_w_o, attn_b_o, ffn_w_up,
           ffn_conv_w, ffn_conv_b, ffn_w_down):
    n_seq, seq_len, d = x_prompt.shape
    n_dec = x_sample.shape[0]
    depth = norm_mix_pre.shape[0]
    rows = seq_len + n_dec
    x_sample = x_sample.reshape(1, n_dec, d)
    mm = functools.partial(matmul, row_tiles=ROW_TILES, tn=TN)

    h = rmsnorm_first(x_prompt, x_sample, norm_mix_pre[0], tr=TR_PROMPT)
    x = None
    gv_p, gv_s, wk_p, wv_p, wk_s, wv_s, cv_p, cv_s = [], [], [], [], [], [], [], []
    for layer in range(depth):
        idx = layer // 2
        if layer % 2 == 0:
            d_g = gmlp_w_out.shape[1]
            u = mm(h, gmlp_w_in, layer=idx, n_out=d_g, out_dtype=BF16, epilogue=_gelu, name="gmlp_in_u")
            v = mm(h, gmlp_w_in, layer=idx, n_out=d_g, col_block_offset=d_g // TN, epilogue=_gelu,
                   name="gmlp_in_v")
            gated, g_p, g_s = spatial_gate(v, u, gmlp_ln_g[idx], gmlp_ln_b[idx], gmlp_w_s[idx],
                                           gmlp_b_s[idx], seq_len=seq_len)
            gv_p.append(g_p)
            gv_s.append(g_s.reshape(n_dec, 1, d_g))
            y = mm(gated, gmlp_w_out, layer=idx, n_out=d, out_dtype=BF16, name="gmlp_out")
        else:
            d_q = attn_w_o.shape[1]
            d_qkv = attn_w_qkv.shape[2]
            d_kv = (d_qkv - d_q) // 2
            q = mm(h, attn_w_qkv, attn_b_qkv, layer=idx, n_out=d_q, out_dtype=BF16, name="attn_q")
            kv = mm(h, attn_w_qkv, attn_b_qkv, layer=idx, n_out=2 * d_kv, col_block_offset=d_q // TN,
                    name="attn_kv")
            o = attention_prompt(q, kv, attn_sinks[idx], seq_len=seq_len)
            o_s, k_new, v_new = attention_sample(q[0, seq_len:], kv[0, seq_len:], cache_win_k[idx],
                                                 cache_win_v[idx], attn_sinks[idx])
            o = insert_sample_rows(o, o_s, seq_len=seq_len)
            tail = kv[:, seq_len - WINDOW:seq_len]
            wk_p.append(tail[:, :, :d_kv].reshape(n_seq, WINDOW, N_KV_HEADS, HEAD_DIM))
            wv_p.append(tail[:, :, d_kv:].reshape(n_seq, WINDOW, N_KV_HEADS, HEAD_DIM))
            wk_s.append(jnp.concatenate(
                [cache_win_k[idx][:, 1:], k_new.reshape(n_dec, 1, N_KV_HEADS, HEAD_DIM)], axis=1))
            wv_s.append(jnp.concatenate(
                [cache_win_v[idx][:, 1:], v_new.reshape(n_dec, 1, N_KV_HEADS, HEAD_DIM)], axis=1))
            y = mm(o, attn_w_o, attn_b_o, layer=idx, n_out=d, out_dtype=BF16, name="attn_out")
        if x is None:
            x, h = resnorm_first(x_prompt, x_sample, y, norm_mix_post[layer], norm_ffn_pre[layer],
                                 tr=TR_PROMPT)
        else:
            x, h = resnorm_tiled(x, y, norm_mix_post[layer], norm_ffn_pre[layer],
                                 tr=rows // TILED_ROW_BLOCKS)

        a, c_p, hu_s = ffn_up(h, ffn_w_up, ffn_conv_w, ffn_conv_b, state_conv[layer], layer=layer,
                              seq_len=seq_len, halves=ROW_TILES, tn=TN_FFN_UP, tail_rows=STATE_TAIL_ROWS)
        cv_p.append(c_p)
        cv_s.append(jnp.concatenate([state_conv[layer][:, 1:], hu_s[:, None]], axis=1))
        f = matmul_weight_resident(a, ffn_w_down, layer=layer, tm=rows // FFN_DOWN_ROW_BLOCKS, tn=TN,
                                   parts=FFN_DOWN_WEIGHT_PARTS, name="ffn_down")
        if layer + 1 < depth:
            x, h = resnorm_tiled(x, f, norm_ffn_post[layer], norm_mix_pre[layer + 1],
                                 tr=rows // TILED_ROW_BLOCKS)
        else:
            y_prompt, y_sample = resnorm_last(x, f, norm_ffn_post[layer], seq_len=seq_len, n_dec=n_dec,
                                              tr=TR_PROMPT)

    return (y_prompt, y_sample, jnp.stack(gv_p), jnp.stack(gv_s),
            jnp.stack(wk_p), jnp.stack(wv_p), jnp.stack(wk_s), jnp.stack(wv_s),
            jnp.stack(cv_p), jnp.stack(cv_s))
```

```python
import functools
import math

import jax
import jax.numpy as jnp
from jax import lax
from jax.experimental import pallas as pl
from jax.experimental.pallas import tpu as pltpu

F32 = jnp.float32
BF16 = jnp.bfloat16

NORM_EPS = 1e-6
CHUNK = 128
N_GROUPS = 16
HEAD_DIM = 64
N_KV_HEADS = 8
Q_PER_KV = 8
WINDOW = 128
CONV_W = 3
ATTN_SCALE = HEAD_DIM ** -0.5
MASKED_SCORE = -1e30
LOG2_E = math.log2(math.e)

V7X_LANES = 128
V7X_SUBLANES = 8
V7X_MXU_COLS = 256
V7X_SCOPED_VMEM_BYTES = 60000 * 1024

DOT_ROWS = 512


def _params(*semantics):
    return pltpu.CompilerParams(dimension_semantics=semantics,
                                vmem_limit_bytes=V7X_SCOPED_VMEM_BYTES)


def _rms(x, g):
    return x * lax.rsqrt(jnp.mean(x * x, axis=-1, keepdims=True) + NORM_EPS) * g


def _gelu(x):
    return 0.5 * x * (1.0 + lax.erf(x * math.sqrt(0.5)))


def _identity(x):
    return x


def _dot_bf16(x, w):
    return jnp.dot(x, w.astype(BF16), preferred_element_type=F32)


def _row_chunks(rows, size=DOT_ROWS):
    n = max(rows // size, 1)
    return [(c * size, (c + 1) * size if c + 1 < n else rows) for c in range(n)]


def _skip_aliased(body, n_aliased):
    def kernel_fn(*refs):
        body(*refs[n_aliased:])
    return kernel_fn


def _rmsnorm_kernel(x_ref, g_ref, h_ref):
    h_ref[0] = _rms(x_ref[0], g_ref[...]).astype(h_ref.dtype)


def _resnorm_kernel(x_ref, y_ref, gpost_ref, gnext_ref, xo_ref, ho_ref):
    xn = x_ref[0] + _rms(y_ref[0].astype(F32), gpost_ref[...])
    xo_ref[0] = xn
    ho_ref[0] = _rms(xn, gnext_ref[...]).astype(ho_ref.dtype)


def _resnorm_last_kernel(x_ref, y_ref, gpost_ref, xo_ref):
    xo_ref[0] = x_ref[0] + _rms(y_ref[0].astype(F32), gpost_ref[...])


def rmsnorm_first(x_prompt, x_sample, g, *, tr):
    n_seq, seq_len, d = x_prompt.shape
    n_dec = x_sample.shape[1]
    g = g.reshape(1, d)
    shape = jax.ShapeDtypeStruct((n_seq, seq_len + n_dec, d), BF16)
    row = pl.BlockSpec((1, tr, d), lambda s, r: (s, r, 0))
    h = pl.pallas_call(
        _rmsnorm_kernel, out_shape=shape, grid=(n_seq, seq_len // tr),
        in_specs=[row, pl.BlockSpec((1, d), lambda s, r: (0, 0))], out_specs=row,
        compiler_params=_params("parallel", "parallel"), name="rmsnorm_first_prompt",
    )(x_prompt, g)
    return pl.pallas_call(
        _skip_aliased(_rmsnorm_kernel, 1), out_shape=shape, grid=(n_seq,),
        in_specs=[pl.BlockSpec(memory_space=pl.ANY),
                  pl.BlockSpec((1, n_dec, d), lambda s: (0, 0, 0)),
                  pl.BlockSpec((1, d), lambda s: (0, 0))],
        out_specs=pl.BlockSpec((1, n_dec, d), lambda s: (s, seq_len // n_dec, 0)),
        input_output_aliases={0: 0},
        compiler_params=_params("arbitrary"), name="rmsnorm_first_sample",
    )(h, x_sample, g)


def resnorm_first(x_prompt, x_sample, y, g_post, g_next, *, tr):
    n_seq, seq_len, d = x_prompt.shape
    n_dec = x_sample.shape[1]
    g_post, g_next = g_post.reshape(1, d), g_next.reshape(1, d)
    shapes = (jax.ShapeDtypeStruct(y.shape, F32), jax.ShapeDtypeStruct(y.shape, BF16))
    row = pl.BlockSpec((1, tr, d), lambda s, r: (s, r, 0))
    vec = pl.BlockSpec((1, d), lambda s, r: (0, 0))
    xo, ho = pl.pallas_call(
        _resnorm_kernel, out_shape=shapes, grid=(n_seq, seq_len // tr),
        in_specs=[row, row, vec, vec], out_specs=(row, row),
        compiler_params=_params("parallel", "parallel"), name="resnorm_first_prompt",
    )(x_prompt, y, g_post, g_next)
    sample_rows = pl.BlockSpec((1, n_dec, d), lambda s: (s, seq_len // n_dec, 0))
    vec1 = pl.BlockSpec((1, d), lambda s: (0, 0))
    return pl.pallas_call(
        _skip_aliased(_resnorm_kernel, 2), out_shape=shapes, grid=(n_seq,),
        in_specs=[pl.BlockSpec(memory_space=pl.ANY), pl.BlockSpec(memory_space=pl.ANY),
                  pl.BlockSpec((1, n_dec, d), lambda s: (0, 0, 0)),
                  pl.BlockSpec((1, n_dec, d), lambda s: (0, seq_len // n_dec, 0)), vec1, vec1],
        out_specs=(sample_rows, sample_rows),
        input_output_aliases={0: 0, 1: 1},
        compiler_params=_params("arbitrary"), name="resnorm_first_sample",
    )(xo, ho, x_sample, y, g_post, g_next)


def resnorm_tiled(x, y, g_post, g_next, *, tr):
    n_seq, rows, d = x.shape
    row = pl.BlockSpec((1, tr, d), lambda s, r: (s, r, 0))
    vec = pl.BlockSpec((1, d), lambda s, r: (0, 0))
    return pl.pallas_call(
        _resnorm_kernel,
        out_shape=(jax.ShapeDtypeStruct(x.shape, F32), jax.ShapeDtypeStruct(x.shape, BF16)),
        grid=(n_seq, rows // tr),
        in_specs=[row, row, vec, vec], out_specs=(row, row),
        compiler_params=_params("parallel", "parallel"), name="resnorm_tiled",
    )(x, y, g_post.reshape(1, d), g_next.reshape(1, d))


def resnorm_last(x, y, g_post, *, seq_len, n_dec, tr):
    n_seq, _, d = x.shape
    g_post = g_post.reshape(1, d)
    row = pl.BlockSpec((1, tr, d), lambda s, r: (s, r, 0))
    y_prompt = pl.pallas_call(
        _resnorm_last_kernel, out_shape=jax.ShapeDtypeStruct((n_seq, seq_len, d), F32),
        grid=(n_seq, seq_len // tr),
        in_specs=[row, row, pl.BlockSpec((1, d), lambda s, r: (0, 0))], out_specs=row,
        compiler_params=_params("parallel", "parallel"), name="resnorm_last_prompt",
    )(x, y, g_post)
    sample_rows = pl.BlockSpec((1, n_dec, d), lambda s: (0, seq_len // n_dec, 0))
    y_sample = pl.pallas_call(
        _resnorm_last_kernel, out_shape=jax.ShapeDtypeStruct((1, n_dec, d), F32), grid=(1,),
        in_specs=[sample_rows, sample_rows, pl.BlockSpec((1, d), lambda s: (0, 0))],
        out_specs=pl.BlockSpec((1, n_dec, d), lambda s: (0, 0, 0)),
        compiler_params=_params("arbitrary"), name="resnorm_last_sample",
    )(x, y, g_post)
    return y_prompt, y_sample.reshape(n_dec, 1, d)


def _copy_rows_kernel(src_ref, o_ref):
    o_ref[0] = src_ref[...]


def insert_sample_rows(tiled, rows, *, seq_len):
    n_seq, _, c = tiled.shape
    n_dec = rows.shape[0]
    return pl.pallas_call(
        _skip_aliased(_copy_rows_kernel, 1), out_shape=jax.ShapeDtypeStruct(tiled.shape, tiled.dtype),
        grid=(n_seq,),
        in_specs=[pl.BlockSpec(memory_space=pl.ANY), pl.BlockSpec((n_dec, c), lambda s: (0, 0))],
        out_specs=pl.BlockSpec((1, n_dec, c), lambda s: (s, seq_len // n_dec, 0)),
        input_output_aliases={0: 0},
        compiler_params=_params("arbitrary"), name="insert_sample_rows",
    )(tiled, rows)


def _resident_row_tile(x_hbm, xbuf_ref, sem, *, row_tiles):
    i = pl.program_id(0)
    j = pl.program_id(1)
    tm = xbuf_ref.shape[1]

    def tile_copy(t):
        rows = pl.ds(pl.multiple_of((t % row_tiles) * tm, 2 * V7X_SUBLANES), tm)
        return pltpu.make_async_copy(x_hbm.at[t // row_tiles, rows, :], xbuf_ref.at[t % 2], sem.at[t % 2])

    @pl.when(j == 0)
    def _():
        @pl.when(i == 0)
        def _():
            tile_copy(0).start()

        @pl.when(i + 1 < pl.num_programs(0))
        def _():
            tile_copy(i + 1).start()

        tile_copy(i).wait()

    return xbuf_ref.at[i % 2]


def _matmul_kernel(*refs, epilogue, has_bias, row_tiles):
    if has_bias:
        x_hbm, w_ref, b_ref, o_ref, xbuf_ref, sem = refs
    else:
        x_hbm, w_ref, o_ref, xbuf_ref, sem = refs
    x_ref = _resident_row_tile(x_hbm, xbuf_ref, sem, row_tiles=row_tiles)
    for r0, r1 in _row_chunks(x_ref.shape[0]):
        acc = _dot_bf16(x_ref[r0:r1], w_ref[...])
        if has_bias:
            acc = acc + b_ref[...]
        o_ref[0, r0:r1] = epilogue(acc).astype(o_ref.dtype)


def matmul(x, w, bias=None, *, layer, row_tiles, tn, n_out, col_block_offset=0, out_dtype=F32,
           epilogue=_identity, name):
    n_seq, rows, k = x.shape
    tm = rows // row_tiles
    in_specs = [pl.BlockSpec(memory_space=pl.ANY),
                pl.BlockSpec((None, k, tn), lambda i, j: (layer, 0, j + col_block_offset))]
    args = [x, w]
    if bias is not None:
        in_specs.append(pl.BlockSpec((None, 1, tn), lambda i, j: (layer, 0, j + col_block_offset)))
        args.append(bias.reshape(bias.shape[0], 1, -1))
    return pl.pallas_call(
        functools.partial(_matmul_kernel, epilogue=epilogue, has_bias=bias is not None,
                          row_tiles=row_tiles),
        out_shape=jax.ShapeDtypeStruct((n_seq, rows, n_out), out_dtype),
        grid=(n_seq * row_tiles, n_out // tn),
        in_specs=in_specs,
        out_specs=pl.BlockSpec((1, tm, tn), lambda i, j: (i // row_tiles, i % row_tiles, j)),
        scratch_shapes=[pltpu.VMEM((2, tm, k), x.dtype), pltpu.SemaphoreType.DMA((2,))],
        compiler_params=_params("arbitrary", "arbitrary"),
        name=name,
    )(*args)


def _matmul_wres_kernel(x_ref, w_hbm, o_ref, wb_ref, stage_ref, sem, *, layer, parts, steps_per_part):
    j = pl.program_id(0)
    i = pl.program_id(1)
    n_col_tiles = pl.num_programs(0)
    _, k, tn = wb_ref.shape
    kp = k // parts

    def part_copy(col_tile, part):
        rows = pl.ds(pl.multiple_of(part * kp, V7X_SUBLANES), kp)
        cols = pl.ds(pl.multiple_of(col_tile * tn, V7X_LANES), tn)
        return pltpu.make_async_copy(w_hbm.at[layer, rows, cols], stage_ref, sem.at[0])

    def round_part(col_tile, part):
        rows = pl.ds(pl.multiple_of(part * kp, 2 * V7X_SUBLANES), kp)
        wb_ref[col_tile % 2, rows, :] = stage_ref[...].astype(BF16)

    @pl.when((j == 0) & (i == 0))
    def _():
        for part in range(parts):
            part_copy(0, part).start()
            part_copy(0, part).wait()
            round_part(0, part)

    part = i // steps_per_part
    phase = i % steps_per_part
    prefetching = (j + 1 < n_col_tiles) & (part < parts)

    @pl.when(prefetching & (phase == 0))
    def _():
        part_copy(j + 1, part).start()

    @pl.when(prefetching & (phase == steps_per_part - 2))
    def _():
        part_copy(j + 1, part).wait()
        round_part(j + 1, part)

    o_ref[0] = jnp.dot(x_ref[0], wb_ref[j % 2], preferred_element_type=F32).astype(o_ref.dtype)


def matmul_weight_resident(x, w, *, layer, tm, tn, parts, name):
    n_seq, rows, k = x.shape
    n = w.shape[2]
    per_seq = rows // tm
    n_row_tiles = n_seq * per_seq
    steps_per_part = n_row_tiles // parts
    assert steps_per_part >= 2 and k % (parts * 2 * V7X_SUBLANES) == 0
    return pl.pallas_call(
        functools.partial(_matmul_wres_kernel, layer=layer, parts=parts, steps_per_part=steps_per_part),
        out_shape=jax.ShapeDtypeStruct((n_seq, rows, n), BF16),
        grid=(n // tn, n_row_tiles),
        in_specs=[pl.BlockSpec((1, tm, k), lambda j, i: (i // per_seq, i % per_seq, 0)),
                  pl.BlockSpec(memory_space=pl.ANY)],
        out_specs=pl.BlockSpec((1, tm, tn), lambda j, i: (i // per_seq, i % per_seq, j)),
        scratch_shapes=[pltpu.VMEM((2, k, tn), BF16), pltpu.VMEM((k // parts, tn), F32),
                        pltpu.SemaphoreType.DMA((1,))],
        compiler_params=_params("arbitrary", "arbitrary"),
        name=name,
    )(x, w)


def _layer_norm(v, g, b):
    xc = v - jnp.mean(v, axis=-1, keepdims=True)
    return xc * lax.rsqrt(jnp.mean(xc * xc, axis=-1, keepdims=True) + NORM_EPS) * g + b


def _spatial_kernel(v_ref, u_ref, lng_ref, lnb_ref, ws_ref, bias_ref, o_ref, gv_ref, *, group_dim):
    vn = _layer_norm(v_ref[0], lng_ref[...], lnb_ref[...])
    gv_ref[0] = vn
    vb = vn.astype(BF16)
    t = lax.broadcasted_iota(jnp.int32, (CHUNK, CHUNK), 0)
    s = lax.broadcasted_iota(jnp.int32, (CHUNK, CHUNK), 1)
    causal = s <= t
    for g in range(N_GROUPS):
        cols = slice(g * group_dim, (g + 1) * group_dim)
        wc = jnp.where(causal, ws_ref[g], 0.0).astype(BF16)
        mix = jnp.dot(wc, vb[:, cols], preferred_element_type=F32) + bias_ref[:, cols]
        o_ref[0, :, cols] = (u_ref[0, :, cols].astype(F32) * mix).astype(o_ref.dtype)


def _spatial_sample_kernel(v_ref, u_ref, lng_ref, lnb_ref, w00_ref, b0_ref, o_ref, gv_ref):
    vn = _layer_norm(v_ref[0], lng_ref[...], lnb_ref[...])
    gv_ref[0] = vn
    mix = w00_ref[...].astype(BF16).astype(F32) * vn.astype(BF16).astype(F32) + b0_ref[...]
    o_ref[0] = (u_ref[0].astype(F32) * mix).astype(o_ref.dtype)


def spatial_gate(v, u, ln_g, ln_b, w_s, b_s, *, seq_len):
    n_seq, rows, d = v.shape
    n_dec = rows - seq_len
    group_dim = d // N_GROUPS
    ln_g, ln_b = ln_g.reshape(1, d), ln_b.reshape(1, d)
    bias_full = jnp.repeat(b_s.T, group_dim, axis=1)
    row = pl.BlockSpec((1, CHUNK, d), lambda s, c: (s, c, 0))
    vec = pl.BlockSpec((1, d), lambda s, c: (0, 0))
    gated, gv_prompt = pl.pallas_call(
        functools.partial(_spatial_kernel, group_dim=group_dim),
        out_shape=(jax.ShapeDtypeStruct(v.shape, BF16),
                   jax.ShapeDtypeStruct((n_seq, CHUNK, d), F32)),
        grid=(n_seq, seq_len // CHUNK),
        in_specs=[row, row, vec, vec,
                  pl.BlockSpec((N_GROUPS, CHUNK, CHUNK), lambda s, c: (0, 0, 0)),
                  pl.BlockSpec((CHUNK, d), lambda s, c: (0, 0))],
        out_specs=(row, pl.BlockSpec((1, CHUNK, d), lambda s, c: (s, 0, 0))),
        compiler_params=_params("parallel", "arbitrary"),
        name="spatial_gate_prompt",
    )(v, u, ln_g, ln_b, w_s, bias_full)
    w00 = jnp.repeat(w_s[:, 0, 0], group_dim).reshape(1, d)
    b0 = jnp.repeat(b_s[:, 0], group_dim).reshape(1, d)
    tile0_rows = pl.BlockSpec((1, n_dec, d), lambda s: (0, seq_len // n_dec, 0))
    vec1 = pl.BlockSpec((1, d), lambda s: (0, 0))
    gated, gv_sample = pl.pallas_call(
        _skip_aliased(_spatial_sample_kernel, 1),
        out_shape=(jax.ShapeDtypeStruct(v.shape, BF16), jax.ShapeDtypeStruct((1, n_dec, d), F32)),
        grid=(n_seq,),
        in_specs=[pl.BlockSpec(memory_space=pl.ANY), tile0_rows, tile0_rows, vec1, vec1, vec1, vec1],
        out_specs=(pl.BlockSpec((1, n_dec, d), lambda s: (s, seq_len // n_dec, 0)),
                   pl.BlockSpec((1, n_dec, d), lambda s: (0, 0, 0))),
        input_output_aliases={0: 0},
        compiler_params=_params("arbitrary"),
        name="spatial_gate_sample",
    )(gated, v, u, ln_g, ln_b, w00, b0)
    return gated, gv_prompt, gv_sample


def _pair_block_diag(pair, even):
    lane = lax.broadcasted_iota(jnp.int32, pair.shape, 1)
    if even:
        own = jnp.where(lane < HEAD_DIM, pair, 0.0)
        return jnp.concatenate([own, pltpu.roll(own, HEAD_DIM, 1)], axis=0)
    own = jnp.where(lane >= HEAD_DIM, pair, 0.0)
    return jnp.concatenate([pltpu.roll(own, HEAD_DIM, 1), own], axis=0)


def _attn_prompt_kernel(sink_ref, q_ref, kprev_ref, kown_ref, vprev_ref, vown_ref, o_ref):
    blk = pl.program_id(1)
    n_keys = 2 * WINDOW
    qi = lax.broadcasted_iota(jnp.int32, (WINDOW, n_keys), 0)
    kj = lax.broadcasted_iota(jnp.int32, (WINDOW, n_keys), 1)
    first_key = jnp.where(blk > 0, 0, WINDOW)
    valid = (kj >= qi) & (kj <= qi + WINDOW) & (kj >= first_key)
    col_blocks = Q_PER_KV * HEAD_DIM // V7X_LANES
    valid = jnp.concatenate([valid] * col_blocks, axis=0)
    block_of_row = lax.broadcasted_iota(jnp.int32, (col_blocks * WINDOW, 1), 0) // WINDOW
    head_lane = lax.broadcasted_iota(jnp.int32, (col_blocks * WINDOW, V7X_LANES), 1)
    group_w = Q_PER_KV * HEAD_DIM
    for pair in range(N_KV_HEADS // 2):
        lanes = slice(pair * V7X_LANES, (pair + 1) * V7X_LANES)
        k_pair = jnp.concatenate([kprev_ref[0, :, lanes], kown_ref[0, :, lanes]], axis=0)
        v_pair = jnp.concatenate([vprev_ref[0, :, lanes], vown_ref[0, :, lanes]], axis=0)
        for e in range(2):
            h = 2 * pair + e
            k2 = _pair_block_diag(k_pair, e == 0).astype(BF16)
            v2 = _pair_block_diag(v_pair, e == 0).astype(BF16)
            qs = jnp.concatenate(
                [q_ref[0, :, h * group_w + c * V7X_LANES: h * group_w + (c + 1) * V7X_LANES]
                 for c in range(col_blocks)], axis=0)
            qs = (qs.astype(F32) * (ATTN_SCALE * LOG2_E)).astype(BF16)
            s = lax.dot_general(qs, k2, (((1,), (1,)), ((), ())), preferred_element_type=F32)
            probs, inv_denoms = [], []
            for half in range(2):
                sh = jnp.where(valid, s[:, half * n_keys:(half + 1) * n_keys], MASKED_SCORE)
                sink = jnp.zeros((col_blocks * WINDOW, 1), F32)
                for c in range(col_blocks):
                    sink = jnp.where(block_of_row == c, sink_ref[h * Q_PER_KV + 2 * c + half] * LOG2_E,
                                     sink)
                m = jnp.maximum(jnp.max(sh, axis=-1, keepdims=True), sink)
                p = jnp.exp2(sh - m)
                inv_denoms.append(1.0 / (jnp.sum(p, axis=-1, keepdims=True) + jnp.exp2(sink - m)))
                probs.append(p.astype(BF16))
            o = jnp.dot(jnp.concatenate(probs, axis=1), v2, preferred_element_type=F32)
            o = o * jnp.where(head_lane < HEAD_DIM, inv_denoms[0], inv_denoms[1])
            for c in range(col_blocks):
                o_ref[0, :, h * group_w + c * V7X_LANES: h * group_w + (c + 1) * V7X_LANES] = (
                    o[c * WINDOW:(c + 1) * WINDOW].astype(o_ref.dtype))


def attention_prompt(q, kv, sinks, *, seq_len):
    n_seq, rows, d_q = q.shape
    d_kv = N_KV_HEADS * HEAD_DIM
    k_col, v_col = 0, 1

    def own(col):
        return lambda s, i: (s, i, col)

    def prev(col):
        return lambda s, i: (s, jnp.maximum(i - 1, 0), col)

    return pl.pallas_call(
        _attn_prompt_kernel,
        out_shape=jax.ShapeDtypeStruct((n_seq, rows, d_q), BF16),
        grid=(n_seq, seq_len // WINDOW),
        in_specs=[pl.BlockSpec(memory_space=pltpu.SMEM),
                  pl.BlockSpec((1, WINDOW, d_q), own(0)),
                  pl.BlockSpec((1, WINDOW, d_kv), prev(k_col)),
                  pl.BlockSpec((1, WINDOW, d_kv), own(k_col)),
                  pl.BlockSpec((1, WINDOW, d_kv), prev(v_col)),
                  pl.BlockSpec((1, WINDOW, d_kv), own(v_col))],
        out_specs=pl.BlockSpec((1, WINDOW, d_q), own(0)),
        compiler_params=_params("parallel", "arbitrary"),
        name="attention_prompt",
    )(sinks, q, kv, kv, kv, kv)


def _attn_sample_kernel(q2_ref, ck_ref, cv_ref, knew_ref, vnew_ref, sink_ref, o_ref):
    for pair in range(N_KV_HEADS // 2):
        lanes = slice(pair * V7X_LANES, (pair + 1) * V7X_LANES)
        q2 = q2_ref[0, pair]
        k_pair = ck_ref[0, :, lanes].astype(BF16)
        v_pair = cv_ref[0, :, lanes].astype(BF16)
        k_new = knew_ref[0, :, lanes].astype(BF16).astype(F32)
        v_new = vnew_ref[0, :, lanes].astype(BF16).astype(F32)
        sink = sink_ref[pair][:, :1]
        s = lax.dot_general(q2, k_pair, (((1,), (1,)), ((), ())), preferred_element_type=F32)
        s_new = jnp.sum(q2.astype(F32) * k_new, axis=-1, keepdims=True)
        m = jnp.maximum(jnp.maximum(jnp.max(s, axis=-1, keepdims=True), s_new), sink)
        p = jnp.exp(s - m)
        p_new = jnp.exp(s_new - m)
        denom = jnp.sum(p, axis=-1, keepdims=True) + p_new + jnp.exp(sink - m)
        o = jnp.dot((p / denom).astype(BF16), v_pair, preferred_element_type=F32)
        o_ref[0, pair] = o + (p_new / denom).astype(BF16).astype(F32) * v_new


def attention_sample(q, kv, cache_k, cache_v, sinks):
    n, d_q = q.shape
    d_kv = N_KV_HEADS * HEAD_DIM
    n_pairs = N_KV_HEADS // 2
    rows = 2 * Q_PER_KV
    q = (q.astype(F32) * ATTN_SCALE).astype(BF16).reshape(n, n_pairs, 2, Q_PER_KV, HEAD_DIM)
    zeros = jnp.zeros((n, n_pairs, Q_PER_KV, HEAD_DIM), BF16)
    q2 = jnp.concatenate([jnp.concatenate([q[:, :, 0], zeros], axis=-1),
                          jnp.concatenate([zeros, q[:, :, 1]], axis=-1)], axis=2)
    k_new = kv[:, :d_kv].reshape(n, 1, d_kv)
    v_new = kv[:, d_kv:].reshape(n, 1, d_kv)
    sink2 = jnp.broadcast_to(sinks.reshape(n_pairs, rows, 1), (n_pairs, rows, V7X_LANES))
    o2 = pl.pallas_call(
        _attn_sample_kernel,
        out_shape=jax.ShapeDtypeStruct((n, n_pairs, rows, V7X_LANES), F32),
        grid=(n,),
        in_specs=[pl.BlockSpec((1, n_pairs, rows, V7X_LANES), lambda b: (b, 0, 0, 0)),
                  pl.BlockSpec((1, WINDOW, d_kv), lambda b: (b, 0, 0)),
                  pl.BlockSpec((1, WINDOW, d_kv), lambda b: (b, 0, 0)),
                  pl.BlockSpec((1, 1, d_kv), lambda b: (b, 0, 0)),
                  pl.BlockSpec((1, 1, d_kv), lambda b: (b, 0, 0)),
                  pl.BlockSpec((n_pairs, rows, V7X_LANES), lambda b: (0, 0, 0))],
        out_specs=pl.BlockSpec((1, n_pairs, rows, V7X_LANES), lambda b: (b, 0, 0, 0)),
        compiler_params=_params("parallel"),
        name="attention_sample",
    )(q2, cache_k.reshape(n, WINDOW, d_kv), cache_v.reshape(n, WINDOW, d_kv), k_new, v_new, sink2)
    o = jnp.stack([o2[:, :, :Q_PER_KV, :HEAD_DIM], o2[:, :, Q_PER_KV:, HEAD_DIM:]], axis=2)
    return o.reshape(n, d_q).astype(BF16), k_new, v_new


def _silu_gate(gate, val):
    return gate * (1.0 / (1.0 + jnp.exp(-gate))) * val


def _ffn_up_kernel(x_hbm, wg_ref, wv_ref, cwg_ref, cwv_ref, cbg_ref, cbv_ref,
                   p0g_ref, p0v_ref, p1g_ref, p1v_ref,
                   a_ref, sg_ref, sv_ref, hsg_ref, hsv_ref, xbuf_ref, sem, h_ref, tail_ref,
                   *, halves, n_dec):
    half = pl.program_id(0) % halves
    j = pl.program_id(1)
    tn = wg_ref.shape[1]
    tail_rows = tail_ref.shape[1]
    x_ref = _resident_row_tile(x_hbm, xbuf_ref, sem, row_tiles=halves)
    tm = x_ref.shape[0]
    gate_cols, val_cols = slice(0, tn), slice(tn, 2 * tn)

    @pl.when((pl.program_id(0) == 0) & (j == 0))
    def _():
        tail_ref[...] = jnp.zeros(tail_ref.shape, tail_ref.dtype)

    first_tile = jnp.full((tail_rows, 2 * tn), half, jnp.int32) == 0
    h_ref[0:tail_rows] = jnp.where(first_tile, 0.0, tail_ref[j])

    def conv(r0, r1, cols, cw_ref, cb_ref):
        ext = h_ref[r0:r1 + tail_rows, cols]
        h1 = pltpu.roll(ext, 1, 0)[tail_rows:]
        h2 = pltpu.roll(ext, 2, 0)[tail_rows:]
        return cb_ref[...] + cw_ref[0:1] * h2 + cw_ref[1:2] * h1 + cw_ref[2:3] * ext[tail_rows:]

    def epilogue(r0, r1):
        a_ref[0, r0:r1] = _silu_gate(conv(r0, r1, gate_cols, cwg_ref, cbg_ref),
                                     conv(r0, r1, val_cols, cwv_ref, cbv_ref)).astype(a_ref.dtype)

    pending = None
    for r0, r1 in _row_chunks(tm):
        x = x_ref[r0:r1]
        h_ref[tail_rows + r0:tail_rows + r1, gate_cols] = _dot_bf16(x, wg_ref[...])
        h_ref[tail_rows + r0:tail_rows + r1, val_cols] = _dot_bf16(x, wv_ref[...])
        if pending is not None:
            epilogue(*pending)
        pending = (r0, r1)
    epilogue(*pending)
    tail_ref[j] = h_ref[tm:tm + tail_rows]

    h_sample = h_ref[tail_rows + tm - n_dec:tail_rows + tm]
    hsg_ref[0, 0] = h_sample[:, gate_cols]
    hsv_ref[0, 0] = h_sample[:, val_cols]
    seq_tail = h_ref[tm - n_dec:tm - n_dec + tail_rows]
    sg_ref[0, 0] = seq_tail[:, gate_cols]
    sv_ref[0, 0] = seq_tail[:, val_cols]

    @pl.when(half == halves - 1)
    def _():
        def conv_sample(cols, p0_ref, p1_ref, cw_ref, cb_ref):
            return (cb_ref[...] + cw_ref[0:1] * p0_ref[...] + cw_ref[1:2] * p1_ref[...]
                    + cw_ref[2:3] * h_sample[:, cols])

        a_ref[0, tm - n_dec:tm] = _silu_gate(
            conv_sample(gate_cols, p0g_ref, p1g_ref, cwg_ref, cbg_ref),
            conv_sample(val_cols, p0v_ref, p1v_ref, cwv_ref, cbv_ref)).astype(a_ref.dtype)


def ffn_up(h, w_up, conv_w, conv_b, state, *, layer, seq_len, halves, tn, tail_rows):
    n_seq, rows, k = h.shape
    n_dec = rows - seq_len
    d_ff = w_up.shape[2] // 2
    nj = d_ff // tn
    tm = rows // halves
    conv_b = conv_b.reshape(conv_b.shape[0], 1, -1)
    p0, p1 = state[:, 0], state[:, 1]

    def gate(shape):
        return pl.BlockSpec(shape, lambda i, j: (0, j))

    def val(shape):
        return pl.BlockSpec(shape, lambda i, j: (0, j + nj))

    def gate_l(shape):
        return pl.BlockSpec((None,) + shape, lambda i, j: (layer, 0, j))

    def val_l(shape):
        return pl.BlockSpec((None,) + shape, lambda i, j: (layer, 0, j + nj))

    per_tile = lambda r: pl.BlockSpec((1, 1, r, tn), lambda i, j: (i // halves, i % halves, 0, j))
    a, sg, sv, hsg, hsv = pl.pallas_call(
        functools.partial(_ffn_up_kernel, halves=halves, n_dec=n_dec),
        out_shape=(jax.ShapeDtypeStruct((n_seq, rows, d_ff), BF16),
                   jax.ShapeDtypeStruct((n_seq, halves, tail_rows, d_ff), F32),
                   jax.ShapeDtypeStruct((n_seq, halves, tail_rows, d_ff), F32),
                   jax.ShapeDtypeStruct((n_seq, halves, n_dec, d_ff), F32),
                   jax.ShapeDtypeStruct((n_seq, halves, n_dec, d_ff), F32)),
        grid=(n_seq * halves, nj),
        in_specs=[pl.BlockSpec(memory_space=pl.ANY),
                  gate_l((k, tn)), val_l((k, tn)), gate_l((CONV_W, tn)), val_l((CONV_W, tn)),
                  gate_l((1, tn)), val_l((1, tn)),
                  gate((n_dec, tn)), val((n_dec, tn)), gate((n_dec, tn)), val((n_dec, tn))],
        out_specs=(pl.BlockSpec((1, tm, tn), lambda i, j: (i // halves, i % halves, j)),
                   per_tile(tail_rows), per_tile(tail_rows), per_tile(n_dec), per_tile(n_dec)),
        scratch_shapes=[pltpu.VMEM((2, tm, k), h.dtype), pltpu.SemaphoreType.DMA((2,)),
                        pltpu.VMEM((tail_rows + tm, 2 * tn), F32),
                        pltpu.VMEM((nj, tail_rows, 2 * tn), F32)],
        compiler_params=_params("arbitrary", "arbitrary"),
        name="ffn_up",
    )(h, w_up, w_up, conv_w, conv_w, conv_b, conv_b, p0, p0, p1, p1)
    keep = slice(tail_rows - (CONV_W - 1), tail_rows)
    state_prompt = jnp.concatenate([sg[:, halves - 1, keep], sv[:, halves - 1, keep]], axis=-1)
    h_sample = jnp.concatenate([hsg[0, halves - 1], hsv[0, halves - 1]], axis=-1)
    return a, state_prompt, h_sample


ROW_TILES = 2
TN = 512
TN_FFN_UP = V7X_MXU_COLS
TR_PROMPT = 256
TILED_ROW_BLOCKS = 10
FFN_DOWN_ROW_BLOCKS = 5
FFN_DOWN_WEIGHT_PARTS = 4
STATE_TAIL_ROWS = V7X_SUBLANES


def kernel(x_prompt, x_sample, cache_win_k, cache_win_v, state_conv, norm_mix_pre, norm_mix_post,
           norm_ffn_pre, norm_ffn_post, gmlp_w_in, gmlp_ln_g, gmlp_ln_b, gmlp_w_s, gmlp_b_s,
           gmlp_w_out, attn_w_qkv, attn_b_qkv, attn_sinks, attn_w_o, attn_b_o, ffn_w_up,
           ffn_conv_w, ffn_conv_b, ffn_w_down):
    n_seq, seq_len, d = x_prompt.shape
    n_dec = x_sample.shape[0]
    depth = norm_mix_pre.shape[0]
    rows = seq_len + n_dec
    x_sample = x_sample.reshape(1, n_dec, d)
    mm = functools.partial(matmul, row_tiles=ROW_TILES, tn=TN)

    h = rmsnorm_first(x_prompt, x_sample, norm_mix_pre[0], tr=TR_PROMPT)
    x = None
    gv_p, gv_s, wk_p, wv_p, wk_s, wv_s, cv_p, cv_s = [], [], [], [], [], [], [], []
    for layer in range(depth):
        idx = layer // 2
        if layer % 2 == 0:
            d_g = gmlp_w_out.shape[1]
            u = mm(h, gmlp_w_in, layer=idx, n_out=d_g, out_dtype=BF16, epilogue=_gelu, name="gmlp_in_u")
            v = mm(h, gmlp_w_in, layer=idx, n_out=d_g, col_block_offset=d_g // TN, epilogue=_gelu,
                   name="gmlp_in_v")
            gated, g_p, g_s = spatial_gate(v, u, gmlp_ln_g[idx], gmlp_ln_b[idx], gmlp_w_s[idx],
                                           gmlp_b_s[idx], seq_len=seq_len)
            gv_p.append(g_p)
            gv_s.append(g_s.reshape(n_dec, 1, d_g))
            y = mm(gated, gmlp_w_out, layer=idx, n_out=d, out_dtype=BF16, name="gmlp_out")
        else:
            d_q = attn_w_o.shape[1]
            d_qkv = attn_w_qkv.shape[2]
            d_kv = (d_qkv - d_q) // 2
            q = mm(h, attn_w_qkv, attn_b_qkv, layer=idx, n_out=d_q, out_dtype=BF16, name="attn_q")
            kv = mm(h, attn_w_qkv, attn_b_qkv, layer=idx, n_out=2 * d_kv, col_block_offset=d_q // TN,
                    name="attn_kv")
            o = attention_prompt(q, kv, attn_sinks[idx], seq_len=seq_len)
            o_s, k_new, v_new = attention_sample(q[0, seq_len:], kv[0, seq_len:], cache_win_k[idx],
                                                 cache_win_v[idx], attn_sinks[idx])
            o = insert_sample_rows(o, o_s, seq_len=seq_len)
            tail = kv[:, seq_len - WINDOW:seq_len]
            wk_p.append(tail[:, :, :d_kv].reshape(n_seq, WINDOW, N_KV_HEADS, HEAD_DIM))
            wv_p.append(tail[:, :, d_kv:].reshape(n_seq, WINDOW, N_KV_HEADS, HEAD_DIM))
            wk_s.append(jnp.concatenate(
                [cache_win_k[idx][:, 1:], k_new.reshape(n_dec, 1, N_KV_HEADS, HEAD_DIM)], axis=1))
            wv_s.append(jnp.concatenate(
                [cache_win_v[idx][:, 1:], v_new.reshape(n_dec, 1, N_KV_HEADS, HEAD_DIM)], axis=1))
            y = mm(o, attn_w_o, attn_b_o, layer=idx, n_out=d, out_dtype=BF16, name="attn_out")
        if x is None:
            x, h = resnorm_first(x_prompt, x_sample, y, norm_mix_post[layer], norm_ffn_pre[layer],
                                 tr=TR_PROMPT)
        else:
            x, h = resnorm_tiled(x, y, norm_mix_post[layer], norm_ffn_pre[layer],
                                 tr=rows // TILED_ROW_BLOCKS)

        a, c_p, hu_s = ffn_up(h, ffn_w_up, ffn_conv_w, ffn_conv_b, state_conv[layer], layer=layer,
                              seq_len=seq_len, halves=ROW_TILES, tn=TN_FFN_UP, tail_rows=STATE_TAIL_ROWS)
        cv_p.append(c_p)
        cv_s.append(jnp.concatenate([state_conv[layer][:, 1:], hu_s[:, None]], axis=1))
        f = matmul_weight_resident(a, ffn_w_down, layer=layer, tm=rows // FFN_DOWN_ROW_BLOCKS, tn=TN,
                                   parts=FFN_DOWN_WEIGHT_PARTS, name="ffn_down")
        if layer + 1 < depth:
            x, h = resnorm_tiled(x, f, norm_ffn_post[layer], norm_mix_pre[layer + 1],
                                 tr=rows // TILED_ROW_BLOCKS)
        else:
            y_prompt, y_sample = resnorm_last(x, f, norm_ffn_post[layer], seq_len=seq_len, n_dec=n_dec,
                                              tr=TR_PROMPT)

    return (y_prompt, y_sample, jnp.stack(gv_p), jnp.stack(gv_s),
            jnp.stack(wk_p), jnp.stack(wv_p), jnp.stack(wk_s), jnp.stack(wv_s),
            jnp.stack(cv_p), jnp.stack(cv_s))
```

```python
import functools
import math

import jax
import jax.numpy as jnp
from jax import lax
from jax.experimental import pallas as pl
from jax.experimental.pallas import tpu as pltpu

F32 = jnp.float32
BF16 = jnp.bfloat16

NORM_EPS = 1e-6
CHUNK = 128
N_GROUPS = 16
HEAD_DIM = 64
N_KV_HEADS = 8
Q_PER_KV = 8
WINDOW = 128
CONV_W = 3
ATTN_SCALE = HEAD_DIM ** -0.5
MASKED_SCORE = -1e30
LOG2_E = math.log2(math.e)

V7X_LANES = 128
V7X_SUBLANES = 8
V7X_MXU_COLS = 256
V7X_SCOPED_VMEM_BYTES = 60000 * 1024

DOT_ROWS = 512


def _params(*semantics):
    return pltpu.CompilerParams(dimension_semantics=semantics,
                                vmem_limit_bytes=V7X_SCOPED_VMEM_BYTES)


def _rms(x, g):
    return x * lax.rsqrt(jnp.mean(x * x, axis=-1, keepdims=True) + NORM_EPS) * g


def _gelu(x):
    return 0.5 * x * (1.0 + lax.erf(x * math.sqrt(0.5)))


def _identity(x):
    return x


def _dot_bf16(x, w):
    return jnp.dot(x, w.astype(BF16), preferred_element_type=F32)


def _row_chunks(rows, size=DOT_ROWS):
    n = max(rows // size, 1)
    return [(c * size, (c + 1) * size if c + 1 < n else rows) for c in range(n)]


def _skip_aliased(body, n_aliased):
    def kernel_fn(*refs):
        body(*refs[n_aliased:])
    return kernel_fn


def _rmsnorm_kernel(x_ref, g_ref, h_ref):
    h_ref[0] = _rms(x_ref[0], g_ref[...]).astype(h_ref.dtype)


def _resnorm_kernel(x_ref, y_ref, gpost_ref, gnext_ref, xo_ref, ho_ref):
    xn = x_ref[0] + _rms(y_ref[0].astype(F32), gpost_ref[...])
    xo_ref[0] = xn
    ho_ref[0] = _rms(xn, gnext_ref[...]).astype(ho_ref.dtype)


def _resnorm_last_kernel(x_ref, y_ref, gpost_ref, xo_ref):
    xo_ref[0] = x_ref[0] + _rms(y_ref[0].astype(F32), gpost_ref[...])


def rmsnorm_first(x_prompt, x_sample, g, *, tr):
    n_seq, seq_len, d = x_prompt.shape
    n_dec = x_sample.shape[1]
    g = g.reshape(1, d)
    shape = jax.ShapeDtypeStruct((n_seq, seq_len + n_dec, d), BF16)
    row = pl.BlockSpec((1, tr, d), lambda s, r: (s, r, 0))
    h = pl.pallas_call(
        _rmsnorm_kernel, out_shape=shape, grid=(n_seq, seq_len // tr),
        in_specs=[row, pl.BlockSpec((1, d), lambda s, r: (0, 0))], out_specs=row,
        compiler_params=_params("parallel", "parallel"), name="rmsnorm_first_prompt",
    )(x_prompt, g)
    return pl.pallas_call(
        _skip_aliased(_rmsnorm_kernel, 1), out_shape=shape, grid=(n_seq,),
        in_specs=[pl.BlockSpec(memory_space=pl.ANY),
                  pl.BlockSpec((1, n_dec, d), lambda s: (0, 0, 0)),
                  pl.BlockSpec((1, d), lambda s: (0, 0))],
        out_specs=pl.BlockSpec((1, n_dec, d), lambda s: (s, seq_len // n_dec, 0)),
        input_output_aliases={0: 0},
        compiler_params=_params("arbitrary"), name="rmsnorm_first_sample",
    )(h, x_sample, g)


def resnorm_first(x_prompt, x_sample, y, g_post, g_next, *, tr):
    n_seq, seq_len, d = x_prompt.shape
    n_dec = x_sample.shape[1]
    g_post, g_next = g_post.reshape(1, d), g_next.reshape(1, d)
    shapes = (jax.ShapeDtypeStruct(y.shape, F32), jax.ShapeDtypeStruct(y.shape, BF16))
    row = pl.BlockSpec((1, tr, d), lambda s, r: (s, r, 0))
    vec = pl.BlockSpec((1, d), lambda s, r: (0, 0))
    xo, ho = pl.pallas_call(
        _resnorm_kernel, out_shape=shapes, grid=(n_seq, seq_len // tr),
        in_specs=[row, row, vec, vec], out_specs=(row, row),
        compiler_params=_params("parallel", "parallel"), name="resnorm_first_prompt",
    )(x_prompt, y, g_post, g_next)
    sample_rows = pl.BlockSpec((1, n_dec, d), lambda s: (s, seq_len // n_dec, 0))
    vec1 = pl.BlockSpec((1, d), lambda s: (0, 0))
    return pl.pallas_call(
        _skip_aliased(_resnorm_kernel, 2), out_shape=shapes, grid=(n_seq,),
        in_specs=[pl.BlockSpec(memory_space=pl.ANY), pl.BlockSpec(memory_space=pl.ANY),
                  pl.BlockSpec((1, n_dec, d), lambda s: (0, 0, 0)),
                  pl.BlockSpec((1, n_dec, d), lambda s: (0, seq_len // n_dec, 0)), vec1, vec1],
        out_specs=(sample_rows, sample_rows),
        input_output_aliases={0: 0, 1: 1},
        compiler_params=_params("arbitrary"), name="resnorm_first_sample",
    )(xo, ho, x_sample, y, g_post, g_next)


def resnorm_tiled(x, y, g_post, g_next, *, tr):
    n_seq, rows, d = x.shape
    row = pl.BlockSpec((1, tr, d), lambda s, r: (s, r, 0))
    vec = pl.BlockSpec((1, d), lambda s, r: (0, 0))
    return pl.pallas_call(
        _resnorm_kernel,
        out_shape=(jax.ShapeDtypeStruct(x.shape, F32), jax.ShapeDtypeStruct(x.shape, BF16)),
        grid=(n_seq, rows // tr),
        in_specs=[row, row, vec, vec], out_specs=(row, row),
        compiler_params=_params("parallel", "parallel"), name="resnorm_tiled",
    )(x, y, g_post.reshape(1, d), g_next.reshape(1, d))


def resnorm_last(x, y, g_post, *, seq_len, n_dec, tr):
    n_seq, _, d = x.shape
    g_post = g_post.reshape(1, d)
    row = pl.BlockSpec((1, tr, d), lambda s, r: (s, r, 0))
    y_prompt = pl.pallas_call(
        _resnorm_last_kernel, out_shape=jax.ShapeDtypeStruct((n_seq, seq_len, d), F32),
        grid=(n_seq, seq_len // tr),
        in_specs=[row, row, pl.BlockSpec((1, d), lambda s, r: (0, 0))], out_specs=row,
        compiler_params=_params("parallel", "parallel"), name="resnorm_last_prompt",
    )(x, y, g_post)
    sample_rows = pl.BlockSpec((1, n_dec, d), lambda s: (0, seq_len // n_dec, 0))
    y_sample = pl.pallas_call(
        _resnorm_last_kernel, out_shape=jax.ShapeDtypeStruct((1, n_dec, d), F32), grid=(1,),
        in_specs=[sample_rows, sample_rows, pl.BlockSpec((1, d), lambda s: (0, 0))],
        out_specs=pl.BlockSpec((1, n_dec, d), lambda s: (0, 0, 0)),
        compiler_params=_params("arbitrary"), name="resnorm_last_sample",
    )(x, y, g_post)
    return y_prompt, y_sample.reshape(n_dec, 1, d)


def _copy_rows_kernel(src_ref, o_ref):
    o_ref[0] = src_ref[...]


def insert_sample_rows(tiled, rows, *, seq_len):
    n_seq, _, c = tiled.shape
    n_dec = rows.shape[0]
    return pl.pallas_call(
        _skip_aliased(_copy_rows_kernel, 1), out_shape=jax.ShapeDtypeStruct(tiled.shape, tiled.dtype),
        grid=(n_seq,),
        in_specs=[pl.BlockSpec(memory_space=pl.ANY), pl.BlockSpec((n_dec, c), lambda s: (0, 0))],
        out_specs=pl.BlockSpec((1, n_dec, c), lambda s: (s, seq_len // n_dec, 0)),
        input_output_aliases={0: 0},
        compiler_params=_params("arbitrary"), name="insert_sample_rows",
    )(tiled, rows)


def _resident_row_tile(x_hbm, xbuf_ref, sem, *, row_tiles):
    i = pl.program_id(0)
    j = pl.program_id(1)
    tm = xbuf_ref.shape[1]

    def tile_copy(t):
        rows = pl.ds(pl.multiple_of((t % row_tiles) * tm, 2 * V7X_SUBLANES), tm)
        return pltpu.make_async_copy(x_hbm.at[t // row_tiles, rows, :], xbuf_ref.at[t % 2], sem.at[t % 2])

    @pl.when(j == 0)
    def _():
        @pl.when(i == 0)
        def _():
            tile_copy(0).start()

        @pl.when(i + 1 < pl.num_programs(0))
        def _():
            tile_copy(i + 1).start()

        tile_copy(i).wait()

    return xbuf_ref.at[i % 2]


def _matmul_kernel(*refs, epilogue, has_bias, row_tiles):
    if has_bias:
        x_hbm, w_ref, b_ref, o_ref, xbuf_ref, sem = refs
    else:
        x_hbm, w_ref, o_ref, xbuf_ref, sem = refs
    x_ref = _resident_row_tile(x_hbm, xbuf_ref, sem, row_tiles=row_tiles)
    for r0, r1 in _row_chunks(x_ref.shape[0]):
        acc = _dot_bf16(x_ref[r0:r1], w_ref[...])
        if has_bias:
            acc = acc + b_ref[...]
        o_ref[0, r0:r1] = epilogue(acc).astype(o_ref.dtype)


def matmul(x, w, bias=None, *, layer, row_tiles, tn, n_out, col_block_offset=0, out_dtype=F32,
           epilogue=_identity, name):
    n_seq, rows, k = x.shape
    tm = rows // row_tiles
    in_specs = [pl.BlockSpec(memory_space=pl.ANY),
                pl.BlockSpec((None, k, tn), lambda i, j: (layer, 0, j + col_block_offset))]
    args = [x, w]
    if bias is not None:
        in_specs.append(pl.BlockSpec((None, 1, tn), lambda i, j: (layer, 0, j + col_block_offset)))
        args.append(bias.reshape(bias.shape[0], 1, -1))
    return pl.pallas_call(
        functools.partial(_matmul_kernel, epilogue=epilogue, has_bias=bias is not None,
                          row_tiles=row_tiles),
        out_shape=jax.ShapeDtypeStruct((n_seq, rows, n_out), out_dtype),
        grid=(n_seq * row_tiles, n_out // tn),
        in_specs=in_specs,
        out_specs=pl.BlockSpec((1, tm, tn), lambda i, j: (i // row_tiles, i % row_tiles, j)),
        scratch_shapes=[pltpu.VMEM((2, tm, k), x.dtype), pltpu.SemaphoreType.DMA((2,))],
        compiler_params=_params("arbitrary", "arbitrary"),
        name=name,
    )(*args)


def _matmul_wres_kernel(x_ref, w_hbm, o_ref, wb_ref, stage_ref, sem, *, layer, parts, steps_per_part):
    j = pl.program_id(0)
    i = pl.program_id(1)
    n_col_tiles = pl.num_programs(0)
    _, k, tn = wb_ref.shape
    kp = k // parts

    def part_copy(col_tile, part):
        rows = pl.ds(pl.multiple_of(part * kp, V7X_SUBLANES), kp)
        cols = pl.ds(pl.multiple_of(col_tile * tn, V7X_LANES), tn)
        return pltpu.make_async_copy(w_hbm.at[layer, rows, cols], stage_ref, sem.at[0])

    def round_part(col_tile, part):
        rows = pl.ds(pl.multiple_of(part * kp, 2 * V7X_SUBLANES), kp)
        wb_ref[col_tile % 2, rows, :] = stage_ref[...].astype(BF16)

    @pl.when((j == 0) & (i == 0))
    def _():
        for part in range(parts):
            part_copy(0, part).start()
            part_copy(0, part).wait()
            round_part(0, part)

    part = i // steps_per_part
    phase = i % steps_per_part
    prefetching = (j + 1 < n_col_tiles) & (part < parts)

    @pl.when(prefetching & (phase == 0))
    def _():
        part_copy(j + 1, part).start()

    @pl.when(prefetching & (phase == steps_per_part - 2))
    def _():
        part_copy(j + 1, part).wait()
        round_part(j + 1, part)

    o_ref[0] = jnp.dot(x_ref[0], wb_ref[j % 2], preferred_element_type=F32).astype(o_ref.dtype)


def matmul_weight_resident(x, w, *, layer, tm, tn, parts, name):
    n_seq, rows, k = x.shape
    n = w.shape[2]
    per_seq = rows // tm
    n_row_tiles = n_seq * per_seq
    steps_per_part = n_row_tiles // parts
    assert steps_per_part >= 2 and k % (parts * 2 * V7X_SUBLANES) == 0
    return pl.pallas_call(
        functools.partial(_matmul_wres_kernel, layer=layer, parts=parts, steps_per_part=steps_per_part),
        out_shape=jax.ShapeDtypeStruct((n_seq, rows, n), BF16),
        grid=(n // tn, n_row_tiles),
        in_specs=[pl.BlockSpec((1, tm, k), lambda j, i: (i // per_seq, i % per_seq, 0)),
                  pl.BlockSpec(memory_space=pl.ANY)],
        out_specs=pl.BlockSpec((1, tm, tn), lambda j, i: (i // per_seq, i % per_seq, j)),
        scratch_shapes=[pltpu.VMEM((2, k, tn), BF16), pltpu.VMEM((k // parts, tn), F32),
                        pltpu.SemaphoreType.DMA((1,))],
        compiler_params=_params("arbitrary", "arbitrary"),
        name=name,
    )(x, w)


def _layer_norm(v, g, b):
    xc = v - jnp.mean(v, axis=-1, keepdims=True)
    return xc * lax.rsqrt(jnp.mean(xc * xc, axis=-1, keepdims=True) + NORM_EPS) * g + b


def _spatial_kernel(v_ref, u_ref, lng_ref, lnb_ref, ws_ref, bias_ref, o_ref, gv_ref, *, group_dim):
    t = lax.broadcasted_iota(jnp.int32, (CHUNK, CHUNK), 0)
    s = lax.broadcasted_iota(jnp.int32, (CHUNK, CHUNK), 1)
    causal = s <= t
    for c in range(v_ref.shape[1] // CHUNK):
        rows = slice(c * CHUNK, (c + 1) * CHUNK)
        vn = _layer_norm(v_ref[0, rows], lng_ref[...], lnb_ref[...])
        gv_ref[0] = vn
        vb = vn.astype(BF16)
        for g in range(N_GROUPS):
            cols = slice(g * group_dim, (g + 1) * group_dim)
            wc = jnp.where(causal, ws_ref[g], 0.0).astype(BF16)
            mix = jnp.dot(wc, vb[:, cols], preferred_element_type=F32) + bias_ref[:, cols]
            o_ref[0, rows, cols] = (u_ref[0, rows, cols].astype(F32) * mix).astype(o_ref.dtype)


def _spatial_sample_kernel(v_ref, u_ref, lng_ref, lnb_ref, w00_ref, b0_ref, o_ref, gv_ref):
    vn = _layer_norm(v_ref[0], lng_ref[...], lnb_ref[...])
    gv_ref[0] = vn
    mix = w00_ref[...].astype(BF16).astype(F32) * vn.astype(BF16).astype(F32) + b0_ref[...]
    o_ref[0] = (u_ref[0].astype(F32) * mix).astype(o_ref.dtype)


def spatial_gate(v, u, ln_g, ln_b, w_s, b_s, *, seq_len):
    n_seq, rows, d = v.shape
    n_dec = rows - seq_len
    group_dim = d // N_GROUPS
    ln_g, ln_b = ln_g.reshape(1, d), ln_b.reshape(1, d)
    bias_full = jnp.repeat(b_s.T, group_dim, axis=1)
    block_rows = SPATIAL_CHUNKS_PER_STEP * CHUNK
    row = pl.BlockSpec((1, block_rows, d), lambda s, c: (s, c, 0))
    vec = pl.BlockSpec((1, d), lambda s, c: (0, 0))
    gated, gv_prompt = pl.pallas_call(
        functools.partial(_spatial_kernel, group_dim=group_dim),
        out_shape=(jax.ShapeDtypeStruct(v.shape, BF16),
                   jax.ShapeDtypeStruct((n_seq, CHUNK, d), F32)),
        grid=(n_seq, seq_len // block_rows),
        in_specs=[row, row, vec, vec,
                  pl.BlockSpec((N_GROUPS, CHUNK, CHUNK), lambda s, c: (0, 0, 0)),
                  pl.BlockSpec((CHUNK, d), lambda s, c: (0, 0))],
        out_specs=(row, pl.BlockSpec((1, CHUNK, d), lambda s, c: (s, 0, 0))),
        compiler_params=_params("parallel", "arbitrary"),
        name="spatial_gate_prompt",
    )(v, u, ln_g, ln_b, w_s, bias_full)
    w00 = jnp.repeat(w_s[:, 0, 0], group_dim).reshape(1, d)
    b0 = jnp.repeat(b_s[:, 0], group_dim).reshape(1, d)
    tile0_rows = pl.BlockSpec((1, n_dec, d), lambda s: (0, seq_len // n_dec, 0))
    vec1 = pl.BlockSpec((1, d), lambda s: (0, 0))
    gated, gv_sample = pl.pallas_call(
        _skip_aliased(_spatial_sample_kernel, 1),
        out_shape=(jax.ShapeDtypeStruct(v.shape, BF16), jax.ShapeDtypeStruct((1, n_dec, d), F32)),
        grid=(n_seq,),
        in_specs=[pl.BlockSpec(memory_space=pl.ANY), tile0_rows, tile0_rows, vec1, vec1, vec1, vec1],
        out_specs=(pl.BlockSpec((1, n_dec, d), lambda s: (s, seq_len // n_dec, 0)),
                   pl.BlockSpec((1, n_dec, d), lambda s: (0, 0, 0))),
        input_output_aliases={0: 0},
        compiler_params=_params("arbitrary"),
        name="spatial_gate_sample",
    )(gated, v, u, ln_g, ln_b, w00, b0)
    return gated, gv_prompt, gv_sample


def _pair_block_diag(pair, even):
    lane = lax.broadcasted_iota(jnp.int32, pair.shape, 1)
    if even:
        own = jnp.where(lane < HEAD_DIM, pair, 0.0)
        return jnp.concatenate([own, pltpu.roll(own, HEAD_DIM, 1)], axis=0)
    own = jnp.where(lane >= HEAD_DIM, pair, 0.0)
    return jnp.concatenate([pltpu.roll(own, HEAD_DIM, 1), own], axis=0)


def _attn_prompt_kernel(sink_ref, q_ref, kprev_ref, kown_ref, vprev_ref, vown_ref, o_ref):
    blk = pl.program_id(1)
    n_keys = 2 * WINDOW
    qi = lax.broadcasted_iota(jnp.int32, (WINDOW, n_keys), 0)
    kj = lax.broadcasted_iota(jnp.int32, (WINDOW, n_keys), 1)
    first_key = jnp.where(blk > 0, 0, WINDOW)
    valid = (kj >= qi) & (kj <= qi + WINDOW) & (kj >= first_key)
    col_blocks = Q_PER_KV * HEAD_DIM // V7X_LANES
    valid = jnp.concatenate([valid] * col_blocks, axis=0)
    block_of_row = lax.broadcasted_iota(jnp.int32, (col_blocks * WINDOW, 1), 0) // WINDOW
    head_lane = lax.broadcasted_iota(jnp.int32, (col_blocks * WINDOW, V7X_LANES), 1)
    group_w = Q_PER_KV * HEAD_DIM
    for pair in range(N_KV_HEADS // 2):
        lanes = slice(pair * V7X_LANES, (pair + 1) * V7X_LANES)
        k_pair = jnp.concatenate([kprev_ref[0, :, lanes], kown_ref[0, :, lanes]], axis=0)
        v_pair = jnp.concatenate([vprev_ref[0, :, lanes], vown_ref[0, :, lanes]], axis=0)
        for e in range(2):
            h = 2 * pair + e
            k2 = _pair_block_diag(k_pair, e == 0).astype(BF16)
            v2 = _pair_block_diag(v_pair, e == 0).astype(BF16)
            qs = jnp.concatenate(
                [q_ref[0, :, h * group_w + c * V7X_LANES: h * group_w + (c + 1) * V7X_LANES]
                 for c in range(col_blocks)], axis=0)
            qs = (qs.astype(F32) * (ATTN_SCALE * LOG2_E)).astype(BF16)
            s = lax.dot_general(qs, k2, (((1,), (1,)), ((), ())), preferred_element_type=F32)
            probs, inv_denoms = [], []
            for half in range(2):
                sh = jnp.where(valid, s[:, half * n_keys:(half + 1) * n_keys], MASKED_SCORE)
                sink = jnp.zeros((col_blocks * WINDOW, 1), F32)
                for c in range(col_blocks):
                    sink = jnp.where(block_of_row == c, sink_ref[h * Q_PER_KV + 2 * c + half] * LOG2_E,
                                     sink)
                m = jnp.maximum(jnp.max(sh, axis=-1, keepdims=True), sink)
                p = jnp.exp2(sh - m)
                inv_denoms.append(1.0 / (jnp.sum(p, axis=-1, keepdims=True) + jnp.exp2(sink - m)))
                probs.append(p.astype(BF16))
            o = jnp.dot(jnp.concatenate(probs, axis=1), v2, preferred_element_type=F32)
            o = o * jnp.where(head_lane < HEAD_DIM, inv_denoms[0], inv_denoms[1])
            for c in range(col_blocks):
                o_ref[0, :, h * group_w + c * V7X_LANES: h * group_w + (c + 1) * V7X_LANES] = (
                    o[c * WINDOW:(c + 1) * WINDOW].astype(o_ref.dtype))


def attention_prompt(q, kv, sinks, *, seq_len):
    n_seq, rows, d_q = q.shape
    d_kv = N_KV_HEADS * HEAD_DIM
    k_col, v_col = 0, 1

    def own(col):
        return lambda s, i: (s, i, col)

    def prev(col):
        return lambda s, i: (s, jnp.maximum(i - 1, 0), col)

    return pl.pallas_call(
        _attn_prompt_kernel,
        out_shape=jax.ShapeDtypeStruct((n_seq, rows, d_q), BF16),
        grid=(n_seq, seq_len // WINDOW),
        in_specs=[pl.BlockSpec(memory_space=pltpu.SMEM),
                  pl.BlockSpec((1, WINDOW, d_q), own(0)),
                  pl.BlockSpec((1, WINDOW, d_kv), prev(k_col)),
                  pl.BlockSpec((1, WINDOW, d_kv), own(k_col)),
                  pl.BlockSpec((1, WINDOW, d_kv), prev(v_col)),
                  pl.BlockSpec((1, WINDOW, d_kv), own(v_col))],
        out_specs=pl.BlockSpec((1, WINDOW, d_q), own(0)),
        compiler_params=_params("parallel", "arbitrary"),
        name="attention_prompt",
    )(sinks, q, kv, kv, kv, kv)


def _attn_sample_kernel(q2_ref, ck_ref, cv_ref, knew_ref, vnew_ref, sink_ref, o_ref):
    for pair in range(N_KV_HEADS // 2):
        lanes = slice(pair * V7X_LANES, (pair + 1) * V7X_LANES)
        q2 = q2_ref[0, pair]
        k_pair = ck_ref[0, :, lanes].astype(BF16)
        v_pair = cv_ref[0, :, lanes].astype(BF16)
        k_new = knew_ref[0, :, lanes].astype(BF16).astype(F32)
        v_new = vnew_ref[0, :, lanes].astype(BF16).astype(F32)
        sink = sink_ref[pair][:, :1]
        s = lax.dot_general(q2, k_pair, (((1,), (1,)), ((), ())), preferred_element_type=F32)
        s_new = jnp.sum(q2.astype(F32) * k_new, axis=-1, keepdims=True)
        m = jnp.maximum(jnp.maximum(jnp.max(s, axis=-1, keepdims=True), s_new), sink)
        p = jnp.exp(s - m)
        p_new = jnp.exp(s_new - m)
        denom = jnp.sum(p, axis=-1, keepdims=True) + p_new + jnp.exp(sink - m)
        o = jnp.dot((p / denom).astype(BF16), v_pair, preferred_element_type=F32)
        o_ref[0, pair] = o + (p_new / denom).astype(BF16).astype(F32) * v_new


def attention_sample(q, kv, cache_k, cache_v, sinks):
    n, d_q = q.shape
    d_kv = N_KV_HEADS * HEAD_DIM
    n_pairs = N_KV_HEADS // 2
    rows = 2 * Q_PER_KV
    q = (q.astype(F32) * ATTN_SCALE).astype(BF16).reshape(n, n_pairs, 2, Q_PER_KV, HEAD_DIM)
    zeros = jnp.zeros((n, n_pairs, Q_PER_KV, HEAD_DIM), BF16)
    q2 = jnp.concatenate([jnp.concatenate([q[:, :, 0], zeros], axis=-1),
                          jnp.concatenate([zeros, q[:, :, 1]], axis=-1)], axis=2)
    k_new = kv[:, :d_kv].reshape(n, 1, d_kv)
    v_new = kv[:, d_kv:].reshape(n, 1, d_kv)
    sink2 = jnp.broadcast_to(sinks.reshape(n_pairs, rows, 1), (n_pairs, rows, V7X_LANES))
    o2 = pl.pallas_call(
        _attn_sample_kernel,
        out_shape=jax.ShapeDtypeStruct((n, n_pairs, rows, V7X_LANES), F32),
        grid=(n,),
        in_specs=[pl.BlockSpec((1, n_pairs, rows, V7X_LANES), lambda b: (b, 0, 0, 0)),
                  pl.BlockSpec((1, WINDOW, d_kv), lambda b: (b, 0, 0)),
                  pl.BlockSpec((1, WINDOW, d_kv), lambda b: (b, 0, 0)),
                  pl.BlockSpec((1, 1, d_kv), lambda b: (b, 0, 0)),
                  pl.BlockSpec((1, 1, d_kv), lambda b: (b, 0, 0)),
                  pl.BlockSpec((n_pairs, rows, V7X_LANES), lambda b: (0, 0, 0))],
        out_specs=pl.BlockSpec((1, n_pairs, rows, V7X_LANES), lambda b: (b, 0, 0, 0)),
        compiler_params=_params("parallel"),
        name="attention_sample",
    )(q2, cache_k.reshape(n, WINDOW, d_kv), cache_v.reshape(n, WINDOW, d_kv), k_new, v_new, sink2)
    o = jnp.stack([o2[:, :, :Q_PER_KV, :HEAD_DIM], o2[:, :, Q_PER_KV:, HEAD_DIM:]], axis=2)
    return o.reshape(n, d_q).astype(BF16), k_new, v_new


def _silu_gate(gate, val):
    return gate * (1.0 / (1.0 + jnp.exp(-gate))) * val


def _ffn_up_kernel(x_hbm, wg_ref, wv_ref, cwg_ref, cwv_ref, cbg_ref, cbv_ref,
                   p0g_ref, p0v_ref, p1g_ref, p1v_ref,
                   a_ref, sg_ref, sv_ref, hsg_ref, hsv_ref, xbuf_ref, sem, h_ref, tail_ref,
                   *, halves, n_dec):
    half = pl.program_id(0) % halves
    j = pl.program_id(1)
    tn = wg_ref.shape[1]
    tail_rows = tail_ref.shape[1]
    x_ref = _resident_row_tile(x_hbm, xbuf_ref, sem, row_tiles=halves)
    tm = x_ref.shape[0]
    gate_cols, val_cols = slice(0, tn), slice(tn, 2 * tn)

    @pl.when((pl.program_id(0) == 0) & (j == 0))
    def _():
        tail_ref[...] = jnp.zeros(tail_ref.shape, tail_ref.dtype)

    first_tile = jnp.full((tail_rows, 2 * tn), half, jnp.int32) == 0
    h_ref[0:tail_rows] = jnp.where(first_tile, 0.0, tail_ref[j])

    def conv(r0, r1, cols, cw_ref, cb_ref):
        ext = h_ref[r0:r1 + tail_rows, cols]
        h1 = pltpu.roll(ext, 1, 0)[tail_rows:]
        h2 = pltpu.roll(ext, 2, 0)[tail_rows:]
        return cb_ref[...] + cw_ref[0:1] * h2 + cw_ref[1:2] * h1 + cw_ref[2:3] * ext[tail_rows:]

    def epilogue(r0, r1):
        a_ref[0, r0:r1] = _silu_gate(conv(r0, r1, gate_cols, cwg_ref, cbg_ref),
                                     conv(r0, r1, val_cols, cwv_ref, cbv_ref)).astype(a_ref.dtype)

    pending = None
    for r0, r1 in _row_chunks(tm):
        x = x_ref[r0:r1]
        h_ref[tail_rows + r0:tail_rows + r1, gate_cols] = _dot_bf16(x, wg_ref[...])
        h_ref[tail_rows + r0:tail_rows + r1, val_cols] = _dot_bf16(x, wv_ref[...])
        if pending is not None:
            epilogue(*pending)
        pending = (r0, r1)
    epilogue(*pending)
    tail_ref[j] = h_ref[tm:tm + tail_rows]

    h_sample = h_ref[tail_rows + tm - n_dec:tail_rows + tm]
    hsg_ref[0, 0] = h_sample[:, gate_cols]
    hsv_ref[0, 0] = h_sample[:, val_cols]
    seq_tail = h_ref[tm - n_dec:tm - n_dec + tail_rows]
    sg_ref[0, 0] = seq_tail[:, gate_cols]
    sv_ref[0, 0] = seq_tail[:, val_cols]

    @pl.when(half == halves - 1)
    def _():
        def conv_sample(cols, p0_ref, p1_ref, cw_ref, cb_ref):
            return (cb_ref[...] + cw_ref[0:1] * p0_ref[...] + cw_ref[1:2] * p1_ref[...]
                    + cw_ref[2:3] * h_sample[:, cols])

        a_ref[0, tm - n_dec:tm] = _silu_gate(
            conv_sample(gate_cols, p0g_ref, p1g_ref, cwg_ref, cbg_ref),
            conv_sample(val_cols, p0v_ref, p1v_ref, cwv_ref, cbv_ref)).astype(a_ref.dtype)


def ffn_up(h, w_up, conv_w, conv_b, state, *, layer, seq_len, halves, tn, tail_rows):
    n_seq, rows, k = h.shape
    n_dec = rows - seq_len
    d_ff = w_up.shape[2] // 2
    nj = d_ff // tn
    tm = rows // halves
    conv_b = conv_b.reshape(conv_b.shape[0], 1, -1)
    p0, p1 = state[:, 0], state[:, 1]

    def gate(shape):
        return pl.BlockSpec(shape, lambda i, j: (0, j))

    def val(shape):
        return pl.BlockSpec(shape, lambda i, j: (0, j + nj))

    def gate_l(shape):
        return pl.BlockSpec((None,) + shape, lambda i, j: (layer, 0, j))

    def val_l(shape):
        return pl.BlockSpec((None,) + shape, lambda i, j: (layer, 0, j + nj))

    per_tile = lambda r: pl.BlockSpec((1, 1, r, tn), lambda i, j: (i // halves, i % halves, 0, j))
    a, sg, sv, hsg, hsv = pl.pallas_call(
        functools.partial(_ffn_up_kernel, halves=halves, n_dec=n_dec),
        out_shape=(jax.ShapeDtypeStruct((n_seq, rows, d_ff), BF16),
                   jax.ShapeDtypeStruct((n_seq, halves, tail_rows, d_ff), F32),
                   jax.ShapeDtypeStruct((n_seq, halves, tail_rows, d_ff), F32),
                   jax.ShapeDtypeStruct((n_seq, halves, n_dec, d_ff), F32),
                   jax.ShapeDtypeStruct((n_seq, halves, n_dec, d_ff), F32)),
        grid=(n_seq * halves, nj),
        in_specs=[pl.BlockSpec(memory_space=pl.ANY),
                  gate_l((k, tn)), val_l((k, tn)), gate_l((CONV_W, tn)), val_l((CONV_W, tn)),
                  gate_l((1, tn)), val_l((1, tn)),
                  gate((n_dec, tn)), val((n_dec, tn)), gate((n_dec, tn)), val((n_dec, tn))],
        out_specs=(pl.BlockSpec((1, tm, tn), lambda i, j: (i // halves, i % halves, j)),
                   per_tile(tail_rows), per_tile(tail_rows), per_tile(n_dec), per_tile(n_dec)),
        scratch_shapes=[pltpu.VMEM((2, tm, k), h.dtype), pltpu.SemaphoreType.DMA((2,)),
                        pltpu.VMEM((tail_rows + tm, 2 * tn), F32),
                        pltpu.VMEM((nj, tail_rows, 2 * tn), F32)],
        compiler_params=_params("arbitrary", "arbitrary"),
        name="ffn_up",
    )(h, w_up, w_up, conv_w, conv_w, conv_b, conv_b, p0, p0, p1, p1)
    keep = slice(tail_rows - (CONV_W - 1), tail_rows)
    state_prompt = jnp.concatenate([sg[:, halves - 1, keep], sv[:, halves - 1, keep]], axis=-1)
    h_sample = jnp.concatenate([hsg[0, halves - 1], hsv[0, halves - 1]], axis=-1)
    return a, state_prompt, h_sample


ROW_TILES = 2
TN = 512
TN_FFN_UP = V7X_MXU_COLS
TR_PROMPT = 512
TILED_ROW_BLOCKS = 5
FFN_DOWN_ROW_BLOCKS = 5
FFN_DOWN_WEIGHT_PARTS = 4
STATE_TAIL_ROWS = V7X_SUBLANES
SPATIAL_CHUNKS_PER_STEP = 2


def kernel(x_prompt, x_sample, cache_win_k, cache_win_v, state_conv, norm_mix_pre, norm_mix_post,
           norm_ffn_pre, norm_ffn_post, gmlp_w_in, gmlp_ln_g, gmlp_ln_b, gmlp_w_s, gmlp_b_s,
           gmlp_w_out, attn_w_qkv, attn_b_qkv, attn_sinks, attn_w_o, attn_b_o, ffn_w_up,
           ffn_conv_w, ffn_conv_b, ffn_w_down):
    n_seq, seq_len, d = x_prompt.shape
    n_dec = x_sample.shape[0]
    depth = norm_mix_pre.shape[0]
    rows = seq_len + n_dec
    x_sample = x_sample.reshape(1, n_dec, d)
    mm = functools.partial(matmul, row_tiles=ROW_TILES, tn=TN)

    h = rmsnorm_first(x_prompt, x_sample, norm_mix_pre[0], tr=TR_PROMPT)
    x = None
    gv_p, gv_s, wk_p, wv_p, wk_s, wv_s, cv_p, cv_s = [], [], [], [], [], [], [], []
    for layer in range(depth):
        idx = layer // 2
        if layer % 2 == 0:
            d_g = gmlp_w_out.shape[1]
            u = mm(h, gmlp_w_in, layer=idx, n_out=d_g, out_dtype=BF16, epilogue=_gelu, name="gmlp_in_u")
            v = mm(h, gmlp_w_in, layer=idx, n_out=d_g, col_block_offset=d_g // TN, epilogue=_gelu,
                   name="gmlp_in_v")
            gated, g_p, g_s = spatial_gate(v, u, gmlp_ln_g[idx], gmlp_ln_b[idx], gmlp_w_s[idx],
                                           gmlp_b_s[idx], seq_len=seq_len)
            gv_p.append(g_p)
            gv_s.append(g_s.reshape(n_dec, 1, d_g))
            y = mm(gated, gmlp_w_out, layer=idx, n_out=d, out_dtype=BF16, name="gmlp_out")
        else:
            d_q = attn_w_o.shape[1]
            d_qkv = attn_w_qkv.shape[2]
            d_kv = (d_qkv - d_q) // 2
            q = mm(h, attn_w_qkv, attn_b_qkv, layer=idx, n_out=d_q, out_dtype=BF16, name="attn_q")
            kv = mm(h, attn_w_qkv, attn_b_qkv, layer=idx, n_out=2 * d_kv, col_block_offset=d_q // TN,
                    name="attn_kv")
            o = attention_prompt(q, kv, attn_sinks[idx], seq_len=seq_len)
            o_s, k_new, v_new = attention_sample(q[0, seq_len:], kv[0, seq_len:], cache_win_k[idx],
                                                 cache_win_v[idx], attn_sinks[idx])
            o = insert_sample_rows(o, o_s, seq_len=seq_len)
            tail = kv[:, seq_len - WINDOW:seq_len]
            wk_p.append(tail[:, :, :d_kv].reshape(n_seq, WINDOW, N_KV_HEADS, HEAD_DIM))
            wv_p.append(tail[:, :, d_kv:].reshape(n_seq, WINDOW, N_KV_HEADS, HEAD_DIM))
            wk_s.append(jnp.concatenate(
                [cache_win_k[idx][:, 1:], k_new.reshape(n_dec, 1, N_KV_HEADS, HEAD_DIM)], axis=1))
            wv_s.append(jnp.concatenate(
                [cache_win_v[idx][:, 1:], v_new.reshape(n_dec, 1, N_KV_HEADS, HEAD_DIM)], axis=1))
            y = mm(o, attn_w_o, attn_b_o, layer=idx, n_out=d, out_dtype=BF16, name="attn_out")
        if x is None:
            x, h = resnorm_first(x_prompt, x_sample, y, norm_mix_post[layer], norm_ffn_pre[layer],
                                 tr=TR_PROMPT)
        else:
            x, h = resnorm_tiled(x, y, norm_mix_post[layer], norm_ffn_pre[layer],
                                 tr=rows // TILED_ROW_BLOCKS)

        a, c_p, hu_s = ffn_up(h, ffn_w_up, ffn_conv_w, ffn_conv_b, state_conv[layer], layer=layer,
                              seq_len=seq_len, halves=ROW_TILES, tn=TN_FFN_UP, tail_rows=STATE_TAIL_ROWS)
        cv_p.append(c_p)
        cv_s.append(jnp.concatenate([state_conv[layer][:, 1:], hu_s[:, None]], axis=1))
        f = matmul_weight_resident(a, ffn_w_down, layer=layer, tm=rows // FFN_DOWN_ROW_BLOCKS, tn=TN,
                                   parts=FFN_DOWN_WEIGHT_PARTS, name="ffn_down")
        if layer + 1 < depth:
            x, h = resnorm_tiled(x, f, norm_ffn_post[layer], norm_mix_pre[layer + 1],
                                 tr=rows // TILED_ROW_BLOCKS)
        else:
            y_prompt, y_sample = resnorm_last(x, f, norm_ffn_post[layer], seq_len=seq_len, n_dec=n_dec,
                                              tr=TR_PROMPT)

    return (y_prompt, y_sample, jnp.stack(gv_p), jnp.stack(gv_s),
            jnp.stack(wk_p), jnp.stack(wv_p), jnp.stack(wk_s), jnp.stack(wv_s),
            jnp.stack(cv_p), jnp.stack(cv_s))
```

```python
import functools
import math

import jax
import jax.numpy as jnp
from jax import lax
from jax.experimental import pallas as pl
from jax.experimental.pallas import tpu as pltpu

F32 = jnp.float32
BF16 = jnp.bfloat16

NORM_EPS = 1e-6
CHUNK = 128
N_GROUPS = 16
HEAD_DIM = 64
N_KV_HEADS = 8
Q_PER_KV = 8
WINDOW = 128
CONV_W = 3
ATTN_SCALE = HEAD_DIM ** -0.5
MASKED_SCORE = -1e30
LOG2_E = math.log2(math.e)

V7X_LANES = 128
V7X_SUBLANES = 8
V7X_MXU_COLS = 256
V7X_SCOPED_VMEM_BYTES = 60000 * 1024

DOT_ROWS = 512
WEIGHT_RING_SLOTS = 3


def _params(*semantics):
    return pltpu.CompilerParams(dimension_semantics=semantics,
                                vmem_limit_bytes=V7X_SCOPED_VMEM_BYTES)


def _rms(x, g):
    return x * lax.rsqrt(jnp.mean(x * x, axis=-1, keepdims=True) + NORM_EPS) * g


def _gelu(x):
    return 0.5 * x * (1.0 + lax.erf(x * math.sqrt(0.5)))


def _identity(x):
    return x


def _dot_bf16(x, w):
    return jnp.dot(x, w.astype(BF16), preferred_element_type=F32)


def _row_chunks(rows, size=DOT_ROWS):
    n = max(rows // size, 1)
    return [(c * size, (c + 1) * size if c + 1 < n else rows) for c in range(n)]


def _skip_aliased(body, n_aliased):
    def kernel_fn(*refs):
        body(*refs[n_aliased:])
    return kernel_fn


def _rmsnorm_kernel(x_ref, g_ref, h_ref):
    h_ref[0] = _rms(x_ref[0], g_ref[...]).astype(h_ref.dtype)


def _resnorm_kernel(x_ref, y_ref, gpost_ref, gnext_ref, xo_ref, ho_ref):
    xn = x_ref[0] + _rms(y_ref[0].astype(F32), gpost_ref[...])
    xo_ref[0] = xn
    ho_ref[0] = _rms(xn, gnext_ref[...]).astype(ho_ref.dtype)


def _resnorm_last_kernel(x_ref, y_ref, gpost_ref, xo_ref):
    xo_ref[0] = x_ref[0] + _rms(y_ref[0].astype(F32), gpost_ref[...])


def rmsnorm_first(x_prompt, x_sample, g, *, tr):
    n_seq, seq_len, d = x_prompt.shape
    n_dec = x_sample.shape[1]
    g = g.reshape(1, d)
    shape = jax.ShapeDtypeStruct((n_seq, seq_len + n_dec, d), BF16)
    row = pl.BlockSpec((1, tr, d), lambda s, r: (s, r, 0))
    h = pl.pallas_call(
        _rmsnorm_kernel, out_shape=shape, grid=(n_seq, seq_len // tr),
        in_specs=[row, pl.BlockSpec((1, d), lambda s, r: (0, 0))], out_specs=row,
        compiler_params=_params("parallel", "parallel"), name="rmsnorm_first_prompt",
    )(x_prompt, g)
    return pl.pallas_call(
        _skip_aliased(_rmsnorm_kernel, 1), out_shape=shape, grid=(n_seq,),
        in_specs=[pl.BlockSpec(memory_space=pl.ANY),
                  pl.BlockSpec((1, n_dec, d), lambda s: (0, 0, 0)),
                  pl.BlockSpec((1, d), lambda s: (0, 0))],
        out_specs=pl.BlockSpec((1, n_dec, d), lambda s: (s, seq_len // n_dec, 0)),
        input_output_aliases={0: 0},
        compiler_params=_params("arbitrary"), name="rmsnorm_first_sample",
    )(h, x_sample, g)


def resnorm_first(x_prompt, x_sample, y, g_post, g_next, *, tr):
    n_seq, seq_len, d = x_prompt.shape
    n_dec = x_sample.shape[1]
    g_post, g_next = g_post.reshape(1, d), g_next.reshape(1, d)
    shapes = (jax.ShapeDtypeStruct(y.shape, F32), jax.ShapeDtypeStruct(y.shape, BF16))
    row = pl.BlockSpec((1, tr, d), lambda s, r: (s, r, 0))
    vec = pl.BlockSpec((1, d), lambda s, r: (0, 0))
    xo, ho = pl.pallas_call(
        _resnorm_kernel, out_shape=shapes, grid=(n_seq, seq_len // tr),
        in_specs=[row, row, vec, vec], out_specs=(row, row),
        compiler_params=_params("parallel", "parallel"), name="resnorm_first_prompt",
    )(x_prompt, y, g_post, g_next)
    sample_rows = pl.BlockSpec((1, n_dec, d), lambda s: (s, seq_len // n_dec, 0))
    vec1 = pl.BlockSpec((1, d), lambda s: (0, 0))
    return pl.pallas_call(
        _skip_aliased(_resnorm_kernel, 2), out_shape=shapes, grid=(n_seq,),
        in_specs=[pl.BlockSpec(memory_space=pl.ANY), pl.BlockSpec(memory_space=pl.ANY),
                  pl.BlockSpec((1, n_dec, d), lambda s: (0, 0, 0)),
                  pl.BlockSpec((1, n_dec, d), lambda s: (0, seq_len // n_dec, 0)), vec1, vec1],
        out_specs=(sample_rows, sample_rows),
        input_output_aliases={0: 0, 1: 1},
        compiler_params=_params("arbitrary"), name="resnorm_first_sample",
    )(xo, ho, x_sample, y, g_post, g_next)


def resnorm_tiled(x, y, g_post, g_next, *, tr):
    n_seq, rows, d = x.shape
    row = pl.BlockSpec((1, tr, d), lambda s, r: (s, r, 0))
    vec = pl.BlockSpec((1, d), lambda s, r: (0, 0))
    return pl.pallas_call(
        _resnorm_kernel,
        out_shape=(jax.ShapeDtypeStruct(x.shape, F32), jax.ShapeDtypeStruct(x.shape, BF16)),
        grid=(n_seq, rows // tr),
        in_specs=[row, row, vec, vec], out_specs=(row, row),
        compiler_params=_params("parallel", "parallel"), name="resnorm_tiled",
    )(x, y, g_post.reshape(1, d), g_next.reshape(1, d))


def resnorm_last(x, y, g_post, *, seq_len, n_dec, tr):
    n_seq, _, d = x.shape
    g_post = g_post.reshape(1, d)
    row = pl.BlockSpec((1, tr, d), lambda s, r: (s, r, 0))
    y_prompt = pl.pallas_call(
        _resnorm_last_kernel, out_shape=jax.ShapeDtypeStruct((n_seq, seq_len, d), F32),
        grid=(n_seq, seq_len // tr),
        in_specs=[row, row, pl.BlockSpec((1, d), lambda s, r: (0, 0))], out_specs=row,
        compiler_params=_params("parallel", "parallel"), name="resnorm_last_prompt",
    )(x, y, g_post)
    sample_rows = pl.BlockSpec((1, n_dec, d), lambda s: (0, seq_len // n_dec, 0))
    y_sample = pl.pallas_call(
        _resnorm_last_kernel, out_shape=jax.ShapeDtypeStruct((1, n_dec, d), F32), grid=(1,),
        in_specs=[sample_rows, sample_rows, pl.BlockSpec((1, d), lambda s: (0, 0))],
        out_specs=pl.BlockSpec((1, n_dec, d), lambda s: (0, 0, 0)),
        compiler_params=_params("arbitrary"), name="resnorm_last_sample",
    )(x, y, g_post)
    return y_prompt, y_sample.reshape(n_dec, 1, d)


def _copy_rows_kernel(src_ref, o_ref):
    o_ref[0] = src_ref[...]


def insert_sample_rows(tiled, rows, *, seq_len):
    n_seq, _, c = tiled.shape
    n_dec = rows.shape[0]
    return pl.pallas_call(
        _skip_aliased(_copy_rows_kernel, 1), out_shape=jax.ShapeDtypeStruct(tiled.shape, tiled.dtype),
        grid=(n_seq,),
        in_specs=[pl.BlockSpec(memory_space=pl.ANY), pl.BlockSpec((n_dec, c), lambda s: (0, 0))],
        out_specs=pl.BlockSpec((1, n_dec, c), lambda s: (s, seq_len // n_dec, 0)),
        input_output_aliases={0: 0},
        compiler_params=_params("arbitrary"), name="insert_sample_rows",
    )(tiled, rows)


def _resident_row_tile(x_hbm, xbuf_ref, sem, *, row_tiles):
    i = pl.program_id(0)
    j = pl.program_id(1)
    tm = xbuf_ref.shape[1]

    def tile_copy(t):
        rows = pl.ds(pl.multiple_of((t % row_tiles) * tm, 2 * V7X_SUBLANES), tm)
        return pltpu.make_async_copy(x_hbm.at[t // row_tiles, rows, :], xbuf_ref.at[t % 2], sem.at[t % 2])

    @pl.when(j == 0)
    def _():
        @pl.when(i == 0)
        def _():
            tile_copy(0).start()

        @pl.when(i + 1 < pl.num_programs(0))
        def _():
            tile_copy(i + 1).start()

        tile_copy(i).wait()

    return xbuf_ref.at[i % 2]


def _streamed_weight_tile(w_hbm, wring_ref, wsem, *, layer, col_block_offset, n_steps):
    n_col_tiles = pl.num_programs(1)
    step = pl.program_id(0) * n_col_tiles + pl.program_id(1)
    tn = wring_ref.shape[2]
    ahead = WEIGHT_RING_SLOTS - 1

    def tile_copy(s):
        col_tile = s % n_col_tiles + col_block_offset
        cols = pl.ds(pl.multiple_of(col_tile * tn, V7X_LANES), tn)
        slot = s % WEIGHT_RING_SLOTS
        return pltpu.make_async_copy(w_hbm.at[layer, :, cols], wring_ref.at[slot], wsem.at[slot])

    @pl.when(step == 0)
    def _():
        for s in range(min(ahead, n_steps)):
            tile_copy(s).start()

    @pl.when(step + ahead < n_steps)
    def _():
        tile_copy(step + ahead).start()

    tile_copy(step).wait()
    return wring_ref.at[step % WEIGHT_RING_SLOTS]


def _matmul_kernel(*refs, epilogue, has_bias, row_tiles, layer, col_block_offset, n_steps):
    if has_bias:
        x_hbm, w_hbm, b_ref, o_ref, xbuf_ref, sem, wring_ref, wsem = refs
    else:
        x_hbm, w_hbm, o_ref, xbuf_ref, sem, wring_ref, wsem = refs
    x_ref = _resident_row_tile(x_hbm, xbuf_ref, sem, row_tiles=row_tiles)
    w_ref = _streamed_weight_tile(w_hbm, wring_ref, wsem, layer=layer, col_block_offset=col_block_offset,
                                  n_steps=n_steps)
    for r0, r1 in _row_chunks(x_ref.shape[0]):
        acc = _dot_bf16(x_ref[r0:r1], w_ref[...])
        if has_bias:
            acc = acc + b_ref[...]
        o_ref[0, r0:r1] = epilogue(acc).astype(o_ref.dtype)


def matmul(x, w, bias=None, *, layer, row_tiles, tn, n_out, col_block_offset=0, out_dtype=F32,
           epilogue=_identity, name):
    n_seq, rows, k = x.shape
    tm = rows // row_tiles
    grid = (n_seq * row_tiles, n_out // tn)
    in_specs = [pl.BlockSpec(memory_space=pl.ANY), pl.BlockSpec(memory_space=pl.ANY)]
    args = [x, w]
    if bias is not None:
        in_specs.append(pl.BlockSpec((None, 1, tn), lambda i, j: (layer, 0, j + col_block_offset)))
        args.append(bias.reshape(bias.shape[0], 1, -1))
    return pl.pallas_call(
        functools.partial(_matmul_kernel, epilogue=epilogue, has_bias=bias is not None,
                          row_tiles=row_tiles, layer=layer, col_block_offset=col_block_offset,
                          n_steps=grid[0] * grid[1]),
        out_shape=jax.ShapeDtypeStruct((n_seq, rows, n_out), out_dtype),
        grid=grid,
        in_specs=in_specs,
        out_specs=pl.BlockSpec((1, tm, tn), lambda i, j: (i // row_tiles, i % row_tiles, j)),
        scratch_shapes=[pltpu.VMEM((2, tm, k), x.dtype), pltpu.SemaphoreType.DMA((2,)),
                        pltpu.VMEM((WEIGHT_RING_SLOTS, k, tn), w.dtype),
                        pltpu.SemaphoreType.DMA((WEIGHT_RING_SLOTS,))],
        compiler_params=_params("arbitrary", "arbitrary"),
        name=name,
    )(*args)


def _matmul_wres_kernel(x_ref, w_hbm, o_ref, wb_ref, stage_ref, sem, *, layer, parts, steps_per_part):
    j = pl.program_id(0)
    i = pl.program_id(1)
    n_col_tiles = pl.num_programs(0)
    _, k, tn = wb_ref.shape
    kp = k // parts

    def part_copy(col_tile, part):
        rows = pl.ds(pl.multiple_of(part * kp, V7X_SUBLANES), kp)
        cols = pl.ds(pl.multiple_of(col_tile * tn, V7X_LANES), tn)
        return pltpu.make_async_copy(w_hbm.at[layer, rows, cols], stage_ref, sem.at[0])

    def round_part(col_tile, part):
        rows = pl.ds(pl.multiple_of(part * kp, 2 * V7X_SUBLANES), kp)
        wb_ref[col_tile % 2, rows, :] = stage_ref[...].astype(BF16)

    @pl.when((j == 0) & (i == 0))
    def _():
        for part in range(parts):
            part_copy(0, part).start()
            part_copy(0, part).wait()
            round_part(0, part)

    part = i // steps_per_part
    phase = i % steps_per_part
    prefetching = (j + 1 < n_col_tiles) & (part < parts)

    @pl.when(prefetching & (phase == 0))
    def _():
        part_copy(j + 1, part).start()

    @pl.when(prefetching & (phase == steps_per_part - 2))
    def _():
        part_copy(j + 1, part).wait()
        round_part(j + 1, part)

    o_ref[0] = jnp.dot(x_ref[0], wb_ref[j % 2], preferred_element_type=F32).astype(o_ref.dtype)


def matmul_weight_resident(x, w, *, layer, tm, tn, parts, name):
    n_seq, rows, k = x.shape
    n = w.shape[2]
    per_seq = rows // tm
    n_row_tiles = n_seq * per_seq
    steps_per_part = n_row_tiles // parts
    assert steps_per_part >= 2 and k % (parts * 2 * V7X_SUBLANES) == 0
    return pl.pallas_call(
        functools.partial(_matmul_wres_kernel, layer=layer, parts=parts, steps_per_part=steps_per_part),
        out_shape=jax.ShapeDtypeStruct((n_seq, rows, n), BF16),
        grid=(n // tn, n_row_tiles),
        in_specs=[pl.BlockSpec((1, tm, k), lambda j, i: (i // per_seq, i % per_seq, 0)),
                  pl.BlockSpec(memory_space=pl.ANY)],
        out_specs=pl.BlockSpec((1, tm, tn), lambda j, i: (i // per_seq, i % per_seq, j)),
        scratch_shapes=[pltpu.VMEM((2, k, tn), BF16), pltpu.VMEM((k // parts, tn), F32),
                        pltpu.SemaphoreType.DMA((1,))],
        compiler_params=_params("arbitrary", "arbitrary"),
        name=name,
    )(x, w)


def _layer_norm(v, g, b):
    xc = v - jnp.mean(v, axis=-1, keepdims=True)
    return xc * lax.rsqrt(jnp.mean(xc * xc, axis=-1, keepdims=True) + NORM_EPS) * g + b


def _spatial_kernel(v_ref, u_ref, lng_ref, lnb_ref, ws_ref, bias_ref, o_ref, gv_ref, *, group_dim):
    t = lax.broadcasted_iota(jnp.int32, (CHUNK, CHUNK), 0)
    s = lax.broadcasted_iota(jnp.int32, (CHUNK, CHUNK), 1)
    causal = s <= t
    for c in range(v_ref.shape[1] // CHUNK):
        rows = slice(c * CHUNK, (c + 1) * CHUNK)
        vn = _layer_norm(v_ref[0, rows], lng_ref[...], lnb_ref[...])
        gv_ref[0] = vn
        vb = vn.astype(BF16)
        for g in range(N_GROUPS):
            cols = slice(g * group_dim, (g + 1) * group_dim)
            wc = jnp.where(causal, ws_ref[g], 0.0).astype(BF16)
            mix = jnp.dot(wc, vb[:, cols], preferred_element_type=F32) + bias_ref[:, cols]
            o_ref[0, rows, cols] = (u_ref[0, rows, cols].astype(F32) * mix).astype(o_ref.dtype)


def _spatial_sample_kernel(v_ref, u_ref, lng_ref, lnb_ref, w00_ref, b0_ref, o_ref, gv_ref):
    vn = _layer_norm(v_ref[0], lng_ref[...], lnb_ref[...])
    gv_ref[0] = vn
    mix = w00_ref[...].astype(BF16).astype(F32) * vn.astype(BF16).astype(F32) + b0_ref[...]
    o_ref[0] = (u_ref[0].astype(F32) * mix).astype(o_ref.dtype)


def spatial_gate(v, u, ln_g, ln_b, w_s, b_s, *, seq_len):
    n_seq, rows, d = v.shape
    n_dec = rows - seq_len
    group_dim = d // N_GROUPS
    ln_g, ln_b = ln_g.reshape(1, d), ln_b.reshape(1, d)
    bias_full = jnp.repeat(b_s.T, group_dim, axis=1)
    block_rows = SPATIAL_CHUNKS_PER_STEP * CHUNK
    row = pl.BlockSpec((1, block_rows, d), lambda s, c: (s, c, 0))
    vec = pl.BlockSpec((1, d), lambda s, c: (0, 0))
    gated, gv_prompt = pl.pallas_call(
        functools.partial(_spatial_kernel, group_dim=group_dim),
        out_shape=(jax.ShapeDtypeStruct(v.shape, BF16),
                   jax.ShapeDtypeStruct((n_seq, CHUNK, d), F32)),
        grid=(n_seq, seq_len // block_rows),
        in_specs=[row, row, vec, vec,
                  pl.BlockSpec((N_GROUPS, CHUNK, CHUNK), lambda s, c: (0, 0, 0)),
                  pl.BlockSpec((CHUNK, d), lambda s, c: (0, 0))],
        out_specs=(row, pl.BlockSpec((1, CHUNK, d), lambda s, c: (s, 0, 0))),
        compiler_params=_params("parallel", "arbitrary"),
        name="spatial_gate_prompt",
    )(v, u, ln_g, ln_b, w_s, bias_full)
    w00 = jnp.repeat(w_s[:, 0, 0], group_dim).reshape(1, d)
    b0 = jnp.repeat(b_s[:, 0], group_dim).reshape(1, d)
    tile0_rows = pl.BlockSpec((1, n_dec, d), lambda s: (0, seq_len // n_dec, 0))
    vec1 = pl.BlockSpec((1, d), lambda s: (0, 0))
    gated, gv_sample = pl.pallas_call(
        _skip_aliased(_spatial_sample_kernel, 1),
        out_shape=(jax.ShapeDtypeStruct(v.shape, BF16), jax.ShapeDtypeStruct((1, n_dec, d), F32)),
        grid=(n_seq,),
        in_specs=[pl.BlockSpec(memory_space=pl.ANY), tile0_rows, tile0_rows, vec1, vec1, vec1, vec1],
        out_specs=(pl.BlockSpec((1, n_dec, d), lambda s: (s, seq_len // n_dec, 0)),
                   pl.BlockSpec((1, n_dec, d), lambda s: (0, 0, 0))),
        input_output_aliases={0: 0},
        compiler_params=_params("arbitrary"),
        name="spatial_gate_sample",
    )(gated, v, u, ln_g, ln_b, w00, b0)
    return gated, gv_prompt, gv_sample


def _pair_block_diag(pair, even):
    lane = lax.broadcasted_iota(jnp.int32, pair.shape, 1)
    if even:
        own = jnp.where(lane < HEAD_DIM, pair, 0.0)
        return jnp.concatenate([own, pltpu.roll(own, HEAD_DIM, 1)], axis=0)
    own = jnp.where(lane >= HEAD_DIM, pair, 0.0)
    return jnp.concatenate([pltpu.roll(own, HEAD_DIM, 1), own], axis=0)


def _attn_prompt_kernel(sink_ref, q_ref, kprev_ref, kown_ref, vprev_ref, vown_ref, o_ref):
    blk = pl.program_id(1)
    n_keys = 2 * WINDOW
    qi = lax.broadcasted_iota(jnp.int32, (WINDOW, n_keys), 0)
    kj = lax.broadcasted_iota(jnp.int32, (WINDOW, n_keys), 1)
    first_key = jnp.where(blk > 0, 0, WINDOW)
    valid = (kj >= qi) & (kj <= qi + WINDOW) & (kj >= first_key)
    col_blocks = Q_PER_KV * HEAD_DIM // V7X_LANES
    valid = jnp.concatenate([valid] * col_blocks, axis=0)
    block_of_row = lax.broadcasted_iota(jnp.int32, (col_blocks * WINDOW, 1), 0) // WINDOW
    head_lane = lax.broadcasted_iota(jnp.int32, (col_blocks * WINDOW, V7X_LANES), 1)
    group_w = Q_PER_KV * HEAD_DIM
    for pair in range(N_KV_HEADS // 2):
        lanes = slice(pair * V7X_LANES, (pair + 1) * V7X_LANES)
        k_pair = jnp.concatenate([kprev_ref[0, :, lanes], kown_ref[0, :, lanes]], axis=0)
        v_pair = jnp.concatenate([vprev_ref[0, :, lanes], vown_ref[0, :, lanes]], axis=0)
        for e in range(2):
            h = 2 * pair + e
            k2 = _pair_block_diag(k_pair, e == 0).astype(BF16)
            v2 = _pair_block_diag(v_pair, e == 0).astype(BF16)
            qs = jnp.concatenate(
                [q_ref[0, :, h * group_w + c * V7X_LANES: h * group_w + (c + 1) * V7X_LANES]
                 for c in range(col_blocks)], axis=0)
            qs = (qs.astype(F32) * (ATTN_SCALE * LOG2_E)).astype(BF16)
            s = lax.dot_general(qs, k2, (((1,), (1,)), ((), ())), preferred_element_type=F32)
            probs, inv_denoms = [], []
            for half in range(2):
                sh = jnp.where(valid, s[:, half * n_keys:(half + 1) * n_keys], MASKED_SCORE)
                sink = jnp.zeros((col_blocks * WINDOW, 1), F32)
                for c in range(col_blocks):
                    sink = jnp.where(block_of_row == c, sink_ref[h * Q_PER_KV + 2 * c + half] * LOG2_E,
                                     sink)
                m = jnp.maximum(jnp.max(sh, axis=-1, keepdims=True), sink)
                p = jnp.exp2(sh - m)
                inv_denoms.append(1.0 / (jnp.sum(p, axis=-1, keepdims=True) + jnp.exp2(sink - m)))
                probs.append(p.astype(BF16))
            o = jnp.dot(jnp.concatenate(probs, axis=1), v2, preferred_element_type=F32)
            o = o * jnp.where(head_lane < HEAD_DIM, inv_denoms[0], inv_denoms[1])
            for c in range(col_blocks):
                o_ref[0, :, h * group_w + c * V7X_LANES: h * group_w + (c + 1) * V7X_LANES] = (
                    o[c * WINDOW:(c + 1) * WINDOW].astype(o_ref.dtype))


def attention_prompt(q, kv, sinks, *, seq_len):
    n_seq, rows, d_q = q.shape
    d_kv = N_KV_HEADS * HEAD_DIM
    k_col, v_col = 0, 1

    def own(col):
        return lambda s, i: (s, i, col)

    def prev(col):
        return lambda s, i: (s, jnp.maximum(i - 1, 0), col)

    return pl.pallas_call(
        _attn_prompt_kernel,
        out_shape=jax.ShapeDtypeStruct((n_seq, rows, d_q), BF16),
        grid=(n_seq, seq_len // WINDOW),
        in_specs=[pl.BlockSpec(memory_space=pltpu.SMEM),
                  pl.BlockSpec((1, WINDOW, d_q), own(0)),
                  pl.BlockSpec((1, WINDOW, d_kv), prev(k_col)),
                  pl.BlockSpec((1, WINDOW, d_kv), own(k_col)),
                  pl.BlockSpec((1, WINDOW, d_kv), prev(v_col)),
                  pl.BlockSpec((1, WINDOW, d_kv), own(v_col))],
        out_specs=pl.BlockSpec((1, WINDOW, d_q), own(0)),
        compiler_params=_params("parallel", "arbitrary"),
        name="attention_prompt",
    )(sinks, q, kv, kv, kv, kv)


def _attn_sample_kernel(q2_ref, ck_ref, cv_ref, knew_ref, vnew_ref, sink_ref, o_ref):
    for pair in range(N_KV_HEADS // 2):
        lanes = slice(pair * V7X_LANES, (pair + 1) * V7X_LANES)
        q2 = q2_ref[0, pair]
        k_pair = ck_ref[0, :, lanes].astype(BF16)
        v_pair = cv_ref[0, :, lanes].astype(BF16)
        k_new = knew_ref[0, :, lanes].astype(BF16).astype(F32)
        v_new = vnew_ref[0, :, lanes].astype(BF16).astype(F32)
        sink = sink_ref[pair][:, :1]
        s = lax.dot_general(q2, k_pair, (((1,), (1,)), ((), ())), preferred_element_type=F32)
        s_new = jnp.sum(q2.astype(F32) * k_new, axis=-1, keepdims=True)
        m = jnp.maximum(jnp.maximum(jnp.max(s, axis=-1, keepdims=True), s_new), sink)
        p = jnp.exp(s - m)
        p_new = jnp.exp(s_new - m)
        denom = jnp.sum(p, axis=-1, keepdims=True) + p_new + jnp.exp(sink - m)
        o = jnp.dot((p / denom).astype(BF16), v_pair, preferred_element_type=F32)
        o_ref[0, pair] = o + (p_new / denom).astype(BF16).astype(F32) * v_new


def attention_sample(q, kv, cache_k, cache_v, sinks):
    n, d_q = q.shape
    d_kv = N_KV_HEADS * HEAD_DIM
    n_pairs = N_KV_HEADS // 2
    rows = 2 * Q_PER_KV
    q = (q.astype(F32) * ATTN_SCALE).astype(BF16).reshape(n, n_pairs, 2, Q_PER_KV, HEAD_DIM)
    zeros = jnp.zeros((n, n_pairs, Q_PER_KV, HEAD_DIM), BF16)
    q2 = jnp.concatenate([jnp.concatenate([q[:, :, 0], zeros], axis=-1),
                          jnp.concatenate([zeros, q[:, :, 1]], axis=-1)], axis=2)
    k_new = kv[:, :d_kv].reshape(n, 1, d_kv)
    v_new = kv[:, d_kv:].reshape(n, 1, d_kv)
    sink2 = jnp.broadcast_to(sinks.reshape(n_pairs, rows, 1), (n_pairs, rows, V7X_LANES))
    o2 = pl.pallas_call(
        _attn_sample_kernel,
        out_shape=jax.ShapeDtypeStruct((n, n_pairs, rows, V7X_LANES), F32),
        grid=(n,),
        in_specs=[pl.BlockSpec((1, n_pairs, rows, V7X_LANES), lambda b: (b, 0, 0, 0)),
                  pl.BlockSpec((1, WINDOW, d_kv), lambda b: (b, 0, 0)),
                  pl.BlockSpec((1, WINDOW, d_kv), lambda b: (b, 0, 0)),
                  pl.BlockSpec((1, 1, d_kv), lambda b: (b, 0, 0)),
                  pl.BlockSpec((1, 1, d_kv), lambda b: (b, 0, 0)),
                  pl.BlockSpec((n_pairs, rows, V7X_LANES), lambda b: (0, 0, 0))],
        out_specs=pl.BlockSpec((1, n_pairs, rows, V7X_LANES), lambda b: (b, 0, 0, 0)),
        compiler_params=_params("parallel"),
        name="attention_sample",
    )(q2, cache_k.reshape(n, WINDOW, d_kv), cache_v.reshape(n, WINDOW, d_kv), k_new, v_new, sink2)
    o = jnp.stack([o2[:, :, :Q_PER_KV, :HEAD_DIM], o2[:, :, Q_PER_KV:, HEAD_DIM:]], axis=2)
    return o.reshape(n, d_q).astype(BF16), k_new, v_new


def _silu_gate(gate, val):
    return gate * (1.0 / (1.0 + jnp.exp(-gate))) * val


def _ffn_up_kernel(x_hbm, wg_ref, wv_ref, cwg_ref, cwv_ref, cbg_ref, cbv_ref,
                   p0g_ref, p0v_ref, p1g_ref, p1v_ref,
                   a_ref, sg_ref, sv_ref, hsg_ref, hsv_ref, xbuf_ref, sem, h_ref, tail_ref,
                   *, halves, n_dec):
    half = pl.program_id(0) % halves
    j = pl.program_id(1)
    tn = wg_ref.shape[1]
    tail_rows = tail_ref.shape[1]
    x_ref = _resident_row_tile(x_hbm, xbuf_ref, sem, row_tiles=halves)
    tm = x_ref.shape[0]
    gate_cols, val_cols = slice(0, tn), slice(tn, 2 * tn)

    @pl.when((pl.program_id(0) == 0) & (j == 0))
    def _():
        tail_ref[...] = jnp.zeros(tail_ref.shape, tail_ref.dtype)

    first_tile = jnp.full((tail_rows, 2 * tn), half, jnp.int32) == 0
    h_ref[0:tail_rows] = jnp.where(first_tile, 0.0, tail_ref[j])

    def conv(r0, r1, cols, cw_ref, cb_ref):
        ext = h_ref[r0:r1 + tail_rows, cols]
        h1 = pltpu.roll(ext, 1, 0)[tail_rows:]
        h2 = pltpu.roll(ext, 2, 0)[tail_rows:]
        return cb_ref[...] + cw_ref[0:1] * h2 + cw_ref[1:2] * h1 + cw_ref[2:3] * ext[tail_rows:]

    def epilogue(r0, r1):
        a_ref[0, r0:r1] = _silu_gate(conv(r0, r1, gate_cols, cwg_ref, cbg_ref),
                                     conv(r0, r1, val_cols, cwv_ref, cbv_ref)).astype(a_ref.dtype)

    pending = None
    for r0, r1 in _row_chunks(tm):
        x = x_ref[r0:r1]
        h_ref[tail_rows + r0:tail_rows + r1, gate_cols] = _dot_bf16(x, wg_ref[...])
        h_ref[tail_rows + r0:tail_rows + r1, val_cols] = _dot_bf16(x, wv_ref[...])
        if pending is not None:
            epilogue(*pending)
        pending = (r0, r1)
    epilogue(*pending)
    tail_ref[j] = h_ref[tm:tm + tail_rows]

    h_sample = h_ref[tail_rows + tm - n_dec:tail_rows + tm]
    hsg_ref[0, 0] = h_sample[:, gate_cols]
    hsv_ref[0, 0] = h_sample[:, val_cols]
    seq_tail = h_ref[tm - n_dec:tm - n_dec + tail_rows]
    sg_ref[0, 0] = seq_tail[:, gate_cols]
    sv_ref[0, 0] = seq_tail[:, val_cols]

    @pl.when(half == halves - 1)
    def _():
        def conv_sample(cols, p0_ref, p1_ref, cw_ref, cb_ref):
            return (cb_ref[...] + cw_ref[0:1] * p0_ref[...] + cw_ref[1:2] * p1_ref[...]
                    + cw_ref[2:3] * h_sample[:, cols])

        a_ref[0, tm - n_dec:tm] = _silu_gate(
            conv_sample(gate_cols, p0g_ref, p1g_ref, cwg_ref, cbg_ref),
            conv_sample(val_cols, p0v_ref, p1v_ref, cwv_ref, cbv_ref)).astype(a_ref.dtype)


def ffn_up(h, w_up, conv_w, conv_b, state, *, layer, seq_len, halves, tn, tail_rows):
    n_seq, rows, k = h.shape
    n_dec = rows - seq_len
    d_ff = w_up.shape[2] // 2
    nj = d_ff // tn
    tm = rows // halves
    conv_b = conv_b.reshape(conv_b.shape[0], 1, -1)
    p0, p1 = state[:, 0], state[:, 1]

    def gate(shape):
        return pl.BlockSpec(shape, lambda i, j: (0, j))

    def val(shape):
        return pl.BlockSpec(shape, lambda i, j: (0, j + nj))

    def gate_l(shape):
        return pl.BlockSpec((None,) + shape, lambda i, j: (layer, 0, j))

    def val_l(shape):
        return pl.BlockSpec((None,) + shape, lambda i, j: (layer, 0, j + nj))

    per_tile = lambda r: pl.BlockSpec((1, 1, r, tn), lambda i, j: (i // halves, i % halves, 0, j))
    a, sg, sv, hsg, hsv = pl.pallas_call(
        functools.partial(_ffn_up_kernel, halves=halves, n_dec=n_dec),
        out_shape=(jax.ShapeDtypeStruct((n_seq, rows, d_ff), BF16),
                   jax.ShapeDtypeStruct((n_seq, halves, tail_rows, d_ff), F32),
                   jax.ShapeDtypeStruct((n_seq, halves, tail_rows, d_ff), F32),
                   jax.ShapeDtypeStruct((n_seq, halves, n_dec, d_ff), F32),
                   jax.ShapeDtypeStruct((n_seq, halves, n_dec, d_ff), F32)),
        grid=(n_seq * halves, nj),
        in_specs=[pl.BlockSpec(memory_space=pl.ANY),
                  gate_l((k, tn)), val_l((k, tn)), gate_l((CONV_W, tn)), val_l((CONV_W, tn)),
                  gate_l((1, tn)), val_l((1, tn)),
                  gate((n_dec, tn)), val((n_dec, tn)), gate((n_dec, tn)), val((n_dec, tn))],
        out_specs=(pl.BlockSpec((1, tm, tn), lambda i, j: (i // halves, i % halves, j)),
                   per_tile(tail_rows), per_tile(tail_rows), per_tile(n_dec), per_tile(n_dec)),
        scratch_shapes=[pltpu.VMEM((2, tm, k), h.dtype), pltpu.SemaphoreType.DMA((2,)),
                        pltpu.VMEM((tail_rows + tm, 2 * tn), F32),
                        pltpu.VMEM((nj, tail_rows, 2 * tn), F32)],
        compiler_params=_params("arbitrary", "arbitrary"),
        name="ffn_up",
    )(h, w_up, w_up, conv_w, conv_w, conv_b, conv_b, p0, p0, p1, p1)
    keep = slice(tail_rows - (CONV_W - 1), tail_rows)
    state_prompt = jnp.concatenate([sg[:, halves - 1, keep], sv[:, halves - 1, keep]], axis=-1)
    h_sample = jnp.concatenate([hsg[0, halves - 1], hsv[0, halves - 1]], axis=-1)
    return a, state_prompt, h_sample


ROW_TILES = 2
TN = 512
TN_FFN_UP = V7X_MXU_COLS
TR_PROMPT = 512
TILED_ROW_BLOCKS = 5
FFN_DOWN_ROW_BLOCKS = 5
FFN_DOWN_WEIGHT_PARTS = 4
STATE_TAIL_ROWS = V7X_SUBLANES
SPATIAL_CHUNKS_PER_STEP = 2


def kernel(x_prompt, x_sample, cache_win_k, cache_win_v, state_conv, norm_mix_pre, norm_mix_post,
           norm_ffn_pre, norm_ffn_post, gmlp_w_in, gmlp_ln_g, gmlp_ln_b, gmlp_w_s, gmlp_b_s,
           gmlp_w_out, attn_w_qkv, attn_b_qkv, attn_sinks, attn_w_o, attn_b_o, ffn_w_up,
           ffn_conv_w, ffn_conv_b, ffn_w_down):
    n_seq, seq_len, d = x_prompt.shape
    n_dec = x_sample.shape[0]
    depth = norm_mix_pre.shape[0]
    rows = seq_len + n_dec
    x_sample = x_sample.reshape(1, n_dec, d)
    mm = functools.partial(matmul, row_tiles=ROW_TILES, tn=TN)

    h = rmsnorm_first(x_prompt, x_sample, norm_mix_pre[0], tr=TR_PROMPT)
    x = None
    gv_p, gv_s, wk_p, wv_p, wk_s, wv_s, cv_p, cv_s = [], [], [], [], [], [], [], []
    for layer in range(depth):
        idx = layer // 2
        if layer % 2 == 0:
            d_g = gmlp_w_out.shape[1]
            u = mm(h, gmlp_w_in, layer=idx, n_out=d_g, out_dtype=BF16, epilogue=_gelu, name="gmlp_in_u")
            v = mm(h, gmlp_w_in, layer=idx, n_out=d_g, col_block_offset=d_g // TN, epilogue=_gelu,
                   name="gmlp_in_v")
            gated, g_p, g_s = spatial_gate(v, u, gmlp_ln_g[idx], gmlp_ln_b[idx], gmlp_w_s[idx],
                                           gmlp_b_s[idx], seq_len=seq_len)
            gv_p.append(g_p)
            gv_s.append(g_s.reshape(n_dec, 1, d_g))
            y = mm(gated, gmlp_w_out, layer=idx, n_out=d, out_dtype=BF16, name="gmlp_out")
        else:
            d_q = attn_w_o.shape[1]
            d_qkv = attn_w_qkv.shape[2]
            d_kv = (d_qkv - d_q) // 2
            q = mm(h, attn_w_qkv, attn_b_qkv, layer=idx, n_out=d_q, out_dtype=BF16, name="attn_q")
            kv = mm(h, attn_w_qkv, attn_b_qkv, layer=idx, n_out=2 * d_kv, col_block_offset=d_q // TN,
                    name="attn_kv")
            o = attention_prompt(q, kv, attn_sinks[idx], seq_len=seq_len)
            o_s, k_new, v_new = attention_sample(q[0, seq_len:], kv[0, seq_len:], cache_win_k[idx],
                                                 cache_win_v[idx], attn_sinks[idx])
            o = insert_sample_rows(o, o_s, seq_len=seq_len)
            tail = kv[:, seq_len - WINDOW:seq_len]
            wk_p.append(tail[:, :, :d_kv].reshape(n_seq, WINDOW, N_KV_HEADS, HEAD_DIM))
            wv_p.append(tail[:, :, d_kv:].reshape(n_seq, WINDOW, N_KV_HEADS, HEAD_DIM))
            wk_s.append(jnp.concatenate(
                [cache_win_k[idx][:, 1:], k_new.reshape(n_dec, 1, N_KV_HEADS, HEAD_DIM)], axis=1))
            wv_s.append(jnp.concatenate(
                [cache_win_v[idx][:, 1:], v_new.reshape(n_dec, 1, N_KV_HEADS, HEAD_DIM)], axis=1))
            y = mm(o, attn_w_o, attn_b_o, layer=idx, n_out=d, out_dtype=BF16, name="attn_out")
        if x is None:
            x, h = resnorm_first(x_prompt, x_sample, y, norm_mix_post[layer], norm_ffn_pre[layer],
                                 tr=TR_PROMPT)
        else:
            x, h = resnorm_tiled(x, y, norm_mix_post[layer], norm_ffn_pre[layer],
                                 tr=rows // TILED_ROW_BLOCKS)

        a, c_p, hu_s = ffn_up(h, ffn_w_up, ffn_conv_w, ffn_conv_b, state_conv[layer], layer=layer,
                              seq_len=seq_len, halves=ROW_TILES, tn=TN_FFN_UP, tail_rows=STATE_TAIL_ROWS)
        cv_p.append(c_p)
        cv_s.append(jnp.concatenate([state_conv[layer][:, 1:], hu_s[:, None]], axis=1))
        f = matmul_weight_resident(a, ffn_w_down, layer=layer, tm=rows // FFN_DOWN_ROW_BLOCKS, tn=TN,
                                   parts=FFN_DOWN_WEIGHT_PARTS, name="ffn_down")
        if layer + 1 < depth:
            x, h = resnorm_tiled(x, f, norm_ffn_post[layer], norm_mix_pre[layer + 1],
                                 tr=rows // TILED_ROW_BLOCKS)
        else:
            y_prompt, y_sample = resnorm_last(x, f, norm_ffn_post[layer], seq_len=seq_len, n_dec=n_dec,
                                              tr=TR_PROMPT)

    return (y_prompt, y_sample, jnp.stack(gv_p), jnp.stack(gv_s),
            jnp.stack(wk_p), jnp.stack(wv_p), jnp.stack(wk_s), jnp.stack(wv_s),
            jnp.stack(cv_p), jnp.stack(cv_s))
```

```python
import functools
import math

import jax
import jax.numpy as jnp
from jax import lax
from jax.experimental import pallas as pl
from jax.experimental.pallas import tpu as pltpu

F32 = jnp.float32
BF16 = jnp.bfloat16

NORM_EPS = 1e-6
CHUNK = 128
N_GROUPS = 16
HEAD_DIM = 64
N_KV_HEADS = 8
Q_PER_KV = 8
WINDOW = 128
CONV_W = 3
ATTN_SCALE = HEAD_DIM ** -0.5
MASKED_SCORE = -1e30
LOG2_E = math.log2(math.e)

V7X_LANES = 128
V7X_SUBLANES = 8
V7X_MXU_COLS = 256
V7X_SCOPED_VMEM_BYTES = 60000 * 1024

DOT_ROWS = 512


def _params(*semantics):
    return pltpu.CompilerParams(dimension_semantics=semantics,
                                vmem_limit_bytes=V7X_SCOPED_VMEM_BYTES)


def _rms(x, g):
    return x * lax.rsqrt(jnp.mean(x * x, axis=-1, keepdims=True) + NORM_EPS) * g


def _gelu(x):
    return 0.5 * x * (1.0 + lax.erf(x * math.sqrt(0.5)))


def _identity(x):
    return x


def _dot_bf16(x, w):
    return jnp.dot(x, w.astype(BF16), preferred_element_type=F32)


def _row_chunks(rows, size=DOT_ROWS):
    n = max(rows // size, 1)
    return [(c * size, (c + 1) * size if c + 1 < n else rows) for c in range(n)]


def _skip_aliased(body, n_aliased):
    def kernel_fn(*refs):
        body(*refs[n_aliased:])
    return kernel_fn


def _rmsnorm_kernel(x_ref, g_ref, h_ref):
    h_ref[0] = _rms(x_ref[0], g_ref[...]).astype(h_ref.dtype)


def _resnorm_kernel(x_ref, y_ref, gpost_ref, gnext_ref, xo_ref, ho_ref):
    xn = x_ref[0] + _rms(y_ref[0].astype(F32), gpost_ref[...])
    xo_ref[0] = xn
    ho_ref[0] = _rms(xn, gnext_ref[...]).astype(ho_ref.dtype)


def _resnorm_last_kernel(x_ref, y_ref, gpost_ref, xo_ref):
    xo_ref[0] = x_ref[0] + _rms(y_ref[0].astype(F32), gpost_ref[...])


def rmsnorm_first(x_prompt, x_sample, g, *, tr):
    n_seq, seq_len, d = x_prompt.shape
    n_dec = x_sample.shape[1]
    g = g.reshape(1, d)
    shape = jax.ShapeDtypeStruct((n_seq, seq_len + n_dec, d), BF16)
    row = pl.BlockSpec((1, tr, d), lambda s, r: (s, r, 0))
    h = pl.pallas_call(
        _rmsnorm_kernel, out_shape=shape, grid=(n_seq, seq_len // tr),
        in_specs=[row, pl.BlockSpec((1, d), lambda s, r: (0, 0))], out_specs=row,
        compiler_params=_params("parallel", "parallel"), name="rmsnorm_first_prompt",
    )(x_prompt, g)
    return pl.pallas_call(
        _skip_aliased(_rmsnorm_kernel, 1), out_shape=shape, grid=(n_seq,),
        in_specs=[pl.BlockSpec(memory_space=pl.ANY),
                  pl.BlockSpec((1, n_dec, d), lambda s: (0, 0, 0)),
                  pl.BlockSpec((1, d), lambda s: (0, 0))],
        out_specs=pl.BlockSpec((1, n_dec, d), lambda s: (s, seq_len // n_dec, 0)),
        input_output_aliases={0: 0},
        compiler_params=_params("arbitrary"), name="rmsnorm_first_sample",
    )(h, x_sample, g)


def resnorm_first(x_prompt, x_sample, y, g_post, g_next, *, tr):
    n_seq, seq_len, d = x_prompt.shape
    n_dec = x_sample.shape[1]
    g_post, g_next = g_post.reshape(1, d), g_next.reshape(1, d)
    shapes = (jax.ShapeDtypeStruct(y.shape, F32), jax.ShapeDtypeStruct(y.shape, BF16))
    row = pl.BlockSpec((1, tr, d), lambda s, r: (s, r, 0))
    vec = pl.BlockSpec((1, d), lambda s, r: (0, 0))
    xo, ho = pl.pallas_call(
        _resnorm_kernel, out_shape=shapes, grid=(n_seq, seq_len // tr),
        in_specs=[row, row, vec, vec], out_specs=(row, row),
        compiler_params=_params("parallel", "parallel"), name="resnorm_first_prompt",
    )(x_prompt, y, g_post, g_next)
    sample_rows = pl.BlockSpec((1, n_dec, d), lambda s: (s, seq_len // n_dec, 0))
    vec1 = pl.BlockSpec((1, d), lambda s: (0, 0))
    return pl.pallas_call(
        _skip_aliased(_resnorm_kernel, 2), out_shape=shapes, grid=(n_seq,),
        in_specs=[pl.BlockSpec(memory_space=pl.ANY), pl.BlockSpec(memory_space=pl.ANY),
                  pl.BlockSpec((1, n_dec, d), lambda s: (0, 0, 0)),
                  pl.BlockSpec((1, n_dec, d), lambda s: (0, seq_len // n_dec, 0)), vec1, vec1],
        out_specs=(sample_rows, sample_rows),
        input_output_aliases={0: 0, 1: 1},
        compiler_params=_params("arbitrary"), name="resnorm_first_sample",
    )(xo, ho, x_sample, y, g_post, g_next)


def resnorm_tiled(x, y, g_post, g_next, *, tr):
    n_seq, rows, d = x.shape
    row = pl.BlockSpec((1, tr, d), lambda s, r: (s, r, 0))
    vec = pl.BlockSpec((1, d), lambda s, r: (0, 0))
    return pl.pallas_call(
        _resnorm_kernel,
        out_shape=(jax.ShapeDtypeStruct(x.shape, F32), jax.ShapeDtypeStruct(x.shape, BF16)),
        grid=(n_seq, rows // tr),
        in_specs=[row, row, vec, vec], out_specs=(row, row),
        compiler_params=_params("parallel", "parallel"), name="resnorm_tiled",
    )(x, y, g_post.reshape(1, d), g_next.reshape(1, d))


def resnorm_last(x, y, g_post, *, seq_len, n_dec, tr):
    n_seq, _, d = x.shape
    g_post = g_post.reshape(1, d)
    row = pl.BlockSpec((1, tr, d), lambda s, r: (s, r, 0))
    y_prompt = pl.pallas_call(
        _resnorm_last_kernel, out_shape=jax.ShapeDtypeStruct((n_seq, seq_len, d), F32),
        grid=(n_seq, seq_len // tr),
        in_specs=[row, row, pl.BlockSpec((1, d), lambda s, r: (0, 0))], out_specs=row,
        compiler_params=_params("parallel", "parallel"), name="resnorm_last_prompt",
    )(x, y, g_post)
    sample_rows = pl.BlockSpec((1, n_dec, d), lambda s: (0, seq_len // n_dec, 0))
    y_sample = pl.pallas_call(
        _resnorm_last_kernel, out_shape=jax.ShapeDtypeStruct((1, n_dec, d), F32), grid=(1,),
        in_specs=[sample_rows, sample_rows, pl.BlockSpec((1, d), lambda s: (0, 0))],
        out_specs=pl.BlockSpec((1, n_dec, d), lambda s: (0, 0, 0)),
        compiler_params=_params("arbitrary"), name="resnorm_last_sample",
    )(x, y, g_post)
    return y_prompt, y_sample.reshape(n_dec, 1, d)


def _copy_rows_kernel(src_ref, o_ref):
    o_ref[0] = src_ref[...]


def insert_sample_rows(tiled, rows, *, seq_len):
    n_seq, _, c = tiled.shape
    n_dec = rows.shape[0]
    return pl.pallas_call(
        _skip_aliased(_copy_rows_kernel, 1), out_shape=jax.ShapeDtypeStruct(tiled.shape, tiled.dtype),
        grid=(n_seq,),
        in_specs=[pl.BlockSpec(memory_space=pl.ANY), pl.BlockSpec((n_dec, c), lambda s: (0, 0))],
        out_specs=pl.BlockSpec((1, n_dec, c), lambda s: (s, seq_len // n_dec, 0)),
        input_output_aliases={0: 0},
        compiler_params=_params("arbitrary"), name="insert_sample_rows",
    )(tiled, rows)


def _resident_row_tile(x_hbm, xbuf_ref, sem, *, row_tiles):
    i = pl.program_id(0)
    j = pl.program_id(1)
    tm = xbuf_ref.shape[1]

    def tile_copy(t):
        rows = pl.ds(pl.multiple_of((t % row_tiles) * tm, 2 * V7X_SUBLANES), tm)
        return pltpu.make_async_copy(x_hbm.at[t // row_tiles, rows, :], xbuf_ref.at[t % 2], sem.at[t % 2])

    @pl.when(j == 0)
    def _():
        @pl.when(i == 0)
        def _():
            tile_copy(0).start()

        @pl.when(i + 1 < pl.num_programs(0))
        def _():
            tile_copy(i + 1).start()

        tile_copy(i).wait()

    return xbuf_ref.at[i % 2]


def _matmul_kernel(*refs, epilogue, has_bias, row_tiles):
    if has_bias:
        x_hbm, w_ref, b_ref, o_ref, xbuf_ref, sem = refs
    else:
        x_hbm, w_ref, o_ref, xbuf_ref, sem = refs
    x_ref = _resident_row_tile(x_hbm, xbuf_ref, sem, row_tiles=row_tiles)
    for r0, r1 in _row_chunks(x_ref.shape[0]):
        acc = _dot_bf16(x_ref[r0:r1], w_ref[...])
        if has_bias:
            acc = acc + b_ref[...]
        o_ref[0, r0:r1] = epilogue(acc).astype(o_ref.dtype)


def matmul(x, w, bias=None, *, layer, row_tiles, tn, n_out, col_block_offset=0, out_dtype=F32,
           epilogue=_identity, name):
    n_seq, rows, k = x.shape
    tm = rows // row_tiles
    in_specs = [pl.BlockSpec(memory_space=pl.ANY),
                pl.BlockSpec((None, k, tn), lambda i, j: (layer, 0, j + col_block_offset))]
    args = [x, w]
    if bias is not None:
        in_specs.append(pl.BlockSpec((None, 1, tn), lambda i, j: (layer, 0, j + col_block_offset)))
        args.append(bias.reshape(bias.shape[0], 1, -1))
    return pl.pallas_call(
        functools.partial(_matmul_kernel, epilogue=epilogue, has_bias=bias is not None,
                          row_tiles=row_tiles),
        out_shape=jax.ShapeDtypeStruct((n_seq, rows, n_out), out_dtype),
        grid=(n_seq * row_tiles, n_out // tn),
        in_specs=in_specs,
        out_specs=pl.BlockSpec((1, tm, tn), lambda i, j: (i // row_tiles, i % row_tiles, j)),
        scratch_shapes=[pltpu.VMEM((2, tm, k), x.dtype), pltpu.SemaphoreType.DMA((2,))],
        compiler_params=_params("arbitrary", "arbitrary"),
        name=name,
    )(*args)


def _matmul_wres_kernel(x_ref, w_hbm, o_ref, wb_ref, stage_ref, sem, *, layer, parts, steps_per_part):
    j = pl.program_id(0)
    i = pl.program_id(1)
    n_col_tiles = pl.num_programs(0)
    _, k, tn = wb_ref.shape
    kp = k // parts

    def part_copy(col_tile, part):
        rows = pl.ds(pl.multiple_of(part * kp, V7X_SUBLANES), kp)
        cols = pl.ds(pl.multiple_of(col_tile * tn, V7X_LANES), tn)
        return pltpu.make_async_copy(w_hbm.at[layer, rows, cols], stage_ref, sem.at[0])

    def round_part(col_tile, part):
        rows = pl.ds(pl.multiple_of(part * kp, 2 * V7X_SUBLANES), kp)
        wb_ref[col_tile % 2, rows, :] = stage_ref[...].astype(BF16)

    @pl.when((j == 0) & (i == 0))
    def _():
        for part in range(parts):
            part_copy(0, part).start()
            part_copy(0, part).wait()
            round_part(0, part)

    part = i // steps_per_part
    phase = i % steps_per_part
    prefetching = (j + 1 < n_col_tiles) & (part < parts)

    @pl.when(prefetching & (phase == 0))
    def _():
        part_copy(j + 1, part).start()

    @pl.when(prefetching & (phase == steps_per_part - 2))
    def _():
        part_copy(j + 1, part).wait()
        round_part(j + 1, part)

    o_ref[0] = jnp.dot(x_ref[0], wb_ref[j % 2], preferred_element_type=F32).astype(o_ref.dtype)


def matmul_weight_resident(x, w, *, layer, tm, tn, parts, name):
    n_seq, rows, k = x.shape
    n = w.shape[2]
    per_seq = rows // tm
    n_row_tiles = n_seq * per_seq
    steps_per_part = n_row_tiles // parts
    assert steps_per_part >= 2 and k % (parts * 2 * V7X_SUBLANES) == 0
    return pl.pallas_call(
        functools.partial(_matmul_wres_kernel, layer=layer, parts=parts, steps_per_part=steps_per_part),
        out_shape=jax.ShapeDtypeStruct((n_seq, rows, n), BF16),
        grid=(n // tn, n_row_tiles),
        in_specs=[pl.BlockSpec((1, tm, k), lambda j, i: (i // per_seq, i % per_seq, 0)),
                  pl.BlockSpec(memory_space=pl.ANY)],
        out_specs=pl.BlockSpec((1, tm, tn), lambda j, i: (i // per_seq, i % per_seq, j)),
        scratch_shapes=[pltpu.VMEM((2, k, tn), BF16), pltpu.VMEM((k // parts, tn), F32),
                        pltpu.SemaphoreType.DMA((1,))],
        compiler_params=_params("arbitrary", "arbitrary"),
        name=name,
    )(x, w)


def _layer_norm(v, g, b):
    xc = v - jnp.mean(v, axis=-1, keepdims=True)
    return xc * lax.rsqrt(jnp.mean(xc * xc, axis=-1, keepdims=True) + NORM_EPS) * g + b


def _spatial_kernel(v_ref, u_ref, lng_ref, lnb_ref, ws_ref, bias_ref, o_ref, gv_ref, *, group_dim):
    t = lax.broadcasted_iota(jnp.int32, (CHUNK, CHUNK), 0)
    s = lax.broadcasted_iota(jnp.int32, (CHUNK, CHUNK), 1)
    causal = s <= t
    for c in range(v_ref.shape[1] // CHUNK):
        rows = slice(c * CHUNK, (c + 1) * CHUNK)
        vn = _layer_norm(v_ref[0, rows], lng_ref[...], lnb_ref[...])
        gv_ref[0] = vn
        vb = vn.astype(BF16)
        for g in range(N_GROUPS):
            cols = slice(g * group_dim, (g + 1) * group_dim)
            wc = jnp.where(causal, ws_ref[g], 0.0).astype(BF16)
            mix = jnp.dot(wc, vb[:, cols], preferred_element_type=F32) + bias_ref[:, cols]
            o_ref[0, rows, cols] = (u_ref[0, rows, cols].astype(F32) * mix).astype(o_ref.dtype)


def _spatial_sample_kernel(v_ref, u_ref, lng_ref, lnb_ref, w00_ref, b0_ref, o_ref, gv_ref):
    vn = _layer_norm(v_ref[0], lng_ref[...], lnb_ref[...])
    gv_ref[0] = vn
    mix = w00_ref[...].astype(BF16).astype(F32) * vn.astype(BF16).astype(F32) + b0_ref[...]
    o_ref[0] = (u_ref[0].astype(F32) * mix).astype(o_ref.dtype)


def spatial_gate(v, u, ln_g, ln_b, w_s, b_s, *, seq_len):
    n_seq, rows, d = v.shape
    n_dec = rows - seq_len
    group_dim = d // N_GROUPS
    ln_g, ln_b = ln_g.reshape(1, d), ln_b.reshape(1, d)
    bias_full = jnp.repeat(b_s.T, group_dim, axis=1)
    block_rows = SPATIAL_CHUNKS_PER_STEP * CHUNK
    row = pl.BlockSpec((1, block_rows, d), lambda s, c: (s, c, 0))
    vec = pl.BlockSpec((1, d), lambda s, c: (0, 0))
    gated, gv_prompt = pl.pallas_call(
        functools.partial(_spatial_kernel, group_dim=group_dim),
        out_shape=(jax.ShapeDtypeStruct(v.shape, BF16),
                   jax.ShapeDtypeStruct((n_seq, CHUNK, d), F32)),
        grid=(n_seq, seq_len // block_rows),
        in_specs=[row, row, vec, vec,
                  pl.BlockSpec((N_GROUPS, CHUNK, CHUNK), lambda s, c: (0, 0, 0)),
                  pl.BlockSpec((CHUNK, d), lambda s, c: (0, 0))],
        out_specs=(row, pl.BlockSpec((1, CHUNK, d), lambda s, c: (s, 0, 0))),
        compiler_params=_params("parallel", "arbitrary"),
        name="spatial_gate_prompt",
    )(v, u, ln_g, ln_b, w_s, bias_full)
    w00 = jnp.repeat(w_s[:, 0, 0], group_dim).reshape(1, d)
    b0 = jnp.repeat(b_s[:, 0], group_dim).reshape(1, d)
    tile0_rows = pl.BlockSpec((1, n_dec, d), lambda s: (0, seq_len // n_dec, 0))
    vec1 = pl.BlockSpec((1, d), lambda s: (0, 0))
    gated, gv_sample = pl.pallas_call(
        _skip_aliased(_spatial_sample_kernel, 1),
        out_shape=(jax.ShapeDtypeStruct(v.shape, BF16), jax.ShapeDtypeStruct((1, n_dec, d), F32)),
        grid=(n_seq,),
        in_specs=[pl.BlockSpec(memory_space=pl.ANY), tile0_rows, tile0_rows, vec1, vec1, vec1, vec1],
        out_specs=(pl.BlockSpec((1, n_dec, d), lambda s: (s, seq_len // n_dec, 0)),
                   pl.BlockSpec((1, n_dec, d), lambda s: (0, 0, 0))),
        input_output_aliases={0: 0},
        compiler_params=_params("arbitrary"),
        name="spatial_gate_sample",
    )(gated, v, u, ln_g, ln_b, w00, b0)
    return gated, gv_prompt, gv_sample


def _pair_block_diag(pair, even):
    lane = lax.broadcasted_iota(jnp.int32, pair.shape, 1)
    if even:
        own = jnp.where(lane < HEAD_DIM, pair, 0.0)
        return jnp.concatenate([own, pltpu.roll(own, HEAD_DIM, 1)], axis=0)
    own = jnp.where(lane >= HEAD_DIM, pair, 0.0)
    return jnp.concatenate([pltpu.roll(own, HEAD_DIM, 1), own], axis=0)


def _attn_prompt_kernel(sink_ref, q_ref, kprev_ref, kown_ref, vprev_ref, vown_ref, o_ref):
    blk = pl.program_id(1)
    n_keys = 2 * WINDOW
    qi = lax.broadcasted_iota(jnp.int32, (WINDOW, n_keys), 0)
    kj = lax.broadcasted_iota(jnp.int32, (WINDOW, n_keys), 1)
    first_key = jnp.where(blk > 0, 0, WINDOW)
    valid = (kj >= qi) & (kj <= qi + WINDOW) & (kj >= first_key)
    col_blocks = Q_PER_KV * HEAD_DIM // V7X_LANES
    valid = jnp.concatenate([valid] * col_blocks, axis=0)
    block_of_row = lax.broadcasted_iota(jnp.int32, (col_blocks * WINDOW, 1), 0) // WINDOW
    head_lane = lax.broadcasted_iota(jnp.int32, (col_blocks * WINDOW, V7X_LANES), 1)
    group_w = Q_PER_KV * HEAD_DIM
    for pair in range(N_KV_HEADS // 2):
        lanes = slice(pair * V7X_LANES, (pair + 1) * V7X_LANES)
        k_pair = jnp.concatenate([kprev_ref[0, :, lanes], kown_ref[0, :, lanes]], axis=0)
        v_pair = jnp.concatenate([vprev_ref[0, :, lanes], vown_ref[0, :, lanes]], axis=0)
        for e in range(2):
            h = 2 * pair + e
            k2 = _pair_block_diag(k_pair, e == 0).astype(BF16)
            v2 = _pair_block_diag(v_pair, e == 0).astype(BF16)
            qs = jnp.concatenate(
                [q_ref[0, :, h * group_w + c * V7X_LANES: h * group_w + (c + 1) * V7X_LANES]
                 for c in range(col_blocks)], axis=0)
            qs = (qs.astype(F32) * (ATTN_SCALE * LOG2_E)).astype(BF16)
            s = lax.dot_general(qs, k2, (((1,), (1,)), ((), ())), preferred_element_type=F32)
            probs, inv_denoms = [], []
            for half in range(2):
                sh = jnp.where(valid, s[:, half * n_keys:(half + 1) * n_keys], MASKED_SCORE)
                sink = jnp.zeros((col_blocks * WINDOW, 1), F32)
                for c in range(col_blocks):
                    sink = jnp.where(block_of_row == c, sink_ref[h * Q_PER_KV + 2 * c + half] * LOG2_E,
                                     sink)
                m = jnp.maximum(jnp.max(sh, axis=-1, keepdims=True), sink)
                p = jnp.exp2(sh - m)
                inv_denoms.append(1.0 / (jnp.sum(p, axis=-1, keepdims=True) + jnp.exp2(sink - m)))
                probs.append(p.astype(BF16))
            o = jnp.dot(jnp.concatenate(probs, axis=1), v2, preferred_element_type=F32)
            o = o * jnp.where(head_lane < HEAD_DIM, inv_denoms[0], inv_denoms[1])
            for c in range(col_blocks):
                o_ref[0, :, h * group_w + c * V7X_LANES: h * group_w + (c + 1) * V7X_LANES] = (
                    o[c * WINDOW:(c + 1) * WINDOW].astype(o_ref.dtype))


def attention_prompt(q, kv, sinks, *, seq_len):
    n_seq, rows, d_q = q.shape
    d_kv = N_KV_HEADS * HEAD_DIM
    k_col, v_col = 0, 1

    def own(col):
        return lambda s, i: (s, i, col)

    def prev(col):
        return lambda s, i: (s, jnp.maximum(i - 1, 0), col)

    return pl.pallas_call(
        _attn_prompt_kernel,
        out_shape=jax.ShapeDtypeStruct((n_seq, rows, d_q), BF16),
        grid=(n_seq, seq_len // WINDOW),
        in_specs=[pl.BlockSpec(memory_space=pltpu.SMEM),
                  pl.BlockSpec((1, WINDOW, d_q), own(0)),
                  pl.BlockSpec((1, WINDOW, d_kv), prev(k_col)),
                  pl.BlockSpec((1, WINDOW, d_kv), own(k_col)),
                  pl.BlockSpec((1, WINDOW, d_kv), prev(v_col)),
                  pl.BlockSpec((1, WINDOW, d_kv), own(v_col))],
        out_specs=pl.BlockSpec((1, WINDOW, d_q), own(0)),
        compiler_params=_params("parallel", "arbitrary"),
        name="attention_prompt",
    )(sinks, q, kv, kv, kv, kv)


def _attn_sample_kernel(q2_ref, ck_ref, cv_ref, knew_ref, vnew_ref, sink_ref, o_ref):
    for pair in range(N_KV_HEADS // 2):
        lanes = slice(pair * V7X_LANES, (pair + 1) * V7X_LANES)
        q2 = q2_ref[0, pair]
        k_pair = ck_ref[0, :, lanes].astype(BF16)
        v_pair = cv_ref[0, :, lanes].astype(BF16)
        k_new = knew_ref[0, :, lanes].astype(BF16).astype(F32)
        v_new = vnew_ref[0, :, lanes].astype(BF16).astype(F32)
        sink = sink_ref[pair][:, :1]
        s = lax.dot_general(q2, k_pair, (((1,), (1,)), ((), ())), preferred_element_type=F32)
        s_new = jnp.sum(q2.astype(F32) * k_new, axis=-1, keepdims=True)
        m = jnp.maximum(jnp.maximum(jnp.max(s, axis=-1, keepdims=True), s_new), sink)
        p = jnp.exp(s - m)
        p_new = jnp.exp(s_new - m)
        denom = jnp.sum(p, axis=-1, keepdims=True) + p_new + jnp.exp(sink - m)
        o = jnp.dot((p / denom).astype(BF16), v_pair, preferred_element_type=F32)
        o_ref[0, pair] = o + (p_new / denom).astype(BF16).astype(F32) * v_new


def attention_sample(q, kv, cache_k, cache_v, sinks):
    n, d_q = q.shape
    d_kv = N_KV_HEADS * HEAD_DIM
    n_pairs = N_KV_HEADS // 2
    rows = 2 * Q_PER_KV
    q = (q.astype(F32) * ATTN_SCALE).astype(BF16).reshape(n, n_pairs, 2, Q_PER_KV, HEAD_DIM)
    zeros = jnp.zeros((n, n_pairs, Q_PER_KV, HEAD_DIM), BF16)
    q2 = jnp.concatenate([jnp.concatenate([q[:, :, 0], zeros], axis=-1),
                          jnp.concatenate([zeros, q[:, :, 1]], axis=-1)], axis=2)
    k_new = kv[:, :d_kv].reshape(n, 1, d_kv)
    v_new = kv[:, d_kv:].reshape(n, 1, d_kv)
    sink2 = jnp.broadcast_to(sinks.reshape(n_pairs, rows, 1), (n_pairs, rows, V7X_LANES))
    o2 = pl.pallas_call(
        _attn_sample_kernel,
        out_shape=jax.ShapeDtypeStruct((n, n_pairs, rows, V7X_LANES), F32),
        grid=(n,),
        in_specs=[pl.BlockSpec((1, n_pairs, rows, V7X_LANES), lambda b: (b, 0, 0, 0)),
                  pl.BlockSpec((1, WINDOW, d_kv), lambda b: (b, 0, 0)),
                  pl.BlockSpec((1, WINDOW, d_kv), lambda b: (b, 0, 0)),
                  pl.BlockSpec((1, 1, d_kv), lambda b: (b, 0, 0)),
                  pl.BlockSpec((1, 1, d_kv), lambda b: (b, 0, 0)),
                  pl.BlockSpec((n_pairs, rows, V7X_LANES), lambda b: (0, 0, 0))],
        out_specs=pl.BlockSpec((1, n_pairs, rows, V7X_LANES), lambda b: (b, 0, 0, 0)),
        compiler_params=_params("parallel"),
        name="attention_sample",
    )(q2, cache_k.reshape(n, WINDOW, d_kv), cache_v.reshape(n, WINDOW, d_kv), k_new, v_new, sink2)
    o = jnp.stack([o2[:, :, :Q_PER_KV, :HEAD_DIM], o2[:, :, Q_PER_KV:, HEAD_DIM:]], axis=2)
    return o.reshape(n, d_q).astype(BF16), k_new, v_new


def _silu_gate(gate, val):
    return gate * (1.0 / (1.0 + jnp.exp(-gate))) * val


def _ffn_up_kernel(x_hbm, wg_ref, wv_ref, cwg_ref, cwv_ref, cbg_ref, cbv_ref,
                   p0g_ref, p0v_ref, p1g_ref, p1v_ref,
                   a_ref, sg_ref, sv_ref, hsg_ref, hsv_ref, xbuf_ref, sem, h_ref, tail_ref,
                   *, halves, n_dec):
    half = pl.program_id(0) % halves
    j = pl.program_id(1)
    tn = wg_ref.shape[1]
    tail_rows = tail_ref.shape[1]
    x_ref = _resident_row_tile(x_hbm, xbuf_ref, sem, row_tiles=halves)
    tm = x_ref.shape[0]
    gate_cols, val_cols = slice(0, tn), slice(tn, 2 * tn)

    @pl.when((pl.program_id(0) == 0) & (j == 0))
    def _():
        tail_ref[...] = jnp.zeros(tail_ref.shape, tail_ref.dtype)

    first_tile = jnp.full((tail_rows, 2 * tn), half, jnp.int32) == 0
    h_ref[0:tail_rows] = jnp.where(first_tile, 0.0, tail_ref[j])

    def conv(r0, r1, cols, cw_ref, cb_ref):
        ext = h_ref[r0:r1 + tail_rows, cols]
        h1 = pltpu.roll(ext, 1, 0)[tail_rows:]
        h2 = pltpu.roll(ext, 2, 0)[tail_rows:]
        return cb_ref[...] + cw_ref[0:1] * h2 + cw_ref[1:2] * h1 + cw_ref[2:3] * ext[tail_rows:]

    def epilogue(r0, r1):
        a_ref[0, r0:r1] = _silu_gate(conv(r0, r1, gate_cols, cwg_ref, cbg_ref),
                                     conv(r0, r1, val_cols, cwv_ref, cbv_ref)).astype(a_ref.dtype)

    pending = None
    for r0, r1 in _row_chunks(tm):
        x = x_ref[r0:r1]
        h_ref[tail_rows + r0:tail_rows + r1, gate_cols] = _dot_bf16(x, wg_ref[...])
        h_ref[tail_rows + r0:tail_rows + r1, val_cols] = _dot_bf16(x, wv_ref[...])
        if pending is not None:
            epilogue(*pending)
        pending = (r0, r1)
    epilogue(*pending)
    tail_ref[j] = h_ref[tm:tm + tail_rows]

    h_sample = h_ref[tail_rows + tm - n_dec:tail_rows + tm]
    hsg_ref[0, 0] = h_sample[:, gate_cols]
    hsv_ref[0, 0] = h_sample[:, val_cols]
    seq_tail = h_ref[tm - n_dec:tm - n_dec + tail_rows]
    sg_ref[0, 0] = seq_tail[:, gate_cols]
    sv_ref[0, 0] = seq_tail[:, val_cols]

    @pl.when(half == halves - 1)
    def _():
        def conv_sample(cols, p0_ref, p1_ref, cw_ref, cb_ref):
            return (cb_ref[...] + cw_ref[0:1] * p0_ref[...] + cw_ref[1:2] * p1_ref[...]
                    + cw_ref[2:3] * h_sample[:, cols])

        a_ref[0, tm - n_dec:tm] = _silu_gate(
            conv_sample(gate_cols, p0g_ref, p1g_ref, cwg_ref, cbg_ref),
            conv_sample(val_cols, p0v_ref, p1v_ref, cwv_ref, cbv_ref)).astype(a_ref.dtype)


def ffn_up(h, w_up, conv_w, conv_b, state, *, layer, seq_len, halves, tn, tail_rows):
    n_seq, rows, k = h.shape
    n_dec = rows - seq_len
    d_ff = w_up.shape[2] // 2
    nj = d_ff // tn
    tm = rows // halves
    conv_b = conv_b.reshape(conv_b.shape[0], 1, -1)
    p0, p1 = state[:, 0], state[:, 1]

    def gate(shape):
        return pl.BlockSpec(shape, lambda i, j: (0, j))

    def val(shape):
        return pl.BlockSpec(shape, lambda i, j: (0, j + nj))

    def gate_l(shape):
        return pl.BlockSpec((None,) + shape, lambda i, j: (layer, 0, j))

    def val_l(shape):
        return pl.BlockSpec((None,) + shape, lambda i, j: (layer, 0, j + nj))

    per_tile = lambda r: pl.BlockSpec((1, 1, r, tn), lambda i, j: (i // halves, i % halves, 0, j))
    a, sg, sv, hsg, hsv = pl.pallas_call(
        functools.partial(_ffn_up_kernel, halves=halves, n_dec=n_dec),
        out_shape=(jax.ShapeDtypeStruct((n_seq, rows, d_ff), BF16),
                   jax.ShapeDtypeStruct((n_seq, halves, tail_rows, d_ff), F32),
                   jax.ShapeDtypeStruct((n_seq, halves, tail_rows, d_ff), F32),
                   jax.ShapeDtypeStruct((n_seq, halves, n_dec, d_ff), F32),
                   jax.ShapeDtypeStruct((n_seq, halves, n_dec, d_ff), F32)),
        grid=(n_seq * halves, nj),
        in_specs=[pl.BlockSpec(memory_space=pl.ANY),
                  gate_l((k, tn)), val_l((k, tn)), gate_l((CONV_W, tn)), val_l((CONV_W, tn)),
                  gate_l((1, tn)), val_l((1, tn)),
                  gate((n_dec, tn)), val((n_dec, tn)), gate((n_dec, tn)), val((n_dec, tn))],
        out_specs=(pl.BlockSpec((1, tm, tn), lambda i, j: (i // halves, i % halves, j)),
                   per_tile(tail_rows), per_tile(tail_rows), per_tile(n_dec), per_tile(n_dec)),
        scratch_shapes=[pltpu.VMEM((2, tm, k), h.dtype), pltpu.SemaphoreType.DMA((2,)),
                        pltpu.VMEM((tail_rows + tm, 2 * tn), F32),
                        pltpu.VMEM((nj, tail_rows, 2 * tn), F32)],
        compiler_params=_params("arbitrary", "arbitrary"),
        name="ffn_up",
    )(h, w_up, w_up, conv_w, conv_w, conv_b, conv_b, p0, p0, p1, p1)
    keep = slice(tail_rows - (CONV_W - 1), tail_rows)
    state_prompt = jnp.concatenate([sg[:, halves - 1, keep], sv[:, halves - 1, keep]], axis=-1)
    h_sample = jnp.concatenate([hsg[0, halves - 1], hsv[0, halves - 1]], axis=-1)
    return a, state_prompt, h_sample


ROW_TILES = 2
TN = 512
TN_FFN_UP = V7X_MXU_COLS
TR_PROMPT = 512
TILED_ROW_BLOCKS = 5
FFN_DOWN_ROW_BLOCKS = 5
FFN_DOWN_WEIGHT_PARTS = 4
GMLP_OUT_WEIGHT_PARTS = 2
STATE_TAIL_ROWS = V7X_SUBLANES
SPATIAL_CHUNKS_PER_STEP = 2


def kernel(x_prompt, x_sample, cache_win_k, cache_win_v, state_conv, norm_mix_pre, norm_mix_post,
           norm_ffn_pre, norm_ffn_post, gmlp_w_in, gmlp_ln_g, gmlp_ln_b, gmlp_w_s, gmlp_b_s,
           gmlp_w_out, attn_w_qkv, attn_b_qkv, attn_sinks, attn_w_o, attn_b_o, ffn_w_up,
           ffn_conv_w, ffn_conv_b, ffn_w_down):
    n_seq, seq_len, d = x_prompt.shape
    n_dec = x_sample.shape[0]
    depth = norm_mix_pre.shape[0]
    rows = seq_len + n_dec
    x_sample = x_sample.reshape(1, n_dec, d)
    mm = functools.partial(matmul, row_tiles=ROW_TILES, tn=TN)

    h = rmsnorm_first(x_prompt, x_sample, norm_mix_pre[0], tr=TR_PROMPT)
    x = None
    gv_p, gv_s, wk_p, wv_p, wk_s, wv_s, cv_p, cv_s = [], [], [], [], [], [], [], []
    for layer in range(depth):
        idx = layer // 2
        if layer % 2 == 0:
            d_g = gmlp_w_out.shape[1]
            u = mm(h, gmlp_w_in, layer=idx, n_out=d_g, out_dtype=BF16, epilogue=_gelu, name="gmlp_in_u")
            v = mm(h, gmlp_w_in, layer=idx, n_out=d_g, col_block_offset=d_g // TN, epilogue=_gelu,
                   name="gmlp_in_v")
            gated, g_p, g_s = spatial_gate(v, u, gmlp_ln_g[idx], gmlp_ln_b[idx], gmlp_w_s[idx],
                                           gmlp_b_s[idx], seq_len=seq_len)
            gv_p.append(g_p)
            gv_s.append(g_s.reshape(n_dec, 1, d_g))
            y = matmul_weight_resident(gated, gmlp_w_out, layer=idx, tm=rows // ROW_TILES, tn=TN,
                                       parts=GMLP_OUT_WEIGHT_PARTS, name="gmlp_out")
        else:
            d_q = attn_w_o.shape[1]
            d_qkv = attn_w_qkv.shape[2]
            d_kv = (d_qkv - d_q) // 2
            q = mm(h, attn_w_qkv, attn_b_qkv, layer=idx, n_out=d_q, out_dtype=BF16, name="attn_q")
            kv = mm(h, attn_w_qkv, attn_b_qkv, layer=idx, n_out=2 * d_kv, col_block_offset=d_q // TN,
                    name="attn_kv")
            o = attention_prompt(q, kv, attn_sinks[idx], seq_len=seq_len)
            o_s, k_new, v_new = attention_sample(q[0, seq_len:], kv[0, seq_len:], cache_win_k[idx],
                                                 cache_win_v[idx], attn_sinks[idx])
            o = insert_sample_rows(o, o_s, seq_len=seq_len)
            tail = kv[:, seq_len - WINDOW:seq_len]
            wk_p.append(tail[:, :, :d_kv].reshape(n_seq, WINDOW, N_KV_HEADS, HEAD_DIM))
            wv_p.append(tail[:, :, d_kv:].reshape(n_seq, WINDOW, N_KV_HEADS, HEAD_DIM))
            wk_s.append(jnp.concatenate(
                [cache_win_k[idx][:, 1:], k_new.reshape(n_dec, 1, N_KV_HEADS, HEAD_DIM)], axis=1))
            wv_s.append(jnp.concatenate(
                [cache_win_v[idx][:, 1:], v_new.reshape(n_dec, 1, N_KV_HEADS, HEAD_DIM)], axis=1))
            y = mm(o, attn_w_o, attn_b_o, layer=idx, n_out=d, out_dtype=BF16, name="attn_out")
        if x is None:
            x, h = resnorm_first(x_prompt, x_sample, y, norm_mix_post[layer], norm_ffn_pre[layer],
                                 tr=TR_PROMPT)
        else:
            x, h = resnorm_tiled(x, y, norm_mix_post[layer], norm_ffn_pre[layer],
                                 tr=rows // TILED_ROW_BLOCKS)

        a, c_p, hu_s = ffn_up(h, ffn_w_up, ffn_conv_w, ffn_conv_b, state_conv[layer], layer=layer,
                              seq_len=seq_len, halves=ROW_TILES, tn=TN_FFN_UP, tail_rows=STATE_TAIL_ROWS)
        cv_p.append(c_p)
        cv_s.append(jnp.concatenate([state_conv[layer][:, 1:], hu_s[:, None]], axis=1))
        f = matmul_weight_resident(a, ffn_w_down, layer=layer, tm=rows // FFN_DOWN_ROW_BLOCKS, tn=TN,
                                   parts=FFN_DOWN_WEIGHT_PARTS, name="ffn_down")
        if layer + 1 < depth:
            x, h = resnorm_tiled(x, f, norm_ffn_post[layer], norm_mix_pre[layer + 1],
                                 tr=rows // TILED_ROW_BLOCKS)
        else:
            y_prompt, y_sample = resnorm_last(x, f, norm_ffn_post[layer], seq_len=seq_len, n_dec=n_dec,
                                              tr=TR_PROMPT)

    return (y_prompt, y_sample, jnp.stack(gv_p), jnp.stack(gv_s),
            jnp.stack(wk_p), jnp.stack(wv_p), jnp.stack(wk_s), jnp.stack(wv_s),
            jnp.stack(cv_p), jnp.stack(cv_s))
```

```python
import functools
import math

import jax
import jax.numpy as jnp
from jax import lax
from jax.experimental import pallas as pl
from jax.experimental.pallas import tpu as pltpu

F32 = jnp.float32
BF16 = jnp.bfloat16

NORM_EPS = 1e-6
CHUNK = 128
N_GROUPS = 16
HEAD_DIM = 64
N_KV_HEADS = 8
Q_PER_KV = 8
WINDOW = 128
CONV_W = 3
ATTN_SCALE = HEAD_DIM ** -0.5
MASKED_SCORE = -1e30
LOG2_E = math.log2(math.e)

V7X_LANES = 128
V7X_SUBLANES = 8
V7X_MXU_COLS = 256
V7X_SCOPED_VMEM_BYTES = 60000 * 1024

DOT_ROWS = 512


def _params(*semantics):
    return pltpu.CompilerParams(dimension_semantics=semantics,
                                vmem_limit_bytes=V7X_SCOPED_VMEM_BYTES)


def _rms(x, g):
    return x * lax.rsqrt(jnp.mean(x * x, axis=-1, keepdims=True) + NORM_EPS) * g


def _gelu(x):
    return 0.5 * x * (1.0 + lax.erf(x * math.sqrt(0.5)))


def _identity(x):
    return x


def _dot_bf16(x, w):
    return jnp.dot(x, w.astype(BF16), preferred_element_type=F32)


def _row_chunks(rows, size=DOT_ROWS):
    n = max(rows // size, 1)
    return [(c * size, (c + 1) * size if c + 1 < n else rows) for c in range(n)]


def _skip_aliased(body, n_aliased):
    def kernel_fn(*refs):
        body(*refs[n_aliased:])
    return kernel_fn


def _rmsnorm_kernel(x_ref, g_ref, h_ref):
    h_ref[0] = _rms(x_ref[0], g_ref[...]).astype(h_ref.dtype)


def _resnorm_kernel(x_ref, y_ref, gpost_ref, gnext_ref, xo_ref, ho_ref):
    xn = x_ref[0] + _rms(y_ref[0].astype(F32), gpost_ref[...])
    xo_ref[0] = xn
    ho_ref[0] = _rms(xn, gnext_ref[...]).astype(ho_ref.dtype)


def _resnorm_last_kernel(x_ref, y_ref, gpost_ref, xo_ref):
    xo_ref[0] = x_ref[0] + _rms(y_ref[0].astype(F32), gpost_ref[...])


def rmsnorm_first(x_prompt, x_sample, g, *, tr):
    n_seq, seq_len, d = x_prompt.shape
    n_dec = x_sample.shape[1]
    g = g.reshape(1, d)
    shape = jax.ShapeDtypeStruct((n_seq, seq_len + n_dec, d), BF16)
    row = pl.BlockSpec((1, tr, d), lambda s, r: (s, r, 0))
    h = pl.pallas_call(
        _rmsnorm_kernel, out_shape=shape, grid=(n_seq, seq_len // tr),
        in_specs=[row, pl.BlockSpec((1, d), lambda s, r: (0, 0))], out_specs=row,
        compiler_params=_params("parallel", "parallel"), name="rmsnorm_first_prompt",
    )(x_prompt, g)
    return pl.pallas_call(
        _skip_aliased(_rmsnorm_kernel, 1), out_shape=shape, grid=(n_seq,),
        in_specs=[pl.BlockSpec(memory_space=pl.ANY),
                  pl.BlockSpec((1, n_dec, d), lambda s: (0, 0, 0)),
                  pl.BlockSpec((1, d), lambda s: (0, 0))],
        out_specs=pl.BlockSpec((1, n_dec, d), lambda s: (s, seq_len // n_dec, 0)),
        input_output_aliases={0: 0},
        compiler_params=_params("arbitrary"), name="rmsnorm_first_sample",
    )(h, x_sample, g)


def resnorm_first(x_prompt, x_sample, y, g_post, g_next, *, tr):
    n_seq, seq_len, d = x_prompt.shape
    n_dec = x_sample.shape[1]
    g_post, g_next = g_post.reshape(1, d), g_next.reshape(1, d)
    shapes = (jax.ShapeDtypeStruct(y.shape, F32), jax.ShapeDtypeStruct(y.shape, BF16))
    row = pl.BlockSpec((1, tr, d), lambda s, r: (s, r, 0))
    vec = pl.BlockSpec((1, d), lambda s, r: (0, 0))
    xo, ho = pl.pallas_call(
        _resnorm_kernel, out_shape=shapes, grid=(n_seq, seq_len // tr),
        in_specs=[row, row, vec, vec], out_specs=(row, row),
        compiler_params=_params("parallel", "parallel"), name="resnorm_first_prompt",
    )(x_prompt, y, g_post, g_next)
    sample_rows = pl.BlockSpec((1, n_dec, d), lambda s: (s, seq_len // n_dec, 0))
    vec1 = pl.BlockSpec((1, d), lambda s: (0, 0))
    return pl.pallas_call(
        _skip_aliased(_resnorm_kernel, 2), out_shape=shapes, grid=(n_seq,),
        in_specs=[pl.BlockSpec(memory_space=pl.ANY), pl.BlockSpec(memory_space=pl.ANY),
                  pl.BlockSpec((1, n_dec, d), lambda s: (0, 0, 0)),
                  pl.BlockSpec((1, n_dec, d), lambda s: (0, seq_len // n_dec, 0)), vec1, vec1],
        out_specs=(sample_rows, sample_rows),
        input_output_aliases={0: 0, 1: 1},
        compiler_params=_params("arbitrary"), name="resnorm_first_sample",
    )(xo, ho, x_sample, y, g_post, g_next)


def resnorm_tiled(x, y, g_post, g_next, *, tr):
    n_seq, rows, d = x.shape
    row = pl.BlockSpec((1, tr, d), lambda s, r: (s, r, 0))
    vec = pl.BlockSpec((1, d), lambda s, r: (0, 0))
    return pl.pallas_call(
        _resnorm_kernel,
        out_shape=(jax.ShapeDtypeStruct(x.shape, F32), jax.ShapeDtypeStruct(x.shape, BF16)),
        grid=(n_seq, rows // tr),
        in_specs=[row, row, vec, vec], out_specs=(row, row),
        compiler_params=_params("parallel", "parallel"), name="resnorm_tiled",
    )(x, y, g_post.reshape(1, d), g_next.reshape(1, d))


def resnorm_last(x, y, g_post, *, seq_len, n_dec, tr):
    n_seq, _, d = x.shape
    g_post = g_post.reshape(1, d)
    row = pl.BlockSpec((1, tr, d), lambda s, r: (s, r, 0))
    y_prompt = pl.pallas_call(
        _resnorm_last_kernel, out_shape=jax.ShapeDtypeStruct((n_seq, seq_len, d), F32),
        grid=(n_seq, seq_len // tr),
        in_specs=[row, row, pl.BlockSpec((1, d), lambda s, r: (0, 0))], out_specs=row,
        compiler_params=_params("parallel", "parallel"), name="resnorm_last_prompt",
    )(x, y, g_post)
    sample_rows = pl.BlockSpec((1, n_dec, d), lambda s: (0, seq_len // n_dec, 0))
    y_sample = pl.pallas_call(
        _resnorm_last_kernel, out_shape=jax.ShapeDtypeStruct((1, n_dec, d), F32), grid=(1,),
        in_specs=[sample_rows, sample_rows, pl.BlockSpec((1, d), lambda s: (0, 0))],
        out_specs=pl.BlockSpec((1, n_dec, d), lambda s: (0, 0, 0)),
        compiler_params=_params("arbitrary"), name="resnorm_last_sample",
    )(x, y, g_post)
    return y_prompt, y_sample.reshape(n_dec, 1, d)


def _copy_rows_kernel(src_ref, o_ref):
    o_ref[0] = src_ref[...]


def insert_sample_rows(tiled, rows, *, seq_len):
    n_seq, _, c = tiled.shape
    n_dec = rows.shape[0]
    return pl.pallas_call(
        _skip_aliased(_copy_rows_kernel, 1), out_shape=jax.ShapeDtypeStruct(tiled.shape, tiled.dtype),
        grid=(n_seq,),
        in_specs=[pl.BlockSpec(memory_space=pl.ANY), pl.BlockSpec((n_dec, c), lambda s: (0, 0))],
        out_specs=pl.BlockSpec((1, n_dec, c), lambda s: (s, seq_len // n_dec, 0)),
        input_output_aliases={0: 0},
        compiler_params=_params("arbitrary"), name="insert_sample_rows",
    )(tiled, rows)


def _resident_row_tile(x_hbm, xbuf_ref, sem, *, row_tiles):
    i = pl.program_id(0)
    j = pl.program_id(1)
    tm = xbuf_ref.shape[1]

    def tile_copy(t):
        rows = pl.ds(pl.multiple_of((t % row_tiles) * tm, 2 * V7X_SUBLANES), tm)
        return pltpu.make_async_copy(x_hbm.at[t // row_tiles, rows, :], xbuf_ref.at[t % 2], sem.at[t % 2])

    @pl.when(j == 0)
    def _():
        @pl.when(i == 0)
        def _():
            tile_copy(0).start()

        @pl.when(i + 1 < pl.num_programs(0))
        def _():
            tile_copy(i + 1).start()

        tile_copy(i).wait()

    return xbuf_ref.at[i % 2]


def _matmul_kernel(*refs, epilogue, has_bias, row_tiles):
    if has_bias:
        x_hbm, w_ref, b_ref, o_ref, xbuf_ref, sem = refs
    else:
        x_hbm, w_ref, o_ref, xbuf_ref, sem = refs
    x_ref = _resident_row_tile(x_hbm, xbuf_ref, sem, row_tiles=row_tiles)
    for r0, r1 in _row_chunks(x_ref.shape[0]):
        acc = _dot_bf16(x_ref[r0:r1], w_ref[...])
        if has_bias:
            acc = acc + b_ref[...]
        o_ref[0, r0:r1] = epilogue(acc).astype(o_ref.dtype)


def matmul(x, w, bias=None, *, layer, row_tiles, tn, n_out, col_block_offset=0, out_dtype=F32,
           epilogue=_identity, name):
    n_seq, rows, k = x.shape
    tm = rows // row_tiles
    in_specs = [pl.BlockSpec(memory_space=pl.ANY),
                pl.BlockSpec((None, k, tn), lambda i, j: (layer, 0, j + col_block_offset))]
    args = [x, w]
    if bias is not None:
        in_specs.append(pl.BlockSpec((None, 1, tn), lambda i, j: (layer, 0, j + col_block_offset)))
        args.append(bias.reshape(bias.shape[0], 1, -1))
    return pl.pallas_call(
        functools.partial(_matmul_kernel, epilogue=epilogue, has_bias=bias is not None,
                          row_tiles=row_tiles),
        out_shape=jax.ShapeDtypeStruct((n_seq, rows, n_out), out_dtype),
        grid=(n_seq * row_tiles, n_out // tn),
        in_specs=in_specs,
        out_specs=pl.BlockSpec((1, tm, tn), lambda i, j: (i // row_tiles, i % row_tiles, j)),
        scratch_shapes=[pltpu.VMEM((2, tm, k), x.dtype), pltpu.SemaphoreType.DMA((2,))],
        compiler_params=_params("arbitrary", "arbitrary"),
        name=name,
    )(*args)


def _matmul_wres_kernel(x_ref, w_hbm, o_ref, wb_ref, stage_ref, sem, *, layer, parts, steps_per_part):
    j = pl.program_id(0)
    i = pl.program_id(1)
    n_col_tiles = pl.num_programs(0)
    _, k, tn = wb_ref.shape
    kp = k // parts

    def part_copy(col_tile, part):
        rows = pl.ds(pl.multiple_of(part * kp, V7X_SUBLANES), kp)
        cols = pl.ds(pl.multiple_of(col_tile * tn, V7X_LANES), tn)
        return pltpu.make_async_copy(w_hbm.at[layer, rows, cols], stage_ref, sem.at[0])

    def round_part(col_tile, part):
        rows = pl.ds(pl.multiple_of(part * kp, 2 * V7X_SUBLANES), kp)
        wb_ref[col_tile % 2, rows, :] = stage_ref[...].astype(BF16)

    @pl.when((j == 0) & (i == 0))
    def _():
        for part in range(parts):
            part_copy(0, part).start()
            part_copy(0, part).wait()
            round_part(0, part)

    part = i // steps_per_part
    phase = i % steps_per_part
    prefetching = (j + 1 < n_col_tiles) & (part < parts)

    @pl.when(prefetching & (phase == 0))
    def _():
        part_copy(j + 1, part).start()

    @pl.when(prefetching & (phase == steps_per_part - 2))
    def _():
        part_copy(j + 1, part).wait()
        round_part(j + 1, part)

    o_ref[0] = jnp.dot(x_ref[0], wb_ref[j % 2], preferred_element_type=F32).astype(o_ref.dtype)


def matmul_weight_resident(x, w, *, layer, tm, tn, parts, name):
    n_seq, rows, k = x.shape
    n = w.shape[2]
    per_seq = rows // tm
    n_row_tiles = n_seq * per_seq
    steps_per_part = n_row_tiles // parts
    assert steps_per_part >= 2 and k % (parts * 2 * V7X_SUBLANES) == 0
    return pl.pallas_call(
        functools.partial(_matmul_wres_kernel, layer=layer, parts=parts, steps_per_part=steps_per_part),
        out_shape=jax.ShapeDtypeStruct((n_seq, rows, n), BF16),
        grid=(n // tn, n_row_tiles),
        in_specs=[pl.BlockSpec((1, tm, k), lambda j, i: (i // per_seq, i % per_seq, 0)),
                  pl.BlockSpec(memory_space=pl.ANY)],
        out_specs=pl.BlockSpec((1, tm, tn), lambda j, i: (i // per_seq, i % per_seq, j)),
        scratch_shapes=[pltpu.VMEM((2, k, tn), BF16), pltpu.VMEM((k // parts, tn), F32),
                        pltpu.SemaphoreType.DMA((1,))],
        compiler_params=_params("arbitrary", "arbitrary"),
        name=name,
    )(x, w)


def _layer_norm(v, g, b):
    xc = v - jnp.mean(v, axis=-1, keepdims=True)
    return xc * lax.rsqrt(jnp.mean(xc * xc, axis=-1, keepdims=True) + NORM_EPS) * g + b


def _spatial_kernel(v_ref, u_ref, lng_ref, lnb_ref, ws_ref, bias_ref, o_ref, gv_ref, *, group_dim):
    t = lax.broadcasted_iota(jnp.int32, (CHUNK, CHUNK), 0)
    s = lax.broadcasted_iota(jnp.int32, (CHUNK, CHUNK), 1)
    causal = s <= t
    for c in range(v_ref.shape[1] // CHUNK):
        rows = slice(c * CHUNK, (c + 1) * CHUNK)
        vn = _layer_norm(v_ref[0, rows], lng_ref[...], lnb_ref[...])
        gv_ref[0] = vn
        vb = vn.astype(BF16)
        for g in range(N_GROUPS):
            cols = slice(g * group_dim, (g + 1) * group_dim)
            wc = jnp.where(causal, ws_ref[g], 0.0).astype(BF16)
            mix = jnp.dot(wc, vb[:, cols], preferred_element_type=F32) + bias_ref[:, cols]
            o_ref[0, rows, cols] = (u_ref[0, rows, cols].astype(F32) * mix).astype(o_ref.dtype)


def _spatial_sample_kernel(v_ref, u_ref, lng_ref, lnb_ref, w00_ref, b0_ref, o_ref, gv_ref):
    vn = _layer_norm(v_ref[0], lng_ref[...], lnb_ref[...])
    gv_ref[0] = vn
    mix = w00_ref[...].astype(BF16).astype(F32) * vn.astype(BF16).astype(F32) + b0_ref[...]
    o_ref[0] = (u_ref[0].astype(F32) * mix).astype(o_ref.dtype)


def spatial_gate(v, u, ln_g, ln_b, w_s, b_s, *, seq_len):
    n_seq, rows, d = v.shape
    n_dec = rows - seq_len
    group_dim = d // N_GROUPS
    ln_g, ln_b = ln_g.reshape(1, d), ln_b.reshape(1, d)
    bias_full = jnp.repeat(b_s.T, group_dim, axis=1)
    block_rows = SPATIAL_CHUNKS_PER_STEP * CHUNK
    row = pl.BlockSpec((1, block_rows, d), lambda s, c: (s, c, 0))
    vec = pl.BlockSpec((1, d), lambda s, c: (0, 0))
    gated, gv_prompt = pl.pallas_call(
        functools.partial(_spatial_kernel, group_dim=group_dim),
        out_shape=(jax.ShapeDtypeStruct(v.shape, BF16),
                   jax.ShapeDtypeStruct((n_seq, CHUNK, d), F32)),
        grid=(n_seq, seq_len // block_rows),
        in_specs=[row, row, vec, vec,
                  pl.BlockSpec((N_GROUPS, CHUNK, CHUNK), lambda s, c: (0, 0, 0)),
                  pl.BlockSpec((CHUNK, d), lambda s, c: (0, 0))],
        out_specs=(row, pl.BlockSpec((1, CHUNK, d), lambda s, c: (s, 0, 0))),
        compiler_params=_params("parallel", "arbitrary"),
        name="spatial_gate_prompt",
    )(v, u, ln_g, ln_b, w_s, bias_full)
    w00 = jnp.repeat(w_s[:, 0, 0], group_dim).reshape(1, d)
    b0 = jnp.repeat(b_s[:, 0], group_dim).reshape(1, d)
    tile0_rows = pl.BlockSpec((1, n_dec, d), lambda s: (0, seq_len // n_dec, 0))
    vec1 = pl.BlockSpec((1, d), lambda s: (0, 0))
    gated, gv_sample = pl.pallas_call(
        _skip_aliased(_spatial_sample_kernel, 1),
        out_shape=(jax.ShapeDtypeStruct(v.shape, BF16), jax.ShapeDtypeStruct((1, n_dec, d), F32)),
        grid=(n_seq,),
        in_specs=[pl.BlockSpec(memory_space=pl.ANY), tile0_rows, tile0_rows, vec1, vec1, vec1, vec1],
        out_specs=(pl.BlockSpec((1, n_dec, d), lambda s: (s, seq_len // n_dec, 0)),
                   pl.BlockSpec((1, n_dec, d), lambda s: (0, 0, 0))),
        input_output_aliases={0: 0},
        compiler_params=_params("arbitrary"),
        name="spatial_gate_sample",
    )(gated, v, u, ln_g, ln_b, w00, b0)
    return gated, gv_prompt, gv_sample


def _pair_block_diag(pair, even):
    lane = lax.broadcasted_iota(jnp.int32, pair.shape, 1)
    if even:
        own = jnp.where(lane < HEAD_DIM, pair, 0.0)
        return jnp.concatenate([own, pltpu.roll(own, HEAD_DIM, 1)], axis=0)
    own = jnp.where(lane >= HEAD_DIM, pair, 0.0)
    return jnp.concatenate([pltpu.roll(own, HEAD_DIM, 1), own], axis=0)


def _attn_prompt_kernel(sink_ref, q_ref, kprev_ref, kown_ref, vprev_ref, vown_ref, o_ref):
    blk = pl.program_id(1)
    n_keys = 2 * WINDOW
    qi = lax.broadcasted_iota(jnp.int32, (WINDOW, n_keys), 0)
    kj = lax.broadcasted_iota(jnp.int32, (WINDOW, n_keys), 1)
    first_key = jnp.where(blk > 0, 0, WINDOW)
    valid = (kj >= qi) & (kj <= qi + WINDOW) & (kj >= first_key)
    col_blocks = Q_PER_KV * HEAD_DIM // V7X_LANES
    valid = jnp.concatenate([valid] * col_blocks, axis=0)
    block_of_row = lax.broadcasted_iota(jnp.int32, (col_blocks * WINDOW, 1), 0) // WINDOW
    head_lane = lax.broadcasted_iota(jnp.int32, (col_blocks * WINDOW, V7X_LANES), 1)
    group_w = Q_PER_KV * HEAD_DIM
    for pair in range(N_KV_HEADS // 2):
        lanes = slice(pair * V7X_LANES, (pair + 1) * V7X_LANES)
        k_pair = jnp.concatenate([kprev_ref[0, :, lanes], kown_ref[0, :, lanes]], axis=0)
        v_pair = jnp.concatenate([vprev_ref[0, :, lanes], vown_ref[0, :, lanes]], axis=0)
        for e in range(2):
            h = 2 * pair + e
            k2 = _pair_block_diag(k_pair, e == 0).astype(BF16)
            v2 = _pair_block_diag(v_pair, e == 0).astype(BF16)
            qs = jnp.concatenate(
                [q_ref[0, :, h * group_w + c * V7X_LANES: h * group_w + (c + 1) * V7X_LANES]
                 for c in range(col_blocks)], axis=0)
            qs = (qs.astype(F32) * (ATTN_SCALE * LOG2_E)).astype(BF16)
            s = lax.dot_general(qs, k2, (((1,), (1,)), ((), ())), preferred_element_type=F32)
            probs, inv_denoms = [], []
            for half in range(2):
                sh = jnp.where(valid, s[:, half * n_keys:(half + 1) * n_keys], MASKED_SCORE)
                sink = jnp.zeros((col_blocks * WINDOW, 1), F32)
                for c in range(col_blocks):
                    sink = jnp.where(block_of_row == c, sink_ref[h * Q_PER_KV + 2 * c + half] * LOG2_E,
                                     sink)
                m = jnp.maximum(jnp.max(sh, axis=-1, keepdims=True), sink)
                p = jnp.exp2(sh - m)
                inv_denoms.append(1.0 / (jnp.sum(p, axis=-1, keepdims=True) + jnp.exp2(sink - m)))
                probs.append(p.astype(BF16))
            o = jnp.dot(jnp.concatenate(probs, axis=1), v2, preferred_element_type=F32)
            o = o * jnp.where(head_lane < HEAD_DIM, inv_denoms[0], inv_denoms[1])
            for c in range(col_blocks):
                o_ref[0, :, h * group_w + c * V7X_LANES: h * group_w + (c + 1) * V7X_LANES] = (
                    o[c * WINDOW:(c + 1) * WINDOW].astype(o_ref.dtype))


def attention_prompt(q, kv, sinks, *, seq_len):
    n_seq, rows, d_q = q.shape
    d_kv = N_KV_HEADS * HEAD_DIM
    k_col, v_col = 0, 1

    def own(col):
        return lambda s, i: (s, i, col)

    def prev(col):
        return lambda s, i: (s, jnp.maximum(i - 1, 0), col)

    return pl.pallas_call(
        _attn_prompt_kernel,
        out_shape=jax.ShapeDtypeStruct((n_seq, rows, d_q), BF16),
        grid=(n_seq, seq_len // WINDOW),
        in_specs=[pl.BlockSpec(memory_space=pltpu.SMEM),
                  pl.BlockSpec((1, WINDOW, d_q), own(0)),
                  pl.BlockSpec((1, WINDOW, d_kv), prev(k_col)),
                  pl.BlockSpec((1, WINDOW, d_kv), own(k_col)),
                  pl.BlockSpec((1, WINDOW, d_kv), prev(v_col)),
                  pl.BlockSpec((1, WINDOW, d_kv), own(v_col))],
        out_specs=pl.BlockSpec((1, WINDOW, d_q), own(0)),
        compiler_params=_params("parallel", "arbitrary"),
        name="attention_prompt",
    )(sinks, q, kv, kv, kv, kv)


def _attn_sample_kernel(q2_ref, ck_ref, cv_ref, knew_ref, vnew_ref, sink_ref, o_ref):
    for pair in range(N_KV_HEADS // 2):
        lanes = slice(pair * V7X_LANES, (pair + 1) * V7X_LANES)
        q2 = q2_ref[0, pair]
        k_pair = ck_ref[0, :, lanes].astype(BF16)
        v_pair = cv_ref[0, :, lanes].astype(BF16)
        k_new = knew_ref[0, :, lanes].astype(BF16).astype(F32)
        v_new = vnew_ref[0, :, lanes].astype(BF16).astype(F32)
        sink = sink_ref[pair][:, :1]
        s = lax.dot_general(q2, k_pair, (((1,), (1,)), ((), ())), preferred_element_type=F32)
        s_new = jnp.sum(q2.astype(F32) * k_new, axis=-1, keepdims=True)
        m = jnp.maximum(jnp.maximum(jnp.max(s, axis=-1, keepdims=True), s_new), sink)
        p = jnp.exp(s - m)
        p_new = jnp.exp(s_new - m)
        denom = jnp.sum(p, axis=-1, keepdims=True) + p_new + jnp.exp(sink - m)
        o = jnp.dot((p / denom).astype(BF16), v_pair, preferred_element_type=F32)
        o_ref[0, pair] = o + (p_new / denom).astype(BF16).astype(F32) * v_new


def attention_sample(q, kv, cache_k, cache_v, sinks):
    n, d_q = q.shape
    d_kv = N_KV_HEADS * HEAD_DIM
    n_pairs = N_KV_HEADS // 2
    rows = 2 * Q_PER_KV
    q = (q.astype(F32) * ATTN_SCALE).astype(BF16).reshape(n, n_pairs, 2, Q_PER_KV, HEAD_DIM)
    zeros = jnp.zeros((n, n_pairs, Q_PER_KV, HEAD_DIM), BF16)
    q2 = jnp.concatenate([jnp.concatenate([q[:, :, 0], zeros], axis=-1),
                          jnp.concatenate([zeros, q[:, :, 1]], axis=-1)], axis=2)
    k_new = kv[:, :d_kv].reshape(n, 1, d_kv)
    v_new = kv[:, d_kv:].reshape(n, 1, d_kv)
    sink2 = jnp.broadcast_to(sinks.reshape(n_pairs, rows, 1), (n_pairs, rows, V7X_LANES))
    o2 = pl.pallas_call(
        _attn_sample_kernel,
        out_shape=jax.ShapeDtypeStruct((n, n_pairs, rows, V7X_LANES), F32),
        grid=(n,),
        in_specs=[pl.BlockSpec((1, n_pairs, rows, V7X_LANES), lambda b: (b, 0, 0, 0)),
                  pl.BlockSpec((1, WINDOW, d_kv), lambda b: (b, 0, 0)),
                  pl.BlockSpec((1, WINDOW, d_kv), lambda b: (b, 0, 0)),
                  pl.BlockSpec((1, 1, d_kv), lambda b: (b, 0, 0)),
                  pl.BlockSpec((1, 1, d_kv), lambda b: (b, 0, 0)),
                  pl.BlockSpec((n_pairs, rows, V7X_LANES), lambda b: (0, 0, 0))],
        out_specs=pl.BlockSpec((1, n_pairs, rows, V7X_LANES), lambda b: (b, 0, 0, 0)),
        compiler_params=_params("parallel"),
        name="attention_sample",
    )(q2, cache_k.reshape(n, WINDOW, d_kv), cache_v.reshape(n, WINDOW, d_kv), k_new, v_new, sink2)
    o = jnp.stack([o2[:, :, :Q_PER_KV, :HEAD_DIM], o2[:, :, Q_PER_KV:, HEAD_DIM:]], axis=2)
    return o.reshape(n, d_q).astype(BF16), k_new, v_new


def _silu_gate(gate, val):
    return gate * (1.0 / (1.0 + jnp.exp(-gate))) * val


def _ffn_up_kernel(x_hbm, wg_ref, wv_ref, cwg_ref, cwv_ref, cbg_ref, cbv_ref,
                   p0g_ref, p0v_ref, p1g_ref, p1v_ref,
                   a_ref, sg_ref, sv_ref, hsg_ref, hsv_ref, xbuf_ref, sem, h_ref, tail_ref,
                   *, halves, n_dec):
    half = pl.program_id(0) % halves
    j = pl.program_id(1)
    tn = wg_ref.shape[1]
    tail_rows = tail_ref.shape[1]
    x_ref = _resident_row_tile(x_hbm, xbuf_ref, sem, row_tiles=halves)
    tm = x_ref.shape[0]
    gate_cols, val_cols = slice(0, tn), slice(tn, 2 * tn)

    @pl.when((pl.program_id(0) == 0) & (j == 0))
    def _():
        tail_ref[...] = jnp.zeros(tail_ref.shape, tail_ref.dtype)

    first_tile = jnp.full((tail_rows, 2 * tn), half, jnp.int32) == 0
    h_ref[0:tail_rows] = jnp.where(first_tile, 0.0, tail_ref[j])

    def conv(r0, r1, cols, cw_ref, cb_ref):
        ext = h_ref[r0:r1 + tail_rows, cols]
        h1 = pltpu.roll(ext, 1, 0)[tail_rows:]
        h2 = pltpu.roll(ext, 2, 0)[tail_rows:]
        return cb_ref[...] + cw_ref[0:1] * h2 + cw_ref[1:2] * h1 + cw_ref[2:3] * ext[tail_rows:]

    def epilogue(r0, r1):
        a_ref[0, r0:r1] = _silu_gate(conv(r0, r1, gate_cols, cwg_ref, cbg_ref),
                                     conv(r0, r1, val_cols, cwv_ref, cbv_ref)).astype(a_ref.dtype)

    pending = None
    for r0, r1 in _row_chunks(tm):
        x = x_ref[r0:r1]
        h_ref[tail_rows + r0:tail_rows + r1, gate_cols] = _dot_bf16(x, wg_ref[...])
        h_ref[tail_rows + r0:tail_rows + r1, val_cols] = _dot_bf16(x, wv_ref[...])
        if pending is not None:
            epilogue(*pending)
        pending = (r0, r1)
    epilogue(*pending)
    tail_ref[j] = h_ref[tm:tm + tail_rows]

    h_sample = h_ref[tail_rows + tm - n_dec:tail_rows + tm]
    hsg_ref[0, 0] = h_sample[:, gate_cols]
    hsv_ref[0, 0] = h_sample[:, val_cols]
    seq_tail = h_ref[tm - n_dec:tm - n_dec + tail_rows]
    sg_ref[0, 0] = seq_tail[:, gate_cols]
    sv_ref[0, 0] = seq_tail[:, val_cols]

    @pl.when(half == halves - 1)
    def _():
        def conv_sample(cols, p0_ref, p1_ref, cw_ref, cb_ref):
            return (cb_ref[...] + cw_ref[0:1] * p0_ref[...] + cw_ref[1:2] * p1_ref[...]
                    + cw_ref[2:3] * h_sample[:, cols])

        a_ref[0, tm - n_dec:tm] = _silu_gate(
            conv_sample(gate_cols, p0g_ref, p1g_ref, cwg_ref, cbg_ref),
            conv_sample(val_cols, p0v_ref, p1v_ref, cwv_ref, cbv_ref)).astype(a_ref.dtype)


def ffn_up(h, w_up, conv_w, conv_b, state, *, layer, seq_len, halves, tn, tail_rows):
    n_seq, rows, k = h.shape
    n_dec = rows - seq_len
    d_ff = w_up.shape[2] // 2
    nj = d_ff // tn
    tm = rows // halves
    conv_b = conv_b.reshape(conv_b.shape[0], 1, -1)
    p0, p1 = state[:, 0], state[:, 1]

    def gate(shape):
        return pl.BlockSpec(shape, lambda i, j: (0, j))

    def val(shape):
        return pl.BlockSpec(shape, lambda i, j: (0, j + nj))

    def gate_l(shape):
        return pl.BlockSpec((None,) + shape, lambda i, j: (layer, 0, j))

    def val_l(shape):
        return pl.BlockSpec((None,) + shape, lambda i, j: (layer, 0, j + nj))

    per_tile = lambda r: pl.BlockSpec((1, 1, r, tn), lambda i, j: (i // halves, i % halves, 0, j))
    a, sg, sv, hsg, hsv = pl.pallas_call(
        functools.partial(_ffn_up_kernel, halves=halves, n_dec=n_dec),
        out_shape=(jax.ShapeDtypeStruct((n_seq, rows, d_ff), BF16),
                   jax.ShapeDtypeStruct((n_seq, halves, tail_rows, d_ff), F32),
                   jax.ShapeDtypeStruct((n_seq, halves, tail_rows, d_ff), F32),
                   jax.ShapeDtypeStruct((n_seq, halves, n_dec, d_ff), F32),
                   jax.ShapeDtypeStruct((n_seq, halves, n_dec, d_ff), F32)),
        grid=(n_seq * halves, nj),
        in_specs=[pl.BlockSpec(memory_space=pl.ANY),
                  gate_l((k, tn)), val_l((k, tn)), gate_l((CONV_W, tn)), val_l((CONV_W, tn)),
                  gate_l((1, tn)), val_l((1, tn)),
                  gate((n_dec, tn)), val((n_dec, tn)), gate((n_dec, tn)), val((n_dec, tn))],
        out_specs=(pl.BlockSpec((1, tm, tn), lambda i, j: (i // halves, i % halves, j)),
                   per_tile(tail_rows), per_tile(tail_rows), per_tile(n_dec), per_tile(n_dec)),
        scratch_shapes=[pltpu.VMEM((2, tm, k), h.dtype), pltpu.SemaphoreType.DMA((2,)),
                        pltpu.VMEM((tail_rows + tm, 2 * tn), F32),
                        pltpu.VMEM((nj, tail_rows, 2 * tn), F32)],
        compiler_params=_params("arbitrary", "arbitrary"),
        name="ffn_up",
    )(h, w_up, w_up, conv_w, conv_w, conv_b, conv_b, p0, p0, p1, p1)
    keep = slice(tail_rows - (CONV_W - 1), tail_rows)
    state_prompt = jnp.concatenate([sg[:, halves - 1, keep], sv[:, halves - 1, keep]], axis=-1)
    h_sample = jnp.concatenate([hsg[0, halves - 1], hsv[0, halves - 1]], axis=-1)
    return a, state_prompt, h_sample


ROW_TILES = 2
TN = 512
TN_FFN_UP = V7X_MXU_COLS
FFN_UP_ROW_TILES = 1
TR_PROMPT = 512
TILED_ROW_BLOCKS = 5
FFN_DOWN_ROW_BLOCKS = 5
FFN_DOWN_WEIGHT_PARTS = 4
STATE_TAIL_ROWS = V7X_SUBLANES
SPATIAL_CHUNKS_PER_STEP = 2


def kernel(x_prompt, x_sample, cache_win_k, cache_win_v, state_conv, norm_mix_pre, norm_mix_post,
           norm_ffn_pre, norm_ffn_post, gmlp_w_in, gmlp_ln_g, gmlp_ln_b, gmlp_w_s, gmlp_b_s,
           gmlp_w_out, attn_w_qkv, attn_b_qkv, attn_sinks, attn_w_o, attn_b_o, ffn_w_up,
           ffn_conv_w, ffn_conv_b, ffn_w_down):
    n_seq, seq_len, d = x_prompt.shape
    n_dec = x_sample.shape[0]
    depth = norm_mix_pre.shape[0]
    rows = seq_len + n_dec
    x_sample = x_sample.reshape(1, n_dec, d)
    mm = functools.partial(matmul, row_tiles=ROW_TILES, tn=TN)

    h = rmsnorm_first(x_prompt, x_sample, norm_mix_pre[0], tr=TR_PROMPT)
    x = None
    gv_p, gv_s, wk_p, wv_p, wk_s, wv_s, cv_p, cv_s = [], [], [], [], [], [], [], []
    for layer in range(depth):
        idx = layer // 2
        if layer % 2 == 0:
            d_g = gmlp_w_out.shape[1]
            u = mm(h, gmlp_w_in, layer=idx, n_out=d_g, out_dtype=BF16, epilogue=_gelu, name="gmlp_in_u")
            v = mm(h, gmlp_w_in, layer=idx, n_out=d_g, col_block_offset=d_g // TN, epilogue=_gelu,
                   name="gmlp_in_v")
            gated, g_p, g_s = spatial_gate(v, u, gmlp_ln_g[idx], gmlp_ln_b[idx], gmlp_w_s[idx],
                                           gmlp_b_s[idx], seq_len=seq_len)
            gv_p.append(g_p)
            gv_s.append(g_s.reshape(n_dec, 1, d_g))
            y = mm(gated, gmlp_w_out, layer=idx, n_out=d, out_dtype=BF16, name="gmlp_out")
        else:
            d_q = attn_w_o.shape[1]
            d_qkv = attn_w_qkv.shape[2]
            d_kv = (d_qkv - d_q) // 2
            q = mm(h, attn_w_qkv, attn_b_qkv, layer=idx, n_out=d_q, out_dtype=BF16, name="attn_q")
            kv = mm(h, attn_w_qkv, attn_b_qkv, layer=idx, n_out=2 * d_kv, col_block_offset=d_q // TN,
                    name="attn_kv")
            o = attention_prompt(q, kv, attn_sinks[idx], seq_len=seq_len)
            o_s, k_new, v_new = attention_sample(q[0, seq_len:], kv[0, seq_len:], cache_win_k[idx],
                                                 cache_win_v[idx], attn_sinks[idx])
            o = insert_sample_rows(o, o_s, seq_len=seq_len)
            tail = kv[:, seq_len - WINDOW:seq_len]
            wk_p.append(tail[:, :, :d_kv].reshape(n_seq, WINDOW, N_KV_HEADS, HEAD_DIM))
            wv_p.append(tail[:, :, d_kv:].reshape(n_seq, WINDOW, N_KV_HEADS, HEAD_DIM))
            wk_s.append(jnp.concatenate(
                [cache_win_k[idx][:, 1:], k_new.reshape(n_dec, 1, N_KV_HEADS, HEAD_DIM)], axis=1))
            wv_s.append(jnp.concatenate(
                [cache_win_v[idx][:, 1:], v_new.reshape(n_dec, 1, N_KV_HEADS, HEAD_DIM)], axis=1))
            y = mm(o, attn_w_o, attn_b_o, layer=idx, n_out=d, out_dtype=BF16, name="attn_out")
        if x is None:
            x, h = resnorm_first(x_prompt, x_sample, y, norm_mix_post[layer], norm_ffn_pre[layer],
                                 tr=TR_PROMPT)
        else:
            x, h = resnorm_tiled(x, y, norm_mix_post[layer], norm_ffn_pre[layer],
                                 tr=rows // TILED_ROW_BLOCKS)

        a, c_p, hu_s = ffn_up(h, ffn_w_up, ffn_conv_w, ffn_conv_b, state_conv[layer], layer=layer,
                              seq_len=seq_len, halves=FFN_UP_ROW_TILES, tn=TN_FFN_UP,
                              tail_rows=STATE_TAIL_ROWS)
        cv_p.append(c_p)
        cv_s.append(jnp.concatenate([state_conv[layer][:, 1:], hu_s[:, None]], axis=1))
        f = matmul_weight_resident(a, ffn_w_down, layer=layer, tm=rows // FFN_DOWN_ROW_BLOCKS, tn=TN,
                                   parts=FFN_DOWN_WEIGHT_PARTS, name="ffn_down")
        if layer + 1 < depth:
            x, h = resnorm_tiled(x, f, norm_ffn_post[layer], norm_mix_pre[layer + 1],
                                 tr=rows // TILED_ROW_BLOCKS)
        else:
            y_prompt, y_sample = resnorm_last(x, f, norm_ffn_post[layer], seq_len=seq_len, n_dec=n_dec,
                                              tr=TR_PROMPT)

    return (y_prompt, y_sample, jnp.stack(gv_p), jnp.stack(gv_s),
            jnp.stack(wk_p), jnp.stack(wv_p), jnp.stack(wk_s), jnp.stack(wv_s),
            jnp.stack(cv_p), jnp.stack(cv_s))
```
